```python
import math
import jax, jax.numpy as jnp
from jax import lax
import numpy as np

D_MODEL = 1024
BATCH = 8
SEQ = 2048
DEPTH = 2
DEC_BATCH = 128
DEC_SEQ = 4
PAST_LEN = 16384
PAGE_SIZE = 128

N_HEADS = 4
HEAD_DIM = 128
MIX_W = N_HEADS * HEAD_DIM
N_BRANCH = 3
CONV_W = 4
FFN_CONV_W = 3
D_FF = 2816
CHUNK = 64
EPS = 1e-6
IN_SPLITS = (3 * MIX_W, MIX_W, N_HEADS, N_HEADS,
             MIX_W, MIX_W, MIX_W, MIX_W, N_HEADS, N_HEADS,
             MIX_W, MIX_W, MIX_W, MIX_W,
             N_BRANCH * D_MODEL)
N_IN = sum(IN_SPLITS)

kernel_name = "gdn_mlstm_hgrn2_parallel_hybrid_step"


def _rmsnorm(x, g):
    xf = x.astype(jnp.float32)
    return (xf * lax.rsqrt(jnp.mean(xf * xf, -1, keepdims=True) + EPS) * g).astype(x.dtype)


def _group_rmsnorm(t, g):
    B, L, _ = t.shape
    th = t.reshape(B, L, N_HEADS, HEAD_DIM)
    th = th * lax.rsqrt(jnp.mean(th * th, -1, keepdims=True) + EPS)
    return th.reshape(B, L, MIX_W) * g.astype(jnp.float32)


def _l2norm(t):
    return t * lax.rsqrt(jnp.sum(t * t, -1, keepdims=True) + EPS)


def _heads(t):
    B, L, _ = t.shape
    return t.astype(jnp.float32).reshape(B, L, N_HEADS, HEAD_DIM).transpose(0, 2, 1, 3)


def _merge(t):
    B, H, L, D = t.shape
    return t.transpose(0, 2, 1, 3).reshape(B, L, H * D)


def _chunk_len(L):
    return CHUNK if L % CHUNK == 0 else L


def _to_chunks(t, c):
    B, H, L = t.shape[:3]
    return jnp.moveaxis(t.reshape(B, H, L // c, c, *t.shape[3:]), 2, 0)


def _from_chunks(t):
    Nc, B, H, c = t.shape[:4]
    return jnp.moveaxis(t, 0, 2).reshape(B, H, Nc * c, *t.shape[4:])


def _causal_conv(x, buf, w):
    W = w.shape[0]
    L = x.shape[1]
    xp = jnp.concatenate([buf.astype(x.dtype), x], axis=1)
    y = sum(w[j] * xp[:, j:j + L] for j in range(W))
    return y, xp[:, L:]


def _gated_delta(q, k, v, beta, g, S0):
    c = _chunk_len(q.shape[2])
    causal = jnp.tril(jnp.ones((c, c), bool))
    strict = jnp.tril(jnp.ones((c, c), bool), -1)
    eye = jnp.eye(c, dtype=jnp.float32)

    def step(S, inp):
        qc, kc, vc, bc, gc = inp
        G = jnp.cumsum(gc, -1)
        decay = jnp.exp(jnp.where(causal, G[..., :, None] - G[..., None, :], -jnp.inf))
        A = jnp.where(strict, bc[..., None] * decay * jnp.einsum('bhtd,bhsd->bhts', kc, kc), 0.0)
        eG = jnp.exp(G)[..., None]
        rhs = bc[..., None] * (vc - eG * jnp.einsum('bhtd,bhde->bhte', kc, S))
        u = lax.linalg.triangular_solve(eye + A, rhs, left_side=True, lower=True)
        qk = jnp.einsum('bhtd,bhsd->bhts', qc, kc) * decay
        o = eG * jnp.einsum('bhtd,bhde->bhte', qc, S) + jnp.einsum('bhts,bhse->bhte', qk, u)
        wl = jnp.exp(G[..., -1:] - G)[..., None]
        S_new = jnp.exp(G[..., -1])[..., None, None] * S + jnp.einsum('bhsd,bhse->bhde', kc * wl, u)
        return S_new, o

    S, o = lax.scan(step, S0, tuple(_to_chunks(t, c) for t in (q, k, v, beta, g)))
    return _from_chunks(o), S


def _mlstm(q, k, v, ig, lf, C0, n0, m0):
    c = _chunk_len(q.shape[2])
    causal = jnp.tril(jnp.ones((c, c), bool))

    def step(carry, inp):
        C, n, m = carry
        qc, kc, vc, ic, fc = inp
        F = jnp.cumsum(fc, -1)
        logD = jnp.where(causal, F[..., :, None] - F[..., None, :] + ic[..., None, :], -jnp.inf)
        b = F + m[..., None]
        mt = jnp.maximum(b, jnp.max(logD, -1))
        s = jnp.einsum('bhtd,bhsd->bhts', qc, kc) * jnp.exp(logD - mt[..., None])
        inter = jnp.exp(b - mt)
        num = jnp.einsum('bhts,bhse->bhte', s, vc) + inter[..., None] * jnp.einsum('bhtd,bhde->bhte', qc, C)
        den = jnp.sum(s, -1) + inter * jnp.einsum('bhtd,bhd->bht', qc, n)
        h = num / jnp.maximum(jnp.abs(den), jnp.exp(-mt))[..., None]
        m_new = mt[..., -1]
        wl = jnp.exp(F[..., -1:] - F + ic - m_new[..., None])
        d0 = jnp.exp(F[..., -1] + m - m_new)
        C_new = d0[..., None, None] * C + jnp.einsum('bhsd,bhse->bhde', kc * wl[..., None], vc)
        n_new = d0[..., None] * n + jnp.einsum('bhs,bhsd->bhd', wl, kc)
        return (C_new, n_new, m_new), h

    state, h = lax.scan(step, (C0, n0, m0), tuple(_to_chunks(t, c) for t in (q, k, v, ig, lf)))
    return _from_chunks(h), state


def _hgrn2(q, k, lg, i, S0):
    c = _chunk_len(q.shape[2])
    causal = jnp.tril(jnp.ones((c, c), bool))[..., None]

    def step(S, inp):
        qc, kc, lgc, ic = inp
        G = jnp.cumsum(lgc, axis=2)
        decay = jnp.exp(jnp.where(causal, G[:, :, :, None, :] - G[:, :, None, :, :], -jnp.inf))
        A = jnp.einsum('bhtsd,bhsd->bhts', qc[:, :, :, None, :] * decay, kc)
        o = jnp.einsum('bhtd,bhde->bhte', qc * jnp.exp(G), S) + jnp.einsum('bhts,bhse->bhte', A, ic)
        S_new = jnp.exp(G[:, :, -1])[..., None] * S + jnp.einsum('bhsd,bhse->bhde', kc * jnp.exp(G[:, :, -1:] - G), ic)
        return S_new, o

    S, o = lax.scan(step, S0, tuple(_to_chunks(t, c) for t in (q, k, lg, i)))
    return _from_chunks(o), S


def _layer(x, gdn_S, gdn_conv, m_C, m_n, m_m, h_S, ffn_conv,
           ln_mix, w_in, gdn_conv_w, gdn_A_log, gdn_dt_bias, gdn_norm,
           m_ibias, m_fbias, m_norm, lb, hgrn_norm, w_br, w_out,
           ln_ffn, w_up, ffn_conv_w, ffn_conv_b, w_down):
    f32 = jnp.float32
    B, L, _ = x.shape
    h = _rmsnorm(x, ln_mix)
    proj = h @ w_in
    (g_qkv, g_z, g_b, g_a, m_q, m_k, m_v, m_o, m_i, m_f,
     h_q, h_f, h_i, h_g, br_gate) = jnp.split(proj, np.cumsum(IN_SPLITS)[:-1].tolist(), axis=-1)

    g_qkv, gdn_conv_new = _causal_conv(g_qkv, gdn_conv, gdn_conv_w)
    gq, gk, gv = jnp.split(jax.nn.silu(g_qkv.astype(f32)), 3, axis=-1)
    gq = _l2norm(_heads(gq)) * HEAD_DIM ** -0.5
    gk = _l2norm(_heads(gk))
    beta = jax.nn.sigmoid(g_b.astype(f32)).transpose(0, 2, 1)
    g_log = (-jnp.exp(gdn_A_log.astype(f32)) * jax.nn.softplus(g_a.astype(f32) + gdn_dt_bias)).transpose(0, 2, 1)
    o, gdn_S_new = _gated_delta(gq, gk, _heads(gv), beta, g_log, gdn_S.astype(f32))
    o_gdn = _group_rmsnorm(_merge(o), gdn_norm) * jax.nn.silu(g_z.astype(f32))

    ig = (m_i.astype(f32) + m_ibias).transpose(0, 2, 1)
    lf = jax.nn.log_sigmoid(m_f.astype(f32) + m_fbias).transpose(0, 2, 1)
    hm, (m_C_new, m_n_new, m_m_new) = _mlstm(_heads(m_q), _heads(m_k) * HEAD_DIM ** -0.5, _heads(m_v),
                                             ig, lf, m_C.astype(f32), m_n.astype(f32), m_m.astype(f32))
    o_m = _group_rmsnorm(_merge(hm), m_norm) * jax.nn.sigmoid(m_o.astype(f32))

    fg = lb + (1.0 - lb) * jax.nn.sigmoid(h_f.astype(f32))
    ho, h_S_new = _hgrn2(_heads(jax.nn.silu(h_q.astype(f32))), _heads(1.0 - fg), _heads(jnp.log(fg)),
                         _heads(h_i), h_S.astype(f32))
    o_h = _group_rmsnorm(_merge(ho), hgrn_norm) * jax.nn.silu(h_g.astype(f32))

    outs = jnp.stack([o_gdn, o_m, o_h], axis=0).astype(x.dtype)
    br = jnp.einsum('nblc,ncd->blnd', outs, w_br)
    gate = jax.nn.sigmoid(br_gate.reshape(B, L, N_BRANCH, D_MODEL))
    x = x + (jnp.sum(gate * br, axis=2) @ w_out).astype(x.dtype)

    u = _rmsnorm(x, ln_ffn) @ w_up
    u, ffn_conv_new = _causal_conv(u, ffn_conv, ffn_conv_w)
    ua, ub = jnp.split(u + ffn_conv_b, 2, axis=-1)
    x = x + ((jax.nn.silu(ua) * ub) @ w_down).astype(x.dtype)

    dt = x.dtype
    new = (gdn_S_new.astype(dt), gdn_conv_new.astype(dt), m_C_new.astype(dt), m_n_new.astype(dt),
           m_m_new.astype(dt), h_S_new.astype(dt), ffn_conv_new.astype(dt))
    return x, new


def _trunk(x, gdn_S, gdn_conv, m_C, m_n, m_m, h_S, ffn_conv,
           ln_mix, w_in, gdn_conv_w, gdn_A_log, gdn_dt_bias, gdn_norm,
           m_ibias, m_fbias, m_norm, hgrn_lb, hgrn_norm, w_br, w_out,
           ln_ffn, w_up, ffn_conv_w, ffn_conv_b, w_down, ln_final):
    lb_all = jnp.cumsum(jax.nn.softmax(hgrn_lb.astype(jnp.float32), axis=0), axis=0)
    lb_all = lb_all - lb_all[0]
    per_layer = []
    for l in range(DEPTH):
        x, st = _layer(x, gdn_S[l], gdn_conv[l], m_C[l], m_n[l], m_m[l], h_S[l], ffn_conv[l],
                       ln_mix[l], w_in[l], gdn_conv_w[l], gdn_A_log[l], gdn_dt_bias[l], gdn_norm[l],
                       m_ibias[l], m_fbias[l], m_norm[l], lb_all[l], hgrn_norm[l], w_br[l], w_out[l],
                       ln_ffn[l], w_up[l], ffn_conv_w[l], ffn_conv_b[l], w_down[l])
        per_layer.append(st)
    stacked = tuple(jnp.stack(s, axis=0) for s in zip(*per_layer))
    return _rmsnorm(x, ln_final), stacked


def setup_inputs(seed: int = 0) -> dict:
    key = jax.random.key(seed)
    ks = jax.random.split(key, 32)
    f32 = jnp.float32
    H, Dh = N_HEADS, HEAD_DIM

    def nrm(k, shape, s):
        return jax.random.normal(k, shape, f32) * s

    dt = jnp.exp(jax.random.uniform(ks[12], (DEPTH, H), f32) * (math.log(0.1) - math.log(0.001)) + math.log(0.001))
    return {
        'x_prompt': nrm(ks[0], (BATCH, SEQ, D_MODEL), 1.0),
        'x_sample': nrm(ks[1], (DEC_BATCH, DEC_SEQ, D_MODEL), 1.0),
        'state_gdn_S': nrm(ks[2], (DEPTH, DEC_BATCH, H, Dh, Dh), 0.05),
        'state_gdn_conv': nrm(ks[3], (DEPTH, DEC_BATCH, CONV_W - 1, 3 * MIX_W), 1.0),
        'state_mlstm_C': nrm(ks[4], (DEPTH, DEC_BATCH, H, Dh, Dh), 0.05),
        'state_mlstm_n': nrm(ks[5], (DEPTH, DEC_BATCH, H, Dh), 0.1),
        'state_mlstm_m': nrm(ks[6], (DEPTH, DEC_BATCH, H), 1.0),
        'state_hgrn_S': nrm(ks[7], (DEPTH, DEC_BATCH, H, Dh, Dh), 0.5),
        'state_ffn_conv': nrm(ks[8], (DEPTH, DEC_BATCH, FFN_CONV_W - 1, 2 * D_FF), 1.0),
        'ln_mix': 1.0 + nrm(ks[9], (DEPTH, D_MODEL), 0.02),
        'w_in': nrm(ks[10], (DEPTH, D_MODEL, N_IN), D_MODEL ** -0.5),
        'gdn_conv_w': nrm(ks[11], (DEPTH, CONV_W, 3 * MIX_W), CONV_W ** -0.5),
        'gdn_A_log': jnp.log(jax.random.uniform(ks[13], (DEPTH, H), f32, 1.0, 16.0)),
        'gdn_dt_bias': dt + jnp.log(-jnp.expm1(-dt)),
        'gdn_norm': 1.0 + nrm(ks[14], (DEPTH, MIX_W), 0.02),
        'm_ibias': nrm(ks[15], (DEPTH, H), 0.1),
        'm_fbias': 3.0 + 3.0 * jax.random.uniform(ks[16], (DEPTH, H), f32),
        'm_norm': 1.0 + nrm(ks[17], (DEPTH, MIX_W), 0.02),
        'hgrn_lb': nrm(ks[18], (DEPTH, MIX_W), 0.1),
        'hgrn_norm': 1.0 + nrm(ks[19], (DEPTH, MIX_W), 0.02),
        'w_br': nrm(ks[20], (DEPTH, N_BRANCH, MIX_W, D_MODEL), MIX_W ** -0.5),
        'w_out': nrm(ks[21], (DEPTH, D_MODEL, D_MODEL), D_MODEL ** -0.5),
        'ln_ffn': 1.0 + nrm(ks[22], (DEPTH, D_MODEL), 0.02),
        'w_up': nrm(ks[23], (DEPTH, D_MODEL, 2 * D_FF), D_MODEL ** -0.5),
        'ffn_conv_w': nrm(ks[24], (DEPTH, FFN_CONV_W, 2 * D_FF), FFN_CONV_W ** -0.5),
        'ffn_conv_b': nrm(ks[25], (DEPTH, 2 * D_FF), 0.01),
        'w_down': nrm(ks[26], (DEPTH, D_FF, D_MODEL), D_FF ** -0.5),
        'ln_final': 1.0 + nrm(ks[27], (D_MODEL,), 0.02),
    }


def reference(x_prompt, x_sample, state_gdn_S, state_gdn_conv, state_mlstm_C, state_mlstm_n,
              state_mlstm_m, state_hgrn_S, state_ffn_conv, ln_mix, w_in, gdn_conv_w, gdn_A_log,
              gdn_dt_bias, gdn_norm, m_ibias, m_fbias, m_norm, hgrn_lb, hgrn_norm, w_br, w_out,
              ln_ffn, w_up, ffn_conv_w, ffn_conv_b, w_down, ln_final):
    weights = (ln_mix, w_in, gdn_conv_w, gdn_A_log, gdn_dt_bias, gdn_norm, m_ibias, m_fbias, m_norm,
               hgrn_lb, hgrn_norm, w_br, w_out, ln_ffn, w_up, ffn_conv_w, ffn_conv_b, w_down, ln_final)
    B = x_prompt.shape[0]
    dt = x_prompt.dtype
    H, Dh = N_HEADS, HEAD_DIM
    z_S = jnp.zeros((DEPTH, B, H, Dh, Dh), dt)
    z_gconv = jnp.zeros((DEPTH, B, CONV_W - 1, 3 * MIX_W), dt)
    z_n = jnp.zeros((DEPTH, B, H, Dh), dt)
    z_m = jnp.zeros((DEPTH, B, H), dt)
    z_fconv = jnp.zeros((DEPTH, B, FFN_CONV_W - 1, 2 * D_FF), dt)
    y_prompt, (p_gdn_S, p_gdn_conv, p_mlstm_C, p_mlstm_n, p_mlstm_m, p_hgrn_S, p_ffn_conv) = _trunk(
        x_prompt, z_S, z_gconv, z_S, z_n, z_m, z_S, z_fconv, *weights)
    y_sample, (s_gdn_S, s_gdn_conv, s_mlstm_C, s_mlstm_n, s_mlstm_m, s_hgrn_S, s_ffn_conv) = _trunk(
        x_sample, state_gdn_S, state_gdn_conv, state_mlstm_C, state_mlstm_n, state_mlstm_m,
        state_hgrn_S, state_ffn_conv, *weights)
    return (y_prompt, y_sample,
            p_gdn_S, p_gdn_conv, p_mlstm_C, p_mlstm_n, p_mlstm_m, p_hgrn_S, p_ffn_conv,
            s_gdn_S, s_gdn_conv, s_mlstm_C, s_mlstm_n, s_mlstm_m, s_hgrn_S, s_ffn_conv)
```

```python
import functools
import math

import jax
import jax.numpy as jnp
from jax import lax
from jax.experimental import pallas as pl
from jax.experimental.pallas import tpu as pltpu

D_MODEL = 1024
N_HEADS = 4
HEAD_DIM = 128
MIX_W = N_HEADS * HEAD_DIM
N_BRANCH = 3
CONV_W = 4
FFN_CONV_W = 3
D_FF = 2816
CHUNK = 64
EPS = 1e-6
N_MIX = 6144
N_MAIN = 9216
SAMPLE_PAD = 8
SUB = 16
FFN_COLS = 256

F32 = jnp.float32
BF16 = jnp.bfloat16
HIGHEST = lax.Precision.HIGHEST
VMEM_LIMIT = 56 * 1024 * 1024


def _mm(a, b):
    return jnp.dot(a.astype(BF16), b.astype(BF16), preferred_element_type=F32)


def _mm_nt(a, b):
    return lax.dot_general(a.astype(BF16), b.astype(BF16), (((1,), (1,)), ((), ())),
                           preferred_element_type=F32)


def _mm_tn(a, b):
    return lax.dot_general(a.astype(BF16), b.astype(BF16), (((0,), (0,)), ((), ())),
                           preferred_element_type=F32)


def _mm_hi(a, b):
    return jnp.dot(a, b, precision=HIGHEST, preferred_element_type=F32)


def _rms(x, g):
    return x * lax.rsqrt(jnp.mean(x * x, -1, keepdims=True) + EPS) * g


def _params(n_axes):
    return pltpu.CompilerParams(dimension_semantics=("arbitrary",) * n_axes,
                                vmem_limit_bytes=VMEM_LIMIT)


def _in_kernel(x_ref, ln_ref, wm_ref, ws_ref, pm_ref, ps_ref, h_ref):
    @pl.when(pl.program_id(1) == 0)
    def _():
        hb = _rms(x_ref[...], ln_ref[...]).astype(BF16)
        h_ref[...] = hb
        ps_ref[...] = jnp.dot(hb, ws_ref[...], preferred_element_type=F32)

    pm_ref[...] = jnp.dot(h_ref[...], wm_ref[...], preferred_element_type=F32)


def _in_proj(x, ln, w_main, w_small, tm, tn=1024):
    T = x.shape[0]
    return pl.pallas_call(
        _in_kernel,
        grid=(T // tm, N_MAIN // tn),
        in_specs=[pl.BlockSpec((tm, D_MODEL), lambda i, j: (i, 0)),
                  pl.BlockSpec((1, D_MODEL), lambda i, j: (0, 0)),
                  pl.BlockSpec((D_MODEL, tn), lambda i, j: (0, j)),
                  pl.BlockSpec((D_MODEL, 128), lambda i, j: (0, 0))],
        out_specs=[pl.BlockSpec((tm, tn), lambda i, j: (i, j)),
                   pl.BlockSpec((tm, 128), lambda i, j: (i, 0))],
        out_shape=[jax.ShapeDtypeStruct((T, N_MAIN), F32),
                   jax.ShapeDtypeStruct((T, 128), F32)],
        scratch_shapes=[pltpu.VMEM((tm, D_MODEL), BF16)],
        compiler_params=_params(2),
        name="in_proj",
    )(x, ln, w_main, w_small)


def _mix_kernel(pm_ref, ps_ref, gS_ref, gconv_ref, mC_ref, mn_ref, mm_ref, hS_ref,
                cw_ref, sp_ref, nrm_ref, lb_ref,
                om_ref, gS_o, mC_o, mn_o, mm_o, hS_o,
                xpad_ref, *, bb, c, lv):
    @pl.when(pl.program_id(1) == 0)
    def _():
        gS_o[...] = gS_ref[...]
        mC_o[...] = mC_ref[...]
        mn_o[...] = mn_ref[...]
        mm_o[...] = mm_ref[...]
        hS_o[...] = hS_ref[...]
        xpad_ref[:, 5:8, :] = gconv_ref[...]

    r2 = lax.broadcasted_iota(jnp.int32, (c, c), 0)
    s2 = lax.broadcasted_iota(jnp.int32, (c, c), 1)
    causal = r2 >= s2
    strict = r2 > s2
    tri = causal.astype(F32)
    eye = (r2 == s2).astype(F32)
    rowc = lax.broadcasted_iota(jnp.int32, (c, 1), 0)
    live = rowc < lv
    lane = lax.broadcasted_iota(jnp.int32, (c, 128), 1)
    sub = min(SUB, c)
    nblk = c // sub
    n_sq = int(math.log2(c)) - 1

    alog = sp_ref[0:1, :]
    bias = sp_ref[1:2, :]

    for bi in range(bb):
        ps = ps_ref[bi]
        z = ps + bias
        gates = jnp.where(lane < 4, jax.nn.sigmoid(ps),
                          jnp.where(lane < 8, -jnp.exp(alog) * jax.nn.softplus(z),
                                    jnp.where(lane < 12, z, jax.nn.log_sigmoid(z))))
        cum = _mm_hi(tri, gates)
        gates_t = gates.T
        cum_t = cum.T

        xpad_ref[bi, 8:8 + c, :] = pm_ref[bi, :, 0:3 * MIX_W]

        for h in range(N_HEADS):
            hs = slice(h * HEAD_DIM, (h + 1) * HEAD_DIM)

            def conv(col0):
                acc = cw_ref[0:1, col0:col0 + HEAD_DIM] * xpad_ref[bi, 5:5 + c, col0:col0 + HEAD_DIM]
                for j in range(1, CONV_W):
                    acc = acc + cw_ref[j:j + 1, col0:col0 + HEAD_DIM] * xpad_ref[bi, 5 + j:5 + j + c, col0:col0 + HEAD_DIM]
                return jax.nn.silu(acc)

            q = conv(h * HEAD_DIM)
            k = conv(MIX_W + h * HEAD_DIM)
            v = conv(2 * MIX_W + h * HEAD_DIM)
            q = q * lax.rsqrt(jnp.sum(q * q, -1, keepdims=True) + EPS) * HEAD_DIM ** -0.5
            k = k * lax.rsqrt(jnp.sum(k * k, -1, keepdims=True) + EPS)
            beta = gates[:, h:h + 1]
            Gc = cum[:, 4 + h:5 + h]
            Gr = cum_t[4 + h:5 + h, :]
            decay = jnp.exp(jnp.where(causal, Gc - Gr, -jnp.inf))
            S = gS_o[bi, h]
            qk2 = jnp.concatenate([q, k], axis=0)
            P = _mm_nt(qk2, k)
            R = _mm(qk2, S)
            QK, KK = P[:c], P[c:]
            QS, KS = R[:c], R[c:]
            eG = jnp.exp(Gc)
            rhs = beta * (v - eG * KS)
            Mp = -jnp.where(strict, beta * decay * KK, 0.0)
            Tinv = eye + Mp
            for _ in range(n_sq):
                Mp = _mm_hi(Mp, Mp)
                Tinv = Tinv + _mm_hi(Tinv, Mp)
            u = _mm_hi(Tinv, rhs)
            o = eG * QS + _mm(QK * decay, u)
            Gl = cum[lv - 1:lv, 4 + h:5 + h]
            wl = jnp.where(live, jnp.exp(Gl - Gc), 0.0)
            gS_o[bi, h] = jnp.exp(Gl) * S + _mm_tn(k * wl, u)
            zg = pm_ref[bi, :, 3 * MIX_W + h * HEAD_DIM:3 * MIX_W + (h + 1) * HEAD_DIM]
            om_ref[bi, :, hs] = _rms(o, nrm_ref[0:1, hs]) * jax.nn.silu(zg)

            base = 4 * MIX_W + h * HEAD_DIM
            q = pm_ref[bi, :, base:base + HEAD_DIM]
            k = pm_ref[bi, :, base + MIX_W:base + MIX_W + HEAD_DIM] * HEAD_DIM ** -0.5
            v = pm_ref[bi, :, base + 2 * MIX_W:base + 2 * MIX_W + HEAD_DIM]
            og = pm_ref[bi, :, base + 3 * MIX_W:base + 3 * MIX_W + HEAD_DIM]
            ig_c = gates[:, 8 + h:9 + h]
            ig_r = gates_t[8 + h:9 + h, :]
            Fc = cum[:, 12 + h:13 + h]
            Fr = cum_t[12 + h:13 + h, :]
            m0 = mm_o[bi, h:h + 1, 0:1]
            C = mC_o[bi, h]
            n_row = mn_o[bi, h:h + 1, :]
            logD = jnp.where(causal, Fc - Fr + ig_r, -jnp.inf)
            b = Fc + m0
            mt = jnp.maximum(b, jnp.max(logD, -1, keepdims=True))
            s = _mm_nt(q, k) * jnp.exp(logD - mt)
            inter = jnp.exp(b - mt)
            num = _mm(s, v) + inter * _mm(q, C)
            den = jnp.sum(s, -1, keepdims=True) + inter * jnp.sum(q * n_row, -1, keepdims=True)
            hm = num / jnp.maximum(jnp.abs(den), jnp.exp(-mt))
            m_new = mt[lv - 1:lv, :]
            Fl = cum[lv - 1:lv, 12 + h:13 + h]
            wl = jnp.where(live, jnp.exp(Fl - Fc + ig_c - m_new), 0.0)
            d0 = jnp.exp(Fl + m0 - m_new)
            kw = k * wl
            mC_o[bi, h] = d0 * C + _mm_tn(kw, v)
            mn_o[bi, h:h + 1, :] = d0 * n_row + jnp.sum(kw, axis=0, keepdims=True)
            mm_o[bi, h:h + 1, :] = jnp.broadcast_to(m_new, (1, HEAD_DIM))
            om_ref[bi, :, MIX_W + h * HEAD_DIM:MIX_W + (h + 1) * HEAD_DIM] = (
                _rms(hm, nrm_ref[1:2, hs]) * jax.nn.sigmoid(og))

            base = 8 * MIX_W + h * HEAD_DIM
            hq = jax.nn.silu(pm_ref[bi, :, base:base + HEAD_DIM])
            hf = pm_ref[bi, :, base + MIX_W:base + MIX_W + HEAD_DIM]
            hi = pm_ref[bi, :, base + 2 * MIX_W:base + 2 * MIX_W + HEAD_DIM]
            hg = pm_ref[bi, :, base + 3 * MIX_W:base + 3 * MIX_W + HEAD_DIM]
            lbh = lb_ref[0:1, hs]
            fg = lbh + (1.0 - lbh) * jax.nn.sigmoid(hf)
            kk = 1.0 - fg
            G = _mm_hi(tri, jnp.log(fg))
            S = hS_o[bi, h]
            blk0 = jnp.bitwise_and(rowc, -sub)
            A = jnp.zeros((c, c), F32)
            for dlt in range(sub):
                ks = kk if dlt == 0 else pltpu.roll(kk, dlt, 0)
                Gs = G if dlt == 0 else pltpu.roll(G, dlt, 0)
                d = jnp.sum(hq * ks * jnp.exp(jnp.minimum(G - Gs, 0.0)), -1, keepdims=True)
                A = jnp.where((s2 == r2 - dlt) & (rowc - dlt >= blk0), d, A)
            if nblk > 1:
                Gref = jnp.concatenate(
                    [jnp.broadcast_to(G[I * sub:I * sub + 1, :], (sub, HEAD_DIM)) for I in range(nblk)], axis=0)
                qt = hq * jnp.exp(G - Gref)
                pieces = [jnp.zeros((sub, c), F32)]
                for I in range(1, nblk):
                    kt = kk * jnp.exp(jnp.minimum(G[I * sub:I * sub + 1, :] - G, 0.0))
                    pieces.append(_mm_nt(qt[I * sub:(I + 1) * sub], kt))
                A = jnp.where(s2 < jnp.bitwise_and(r2, -sub), jnp.concatenate(pieces, axis=0), A)
            o = _mm(hq * jnp.exp(G), S) + _mm(A, hi)
            Gl = G[lv - 1:lv, :]
            kw = jnp.where(live, kk * jnp.exp(Gl - G), 0.0)
            dcol = jnp.exp(jnp.broadcast_to(Gl, (8, HEAD_DIM)).T[:, 0:1])
            hS_o[bi, h] = dcol * S + _mm_tn(kw, hi)
            om_ref[bi, :, 2 * MIX_W + h * HEAD_DIM:2 * MIX_W + (h + 1) * HEAD_DIM] = (
                _rms(o, nrm_ref[2:3, hs]) * jax.nn.silu(hg))

        xpad_ref[bi, 5:8, :] = xpad_ref[bi, 5 + c:8 + c, :]


def _mixers(pm3, ps3, gS, gconv, mC, mn, mm, hS, cw, sp, nrm, lb, *, bb, c, lv):
    Bt, Lp, _ = pm3.shape
    kern = functools.partial(_mix_kernel, bb=bb, c=c, lv=lv)
    st4 = pl.BlockSpec((bb, N_HEADS, HEAD_DIM, HEAD_DIM), lambda b, j: (b, 0, 0, 0))
    st3 = pl.BlockSpec((bb, N_HEADS, HEAD_DIM), lambda b, j: (b, 0, 0))
    full = lambda a: pl.BlockSpec(a.shape, lambda b, j: (0,) * a.ndim)
    return pl.pallas_call(
        kern,
        grid=(Bt // bb, Lp // c),
        in_specs=[pl.BlockSpec((bb, c, N_MIX), lambda b, j: (b, j, 0)),
                  pl.BlockSpec((bb, c, 128), lambda b, j: (b, j, 0)),
                  st4,
                  pl.BlockSpec((bb, CONV_W - 1, 3 * MIX_W), lambda b, j: (b, 0, 0)),
                  st4, st3, st3, st4,
                  full(cw), full(sp), full(nrm), full(lb)],
        out_specs=[pl.BlockSpec((bb, c, 3 * MIX_W), lambda b, j: (b, j, 0)),
                   st4, st4, st3, st3, st4],
        out_shape=[jax.ShapeDtypeStruct((Bt, Lp, 3 * MIX_W), F32),
                   jax.ShapeDtypeStruct(gS.shape, F32),
                   jax.ShapeDtypeStruct(mC.shape, F32),
                   jax.ShapeDtypeStruct(mn.shape, F32),
                   jax.ShapeDtypeStruct(mm.shape, F32),
                   jax.ShapeDtypeStruct(hS.shape, F32)],
        scratch_shapes=[pltpu.VMEM((bb, c + 8, 3 * MIX_W), F32)],
        compiler_params=_params(2),
        name="mixers",
    )(pm3, ps3, gS, gconv, mC, mn, mm, hS, cw, sp, nrm, lb)


def _merge_kernel(om_ref, pg_ref, x_ref, wbr_ref, wout_ref, ln_ref, x1_ref, hf_ref):
    acc = None
    for n in range(N_BRANCH):
        br = jnp.dot(om_ref[:, n * MIX_W:(n + 1) * MIX_W].astype(BF16), wbr_ref[n],
                     preferred_element_type=F32)
        t = jax.nn.sigmoid(pg_ref[:, n * D_MODEL:(n + 1) * D_MODEL]) * br
        acc = t if acc is None else acc + t
    x1 = x_ref[...] + jnp.dot(acc.astype(BF16), wout_ref[...], preferred_element_type=F32)
    x1_ref[...] = x1
    hf_ref[...] = _rms(x1, ln_ref[...]).astype(BF16)


def _merge(om, pm, x, w_br, w_out, ln, tm=256):
    T = x.shape[0]
    return pl.pallas_call(
        _merge_kernel,
        grid=(T // tm,),
        in_specs=[pl.BlockSpec((tm, 3 * MIX_W), lambda i: (i, 0)),
                  pl.BlockSpec((tm, N_BRANCH * D_MODEL), lambda i: (i, N_MIX // (N_BRANCH * D_MODEL))),
                  pl.BlockSpec((tm, D_MODEL), lambda i: (i, 0)),
                  pl.BlockSpec((N_BRANCH, MIX_W, D_MODEL), lambda i: (0, 0, 0)),
                  pl.BlockSpec((D_MODEL, D_MODEL), lambda i: (0, 0)),
                  pl.BlockSpec((1, D_MODEL), lambda i: (0, 0))],
        out_specs=[pl.BlockSpec((tm, D_MODEL), lambda i: (i, 0)),
                   pl.BlockSpec((tm, D_MODEL), lambda i: (i, 0))],
        out_shape=[jax.ShapeDtypeStruct((T, D_MODEL), F32),
                   jax.ShapeDtypeStruct((T, D_MODEL), BF16)],
        compiler_params=_params(1),
        name="merge",
    )(om, pm, x, w_br, w_out, ln)


def _up_kernel(h_ref, w_ref, u_ref):
    u_ref[...] = jnp.dot(h_ref[...], w_ref[...], preferred_element_type=F32)


def _ffn_up(hf, w_up, tm=1024, tn=512):
    T = hf.shape[0]
    N = w_up.shape[1]
    return pl.pallas_call(
        _up_kernel,
        grid=(T // tm, N // tn),
        in_specs=[pl.BlockSpec((tm, D_MODEL), lambda i, j: (i, 0)),
                  pl.BlockSpec((D_MODEL, tn), lambda i, j: (0, j))],
        out_specs=pl.BlockSpec((tm, tn), lambda i, j: (i, j)),
        out_shape=jax.ShapeDtypeStruct((T, N), F32),
        compiler_params=_params(2),
        name="ffn_up",
    )(hf, w_up)


def _ffn_kernel(u_ref, st_ref, cw_ref, cb_ref, wd_ref, x1_ref, ln_ref, x2_ref, y_ref, carry_ref,
                *, tm, tiles_per_seq):
    @pl.when(pl.program_id(0) % tiles_per_seq == 0)
    def _():
        carry_ref[0:2, :] = st_ref[0]

    row = lax.broadcasted_iota(jnp.int32, (tm, 1), 0)

    def conv(col0):
        cs = slice(col0, col0 + FFN_COLS)
        uc = u_ref[:, cs]
        p0 = carry_ref[0:1, cs]
        p1 = carry_ref[1:2, cs]
        s1 = jnp.where(row == 0, p1, pltpu.roll(uc, 1, 0))
        s2 = jnp.where(row == 0, p0, jnp.where(row == 1, p1, pltpu.roll(uc, 2, 0)))
        return cw_ref[0:1, cs] * s2 + cw_ref[1:2, cs] * s1 + cw_ref[2:3, cs] * uc + cb_ref[:, cs]

    acc = x1_ref[...]
    for jc in range(D_FF // FFN_COLS):
        ua = conv(jc * FFN_COLS)
        ub = conv(D_FF + jc * FFN_COLS)
        act = (jax.nn.silu(ua) * ub).astype(BF16)
        acc = acc + jnp.dot(act, wd_ref[jc * FFN_COLS:(jc + 1) * FFN_COLS, :], preferred_element_type=F32)
    carry_ref[0:2, :] = u_ref[tm - 2:tm, :]
    x2_ref[...] = acc
    y_ref[...] = _rms(acc, ln_ref[...])


def _ffn_down(u, st, cw, cb, w_down, x1, ln, rows_per_seq, tm=256):
    T = u.shape[0]
    tiles_per_seq = rows_per_seq // tm
    kern = functools.partial(_ffn_kernel, tm=tm, tiles_per_seq=tiles_per_seq)
    return pl.pallas_call(
        kern,
        grid=(T // tm,),
        in_specs=[pl.BlockSpec((tm, 2 * D_FF), lambda i: (i, 0)),
                  pl.BlockSpec((1, FFN_CONV_W - 1, 2 * D_FF), lambda i: (i // tiles_per_seq, 0, 0)),
                  pl.BlockSpec((FFN_CONV_W, 2 * D_FF), lambda i: (0, 0)),
                  pl.BlockSpec((1, 2 * D_FF), lambda i: (0, 0)),
                  pl.BlockSpec((D_FF, D_MODEL), lambda i: (0, 0)),
                  pl.BlockSpec((tm, D_MODEL), lambda i: (i, 0)),
                  pl.BlockSpec((1, D_MODEL), lambda i: (0, 0))],
        out_specs=[pl.BlockSpec((tm, D_MODEL), lambda i: (i, 0)),
                   pl.BlockSpec((tm, D_MODEL), lambda i: (i, 0))],
        out_shape=[jax.ShapeDtypeStruct((T, D_MODEL), F32),
                   jax.ShapeDtypeStruct((T, D_MODEL), F32)],
        scratch_shapes=[pltpu.VMEM((8, 2 * D_FF), F32)],
        compiler_params=_params(1),
        name="ffn_down",
    )(u, st, cw, cb, w_down, x1, ln)


def _layer_weights(l, ln_mix, w_in, gdn_conv_w, gdn_A_log, gdn_dt_bias, gdn_norm, m_ibias, m_fbias,
                   m_norm, lb_all, hgrn_norm, w_br, w_out, ln_ffn, w_up, ffn_conv_w, ffn_conv_b, w_down):
    w = w_in[l]
    w_main = jnp.concatenate([w[:, 0:2048], w[:, 2056:4104], w[:, 4112:]], axis=1).astype(BF16)
    w_small = jnp.concatenate([w[:, 2048:2056], w[:, 4104:4112],
                               jnp.zeros((D_MODEL, 128 - 4 * N_HEADS), F32)], axis=1).astype(BF16)
    z4 = jnp.zeros((N_HEADS,), F32)
    pad = jnp.zeros((128 - 4 * N_HEADS,), F32)
    sp = jnp.zeros((8, 128), F32)
    sp = sp.at[0].set(jnp.concatenate([z4, gdn_A_log[l], z4, z4, pad]))
    sp = sp.at[1].set(jnp.concatenate([z4, gdn_dt_bias[l], m_ibias[l], m_fbias[l], pad]))
    return dict(
        ln_mix=ln_mix[l][None], w_main=w_main, w_small=w_small, cw=gdn_conv_w[l], sp=sp,
        nrm=jnp.stack([gdn_norm[l], m_norm[l], hgrn_norm[l]]), lb=lb_all[l][None],
        w_br=w_br[l].astype(BF16), w_out=w_out[l].astype(BF16), ln_ffn=ln_ffn[l][None],
        w_up=w_up[l].astype(BF16), fcw=ffn_conv_w[l], fcb=ffn_conv_b[l][None],
        w_down=w_down[l].astype(BF16))


def _trunk(x3, states, layers, ln_final, *, lv, c, bb, sample):
    Bt, Lp, _ = x3.shape
    T = Bt * Lp
    x = x3.reshape(T, D_MODEL)
    gdn_S, gdn_conv, m_C, m_n, m_m, h_S, ffn_conv = states
    new = [[] for _ in range(7)]
    y = None
    for l, W in enumerate(layers):
        pm, ps = _in_proj(x, W["ln_mix"], W["w_main"], W["w_small"], tm=min(T, 1024))
        pm3 = pm.reshape(Bt, Lp, N_MAIN)
        mm_in = jnp.broadcast_to(m_m[l][..., None], (Bt, N_HEADS, HEAD_DIM))
        om, gS, mC, mn, mm, hS = _mixers(
            pm3, ps.reshape(Bt, Lp, 128), gdn_S[l], gdn_conv[l], m_C[l], m_n[l], mm_in, h_S[l],
            W["cw"], W["sp"], W["nrm"], W["lb"], bb=bb, c=c, lv=lv)
        x1, hf = _merge(om.reshape(T, 3 * MIX_W), pm, x, W["w_br"], W["w_out"], W["ln_ffn"])
        u = _ffn_up(hf, W["w_up"])
        u3 = u.reshape(Bt, Lp, 2 * D_FF)
        if sample:
            u3 = u3.at[:-1, Lp - 2:Lp, :].set(ffn_conv[l][1:])
            st, rows_per_seq = ffn_conv[l][0:1], T
        else:
            st, rows_per_seq = ffn_conv[l], Lp
        x, y = _ffn_down(u3.reshape(T, 2 * D_FF), st, W["fcw"], W["fcb"], W["w_down"], x1, ln_final[None],
                         rows_per_seq)
        end = lv if sample else Lp
        outs = (gS, pm3[:, end - (CONV_W - 1):end, 0:3 * MIX_W], mC, mn, mm[..., 0], hS,
                u3[:, end - (FFN_CONV_W - 1):end, :])
        for acc, o in zip(new, outs):
            acc.append(o)
    return y.reshape(Bt, Lp, D_MODEL), tuple(jnp.stack(s, axis=0) for s in new)


def kernel(x_prompt, x_sample, state_gdn_S, state_gdn_conv, state_mlstm_C, state_mlstm_n, state_mlstm_m, state_hgrn_S, state_ffn_conv, ln_mix, w_in, gdn_conv_w, gdn_A_log, gdn_dt_bias, gdn_norm, m_ibias, m_fbias, m_norm, hgrn_lb, hgrn_norm, w_br, w_out, ln_ffn, w_up, ffn_conv_w, ffn_conv_b, w_down, ln_final):
    depth = w_in.shape[0]
    lb_all = jnp.cumsum(jax.nn.softmax(hgrn_lb.astype(F32), axis=0), axis=0)
    lb_all = lb_all - lb_all[0]
    layers = [_layer_weights(l, ln_mix, w_in, gdn_conv_w, gdn_A_log, gdn_dt_bias, gdn_norm, m_ibias,
                             m_fbias, m_norm, lb_all, hgrn_norm, w_br, w_out, ln_ffn, w_up, ffn_conv_w,
                             ffn_conv_b, w_down) for l in range(depth)]

    B, L, _ = x_prompt.shape
    H, Dh = N_HEADS, HEAD_DIM
    zeros = (jnp.zeros((depth, B, H, Dh, Dh), F32), jnp.zeros((depth, B, CONV_W - 1, 3 * MIX_W), F32),
             jnp.zeros((depth, B, H, Dh, Dh), F32), jnp.zeros((depth, B, H, Dh), F32),
             jnp.zeros((depth, B, H), F32), jnp.zeros((depth, B, H, Dh, Dh), F32),
             jnp.zeros((depth, B, FFN_CONV_W - 1, 2 * D_FF), F32))
    y_p, st_p = _trunk(x_prompt, zeros, layers, ln_final, lv=CHUNK, c=CHUNK, bb=1, sample=False)

    Bs, Ls, _ = x_sample.shape
    xs = jnp.pad(x_sample, ((0, 0), (0, SAMPLE_PAD - Ls), (0, 0)))
    st_in = (state_gdn_S, state_gdn_conv, state_mlstm_C, state_mlstm_n, state_mlstm_m, state_hgrn_S,
             state_ffn_conv)
    y_s, st_s = _trunk(xs, st_in, layers, ln_final, lv=Ls, c=SAMPLE_PAD, bb=2, sample=True)
    return (y_p, y_s[:, :Ls]) + st_p + st_s
```

```python
import functools
import math

import jax
import jax.numpy as jnp
from jax import lax
from jax.experimental import pallas as pl
from jax.experimental.pallas import tpu as pltpu

D_MODEL = 1024
N_HEADS = 4
HEAD_DIM = 128
MIX_W = N_HEADS * HEAD_DIM
N_BRANCH = 3
CONV_W = 4
FFN_CONV_W = 3
D_FF = 2816
CHUNK = 64
EPS = 1e-6
N_MIX = 6144
N_MAIN = 9216
SAMPLE_PAD = 8
SUB = 16
FFN_COLS = 256

F32 = jnp.float32
BF16 = jnp.bfloat16
VMEM_LIMIT = 56 * 1024 * 1024


def _bmm(a, b):
    return jnp.einsum('gtd,gde->gte', a.astype(BF16), b.astype(BF16), preferred_element_type=F32)


def _bmm_nt(a, b):
    return jnp.einsum('gtd,gsd->gts', a.astype(BF16), b.astype(BF16), preferred_element_type=F32)


def _bmm_tn(a, b):
    return jnp.einsum('gsd,gse->gde', a.astype(BF16), b.astype(BF16), preferred_element_type=F32)


def _split2(a):
    hi = a.astype(BF16)
    return hi, (a - hi.astype(F32)).astype(BF16)


def _bmm_split(a, b):
    a_hi, a_lo = a
    b_hi, b_lo = b
    n = a_hi.shape[1]
    p = jnp.einsum('gtk,gks->gts', jnp.concatenate([a_hi, a_lo], axis=1), b_hi, preferred_element_type=F32)
    return p[:, :n] + p[:, n:] + jnp.einsum('gtk,gks->gts', a_hi, b_lo, preferred_element_type=F32)


def _cumsum_rows(tri_bf, x):
    n = x.shape[1]
    x1 = x.astype(BF16)
    r1 = x - x1.astype(F32)
    x2 = r1.astype(BF16)
    x3 = (r1 - x2.astype(F32)).astype(BF16)
    p = jnp.dot(tri_bf, jnp.concatenate([x1, x2, x3], axis=1), preferred_element_type=F32)
    return p[:, 0:n] + p[:, n:2 * n] + p[:, 2 * n:3 * n]


def _rms(x, g):
    return x * lax.rsqrt(jnp.mean(x * x, -1, keepdims=True) + EPS) * g


def _params(n_axes):
    return pltpu.CompilerParams(dimension_semantics=("arbitrary",) * n_axes,
                                vmem_limit_bytes=VMEM_LIMIT)


def _in_kernel(x_ref, ln_ref, wm_ref, ws_ref, pm_ref, ps_ref, h_ref):
    @pl.when(pl.program_id(1) == 0)
    def _():
        hb = _rms(x_ref[...], ln_ref[...]).astype(BF16)
        h_ref[...] = hb
        ps_ref[...] = jnp.dot(hb, ws_ref[...], preferred_element_type=F32)

    pm_ref[...] = jnp.dot(h_ref[...], wm_ref[...], preferred_element_type=F32)


def _in_proj(x, ln, w_main, w_small, tm, tn=1024):
    T = x.shape[0]
    return pl.pallas_call(
        _in_kernel,
        grid=(T // tm, N_MAIN // tn),
        in_specs=[pl.BlockSpec((tm, D_MODEL), lambda i, j: (i, 0)),
                  pl.BlockSpec((1, D_MODEL), lambda i, j: (0, 0)),
                  pl.BlockSpec((D_MODEL, tn), lambda i, j: (0, j)),
                  pl.BlockSpec((D_MODEL, 128), lambda i, j: (0, 0))],
        out_specs=[pl.BlockSpec((tm, tn), lambda i, j: (i, j)),
                   pl.BlockSpec((tm, 128), lambda i, j: (i, 0))],
        out_shape=[jax.ShapeDtypeStruct((T, N_MAIN), F32),
                   jax.ShapeDtypeStruct((T, 128), F32)],
        scratch_shapes=[pltpu.VMEM((tm, D_MODEL), BF16)],
        compiler_params=_params(2),
        name="in_proj",
    )(x, ln, w_main, w_small)


def _mix_kernel(pm_ref, ps_ref, gS_ref, gconv_ref, mC_ref, mn_ref, mm_ref, hS_ref,
                cw_ref, sp_ref, nrm_ref, lb_ref,
                om_ref, gS_o, mC_o, mn_o, mm_o, hS_o,
                xpad_ref, *, bb, c, lv):
    G = bb * N_HEADS
    gh = [(bi, h) for bi in range(bb) for h in range(N_HEADS)]

    @pl.when(pl.program_id(1) == 0)
    def _():
        gS_o[...] = gS_ref[...]
        mC_o[...] = mC_ref[...]
        mn_o[...] = mn_ref[...]
        mm_o[...] = mm_ref[...]
        hS_o[...] = hS_ref[...]
        xpad_ref[:, 5:8, :] = gconv_ref[...]

    r2 = lax.broadcasted_iota(jnp.int32, (c, c), 0)
    s2 = lax.broadcasted_iota(jnp.int32, (c, c), 1)
    causal = r2 >= s2
    strict = r2 > s2
    tri_bf = causal.astype(BF16)
    eye = (r2 == s2).astype(F32)
    rowc = lax.broadcasted_iota(jnp.int32, (c, 1), 0)
    live = rowc < lv
    lane = lax.broadcasted_iota(jnp.int32, (c, 128), 1)
    sub = min(SUB, c)
    nblk = c // sub
    n_sq = int(math.log2(c)) - 1

    def heads(col0):
        return jnp.stack([pm_ref[bi, :, col0 + h * HEAD_DIM:col0 + (h + 1) * HEAD_DIM] for bi, h in gh])

    def head_rows(ref, r):
        return jnp.stack([ref[r:r + 1, h * HEAD_DIM:(h + 1) * HEAD_DIM] for _, h in gh])

    def put(col0, val):
        for g, (bi, h) in enumerate(gh):
            om_ref[bi, :, col0 + h * HEAD_DIM:col0 + (h + 1) * HEAD_DIM] = val[g]

    alog = sp_ref[0:1, :]
    bias = sp_ref[1:2, :]
    gates, cums, gates_t, cums_t, convs = [], [], [], [], []
    for bi in range(bb):
        ps = ps_ref[bi]
        z = ps + bias
        gt = jnp.where(lane < 4, jax.nn.sigmoid(ps),
                       jnp.where(lane < 8, -jnp.exp(alog) * jax.nn.softplus(z),
                                 jnp.where(lane < 12, z, jax.nn.log_sigmoid(z))))
        cm = _cumsum_rows(tri_bf, gt)
        gates.append(gt)
        cums.append(cm)
        gates_t.append(gt.T)
        cums_t.append(cm.T)
        xpad_ref[bi, 8:8 + c, :] = pm_ref[bi, :, 0:3 * MIX_W]
        acc = cw_ref[0:1, :] * xpad_ref[bi, 5:5 + c, :]
        for j in range(1, CONV_W):
            acc = acc + cw_ref[j:j + 1, :] * xpad_ref[bi, 5 + j:5 + j + c, :]
        convs.append(jax.nn.silu(acc))
        xpad_ref[bi, 5:8, :] = xpad_ref[bi, 5 + c:8 + c, :]

    col = lambda arrs, j: jnp.stack([arrs[bi][:, j + h:j + h + 1] for bi, h in gh])
    row = lambda arrs, j: jnp.stack([arrs[bi][j + h:j + h + 1, :] for bi, h in gh])
    part = lambda p: jnp.stack([convs[bi][:, p * MIX_W + h * HEAD_DIM:p * MIX_W + (h + 1) * HEAD_DIM]
                                for bi, h in gh])

    q = part(0)
    k = part(1)
    v = part(2)
    q = q * lax.rsqrt(jnp.sum(q * q, -1, keepdims=True) + EPS) * HEAD_DIM ** -0.5
    k = k * lax.rsqrt(jnp.sum(k * k, -1, keepdims=True) + EPS)
    beta = col(gates, 0)
    Gc = col(cums, 4)
    Gr = row(cums_t, 4)
    decay = jnp.exp(jnp.where(causal, Gc - Gr, -jnp.inf))
    S = gS_o[...].reshape(G, HEAD_DIM, HEAD_DIM)
    qk2 = jnp.concatenate([q, k], axis=1)
    P = _bmm_nt(qk2, k)
    R = _bmm(qk2, S)
    QK, KK = P[:, :c], P[:, c:]
    QS, KS = R[:, :c], R[:, c:]
    eG = jnp.exp(Gc)
    rhs = beta * (v - eG * KS)
    Mp = -jnp.where(strict, beta * decay * KK, 0.0)
    Tinv = eye + Mp
    Ms = _split2(Mp)
    for _ in range(n_sq):
        Mp = _bmm_split(Ms, Ms)
        Ms = _split2(Mp)
        Tinv = Tinv + _bmm_split(_split2(Tinv), Ms)
    u = _bmm_split(_split2(Tinv), _split2(rhs))
    o = eG * QS + _bmm(QK * decay, u)
    Gl = Gc[:, lv - 1:lv, :]
    wl = jnp.where(live, jnp.exp(Gl - Gc), 0.0)
    gS_o[...] = (jnp.exp(Gl) * S + _bmm_tn(k * wl, u)).reshape(bb, N_HEADS, HEAD_DIM, HEAD_DIM)
    put(0, _rms(o, head_rows(nrm_ref, 0)) * jax.nn.silu(heads(3 * MIX_W)))

    q = heads(4 * MIX_W)
    k = heads(5 * MIX_W) * HEAD_DIM ** -0.5
    v = heads(6 * MIX_W)
    ig_c = col(gates, 8)
    ig_r = row(gates_t, 8)
    Fc = col(cums, 12)
    Fr = row(cums_t, 12)
    m0 = jnp.stack([mm_o[bi, h:h + 1, 0:1] for bi, h in gh])
    n_row = jnp.stack([mn_o[bi, h:h + 1, :] for bi, h in gh])
    C = mC_o[...].reshape(G, HEAD_DIM, HEAD_DIM)
    logD = jnp.where(causal, Fc - Fr + ig_r, -jnp.inf)
    b = Fc + m0
    mt = jnp.maximum(b, jnp.max(logD, -1, keepdims=True))
    s = _bmm_nt(q, k) * jnp.exp(logD - mt)
    inter = jnp.exp(b - mt)
    num = _bmm(s, v) + inter * _bmm(q, C)
    den = jnp.sum(s, -1, keepdims=True) + inter * jnp.sum(q * n_row, -1, keepdims=True)
    hm = num / jnp.maximum(jnp.abs(den), jnp.exp(-mt))
    m_new = mt[:, lv - 1:lv, :]
    Fl = Fc[:, lv - 1:lv, :]
    wl = jnp.where(live, jnp.exp(Fl - Fc + ig_c - m_new), 0.0)
    d0 = jnp.exp(Fl + m0 - m_new)
    kw = k * wl
    mC_o[...] = (d0 * C + _bmm_tn(kw, v)).reshape(bb, N_HEADS, HEAD_DIM, HEAD_DIM)
    n_new = d0 * n_row + jnp.sum(kw, axis=1, keepdims=True)
    m_b = jnp.broadcast_to(m_new, (G, 1, HEAD_DIM))
    for g, (bi, h) in enumerate(gh):
        mn_o[bi, h:h + 1, :] = n_new[g]
        mm_o[bi, h:h + 1, :] = m_b[g]
    put(MIX_W, _rms(hm, head_rows(nrm_ref, 1)) * jax.nn.sigmoid(heads(7 * MIX_W)))

    hq = jax.nn.silu(heads(8 * MIX_W))
    hi = heads(10 * MIX_W)
    lbh = lb_ref[...]
    Gs_l, kk_l = [], []
    for bi in range(bb):
        fg = lbh + (1.0 - lbh) * jax.nn.sigmoid(pm_ref[bi, :, 9 * MIX_W:10 * MIX_W])
        kk_l.append(1.0 - fg)
        Gs_l.append(_cumsum_rows(tri_bf, jnp.log(fg)))
    kk = jnp.stack([kk_l[bi][:, h * HEAD_DIM:(h + 1) * HEAD_DIM] for bi, h in gh])
    Gm = jnp.stack([Gs_l[bi][:, h * HEAD_DIM:(h + 1) * HEAD_DIM] for bi, h in gh])
    S = hS_o[...].reshape(G, HEAD_DIM, HEAD_DIM)
    blk0 = jnp.bitwise_and(rowc, -sub)
    A = jnp.zeros((G, c, c), F32)
    for dlt in range(sub):
        ks = kk if dlt == 0 else pltpu.roll(kk, dlt, 1)
        Gs = Gm if dlt == 0 else pltpu.roll(Gm, dlt, 1)
        d = jnp.sum(hq * ks * jnp.exp(jnp.minimum(Gm - Gs, 0.0)), -1, keepdims=True)
        A = jnp.where((s2 == r2 - dlt) & (rowc - dlt >= blk0), d, A)
    if nblk > 1:
        Gref = jnp.concatenate(
            [jnp.broadcast_to(Gm[:, I * sub:I * sub + 1, :], (G, sub, HEAD_DIM)) for I in range(nblk)], axis=1)
        qt = hq * jnp.exp(Gm - Gref)
        pieces = [jnp.zeros((G, sub, c), F32)]
        for I in range(1, nblk):
            kt = kk * jnp.exp(jnp.minimum(Gm[:, I * sub:I * sub + 1, :] - Gm, 0.0))
            pieces.append(_bmm_nt(qt[:, I * sub:(I + 1) * sub], kt))
        A = jnp.where(s2 < jnp.bitwise_and(r2, -sub), jnp.concatenate(pieces, axis=1), A)
    o = _bmm(hq * jnp.exp(Gm), S) + _bmm(A, hi)
    Gl = Gm[:, lv - 1:lv, :]
    kw = jnp.where(live, kk * jnp.exp(Gl - Gm), 0.0)
    dcol = jnp.exp(jnp.swapaxes(jnp.broadcast_to(Gl, (G, 8, HEAD_DIM)), 1, 2)[:, :, 0:1])
    hS_o[...] = (dcol * S + _bmm_tn(kw, hi)).reshape(bb, N_HEADS, HEAD_DIM, HEAD_DIM)
    put(2 * MIX_W, _rms(o, head_rows(nrm_ref, 2)) * jax.nn.silu(heads(11 * MIX_W)))


def _mixers(pm3, ps3, gS, gconv, mC, mn, mm, hS, cw, sp, nrm, lb, *, bb, c, lv):
    Bt, Lp, _ = pm3.shape
    kern = functools.partial(_mix_kernel, bb=bb, c=c, lv=lv)
    st4 = pl.BlockSpec((bb, N_HEADS, HEAD_DIM, HEAD_DIM), lambda b, j: (b, 0, 0, 0))
    st3 = pl.BlockSpec((bb, N_HEADS, HEAD_DIM), lambda b, j: (b, 0, 0))
    full = lambda a: pl.BlockSpec(a.shape, lambda b, j: (0,) * a.ndim)
    return pl.pallas_call(
        kern,
        grid=(Bt // bb, Lp // c),
        in_specs=[pl.BlockSpec((bb, c, N_MIX), lambda b, j: (b, j, 0)),
                  pl.BlockSpec((bb, c, 128), lambda b, j: (b, j, 0)),
                  st4,
                  pl.BlockSpec((bb, CONV_W - 1, 3 * MIX_W), lambda b, j: (b, 0, 0)),
                  st4, st3, st3, st4,
                  full(cw), full(sp), full(nrm), full(lb)],
        out_specs=[pl.BlockSpec((bb, c, 3 * MIX_W), lambda b, j: (b, j, 0)),
                   st4, st4, st3, st3, st4],
        out_shape=[jax.ShapeDtypeStruct((Bt, Lp, 3 * MIX_W), F32),
                   jax.ShapeDtypeStruct(gS.shape, F32),
                   jax.ShapeDtypeStruct(mC.shape, F32),
                   jax.ShapeDtypeStruct(mn.shape, F32),
                   jax.ShapeDtypeStruct(mm.shape, F32),
                   jax.ShapeDtypeStruct(hS.shape, F32)],
        scratch_shapes=[pltpu.VMEM((bb, c + 8, 3 * MIX_W), F32)],
        compiler_params=_params(2),
        name="mixers",
    )(pm3, ps3, gS, gconv, mC, mn, mm, hS, cw, sp, nrm, lb)


def _merge_kernel(om_ref, pg_ref, x_ref, wbr_ref, wout_ref, ln_ref, x1_ref, hf_ref):
    acc = None
    for n in range(N_BRANCH):
        br = jnp.dot(om_ref[:, n * MIX_W:(n + 1) * MIX_W].astype(BF16), wbr_ref[n],
                     preferred_element_type=F32)
        t = jax.nn.sigmoid(pg_ref[:, n * D_MODEL:(n + 1) * D_MODEL]) * br
        acc = t if acc is None else acc + t
    x1 = x_ref[...] + jnp.dot(acc.astype(BF16), wout_ref[...], preferred_element_type=F32)
    x1_ref[...] = x1
    hf_ref[...] = _rms(x1, ln_ref[...]).astype(BF16)


def _merge(om, pm, x, w_br, w_out, ln, tm=256):
    T = x.shape[0]
    return pl.pallas_call(
        _merge_kernel,
        grid=(T // tm,),
        in_specs=[pl.BlockSpec((tm, 3 * MIX_W), lambda i: (i, 0)),
                  pl.BlockSpec((tm, N_BRANCH * D_MODEL), lambda i: (i, N_MIX // (N_BRANCH * D_MODEL))),
                  pl.BlockSpec((tm, D_MODEL), lambda i: (i, 0)),
                  pl.BlockSpec((N_BRANCH, MIX_W, D_MODEL), lambda i: (0, 0, 0)),
                  pl.BlockSpec((D_MODEL, D_MODEL), lambda i: (0, 0)),
                  pl.BlockSpec((1, D_MODEL), lambda i: (0, 0))],
        out_specs=[pl.BlockSpec((tm, D_MODEL), lambda i: (i, 0)),
                   pl.BlockSpec((tm, D_MODEL), lambda i: (i, 0))],
        out_shape=[jax.ShapeDtypeStruct((T, D_MODEL), F32),
                   jax.ShapeDtypeStruct((T, D_MODEL), BF16)],
        compiler_params=_params(1),
        name="merge",
    )(om, pm, x, w_br, w_out, ln)


def _up_kernel(h_ref, w_ref, u_ref):
    u_ref[...] = jnp.dot(h_ref[...], w_ref[...], preferred_element_type=F32)


def _ffn_up(hf, w_up, tm=1024, tn=512):
    T = hf.shape[0]
    N = w_up.shape[1]
    return pl.pallas_call(
        _up_kernel,
        grid=(T // tm, N // tn),
        in_specs=[pl.BlockSpec((tm, D_MODEL), lambda i, j: (i, 0)),
                  pl.BlockSpec((D_MODEL, tn), lambda i, j: (0, j))],
        out_specs=pl.BlockSpec((tm, tn), lambda i, j: (i, j)),
        out_shape=jax.ShapeDtypeStruct((T, N), F32),
        compiler_params=_params(2),
        name="ffn_up",
    )(hf, w_up)


def _ffn_kernel(u_ref, st_ref, cw_ref, cb_ref, wd_ref, x1_ref, ln_ref, x2_ref, y_ref, carry_ref,
                *, tm, tiles_per_seq):
    @pl.when(pl.program_id(0) % tiles_per_seq == 0)
    def _():
        carry_ref[0:2, :] = st_ref[0]

    row = lax.broadcasted_iota(jnp.int32, (tm, 1), 0)

    def conv(col0):
        cs = slice(col0, col0 + FFN_COLS)
        uc = u_ref[:, cs]
        p0 = carry_ref[0:1, cs]
        p1 = carry_ref[1:2, cs]
        s1 = jnp.where(row == 0, p1, pltpu.roll(uc, 1, 0))
        s2 = jnp.where(row == 0, p0, jnp.where(row == 1, p1, pltpu.roll(uc, 2, 0)))
        return cw_ref[0:1, cs] * s2 + cw_ref[1:2, cs] * s1 + cw_ref[2:3, cs] * uc + cb_ref[:, cs]

    acc = x1_ref[...]
    for jc in range(D_FF // FFN_COLS):
        ua = conv(jc * FFN_COLS)
        ub = conv(D_FF + jc * FFN_COLS)
        act = (jax.nn.silu(ua) * ub).astype(BF16)
        acc = acc + jnp.dot(act, wd_ref[jc * FFN_COLS:(jc + 1) * FFN_COLS, :], preferred_element_type=F32)
    carry_ref[0:2, :] = u_ref[tm - 2:tm, :]
    x2_ref[...] = acc
    y_ref[...] = _rms(acc, ln_ref[...])


def _ffn_down(u, st, cw, cb, w_down, x1, ln, rows_per_seq, tm=256):
    T = u.shape[0]
    tiles_per_seq = rows_per_seq // tm
    kern = functools.partial(_ffn_kernel, tm=tm, tiles_per_seq=tiles_per_seq)
    return pl.pallas_call(
        kern,
        grid=(T // tm,),
        in_specs=[pl.BlockSpec((tm, 2 * D_FF), lambda i: (i, 0)),
                  pl.BlockSpec((1, FFN_CONV_W - 1, 2 * D_FF), lambda i: (i // tiles_per_seq, 0, 0)),
                  pl.BlockSpec((FFN_CONV_W, 2 * D_FF), lambda i: (0, 0)),
                  pl.BlockSpec((1, 2 * D_FF), lambda i: (0, 0)),
                  pl.BlockSpec((D_FF, D_MODEL), lambda i: (0, 0)),
                  pl.BlockSpec((tm, D_MODEL), lambda i: (i, 0)),
                  pl.BlockSpec((1, D_MODEL), lambda i: (0, 0))],
        out_specs=[pl.BlockSpec((tm, D_MODEL), lambda i: (i, 0)),
                   pl.BlockSpec((tm, D_MODEL), lambda i: (i, 0))],
        out_shape=[jax.ShapeDtypeStruct((T, D_MODEL), F32),
                   jax.ShapeDtypeStruct((T, D_MODEL), F32)],
        scratch_shapes=[pltpu.VMEM((8, 2 * D_FF), F32)],
        compiler_params=_params(1),
        name="ffn_down",
    )(u, st, cw, cb, w_down, x1, ln)


def _layer_weights(l, ln_mix, w_in, gdn_conv_w, gdn_A_log, gdn_dt_bias, gdn_norm, m_ibias, m_fbias,
                   m_norm, lb_all, hgrn_norm, w_br, w_out, ln_ffn, w_up, ffn_conv_w, ffn_conv_b, w_down):
    w = w_in[l]
    w_main = jnp.concatenate([w[:, 0:2048], w[:, 2056:4104], w[:, 4112:]], axis=1).astype(BF16)
    w_small = jnp.concatenate([w[:, 2048:2056], w[:, 4104:4112],
                               jnp.zeros((D_MODEL, 128 - 4 * N_HEADS), F32)], axis=1).astype(BF16)
    z4 = jnp.zeros((N_HEADS,), F32)
    pad = jnp.zeros((128 - 4 * N_HEADS,), F32)
    sp = jnp.zeros((8, 128), F32)
    sp = sp.at[0].set(jnp.concatenate([z4, gdn_A_log[l], z4, z4, pad]))
    sp = sp.at[1].set(jnp.concatenate([z4, gdn_dt_bias[l], m_ibias[l], m_fbias[l], pad]))
    return dict(
        ln_mix=ln_mix[l][None], w_main=w_main, w_small=w_small, cw=gdn_conv_w[l], sp=sp,
        nrm=jnp.stack([gdn_norm[l], m_norm[l], hgrn_norm[l]]), lb=lb_all[l][None],
        w_br=w_br[l].astype(BF16), w_out=w_out[l].astype(BF16), ln_ffn=ln_ffn[l][None],
        w_up=w_up[l].astype(BF16), fcw=ffn_conv_w[l], fcb=ffn_conv_b[l][None],
        w_down=w_down[l].astype(BF16))


def _trunk(x3, states, layers, ln_final, *, lv, c, bb, sample):
    Bt, Lp, _ = x3.shape
    T = Bt * Lp
    x = x3.reshape(T, D_MODEL)
    gdn_S, gdn_conv, m_C, m_n, m_m, h_S, ffn_conv = states
    new = [[] for _ in range(7)]
    y = None
    for l, W in enumerate(layers):
        pm, ps = _in_proj(x, W["ln_mix"], W["w_main"], W["w_small"], tm=min(T, 1024))
        pm3 = pm.reshape(Bt, Lp, N_MAIN)
        mm_in = jnp.broadcast_to(m_m[l][..., None], (Bt, N_HEADS, HEAD_DIM))
        om, gS, mC, mn, mm, hS = _mixers(
            pm3, ps.reshape(Bt, Lp, 128), gdn_S[l], gdn_conv[l], m_C[l], m_n[l], mm_in, h_S[l],
            W["cw"], W["sp"], W["nrm"], W["lb"], bb=bb, c=c, lv=lv)
        x1, hf = _merge(om.reshape(T, 3 * MIX_W), pm, x, W["w_br"], W["w_out"], W["ln_ffn"])
        u = _ffn_up(hf, W["w_up"])
        u3 = u.reshape(Bt, Lp, 2 * D_FF)
        if sample:
            u3 = u3.at[:-1, Lp - 2:Lp, :].set(ffn_conv[l][1:])
            st, rows_per_seq = ffn_conv[l][0:1], T
        else:
            st, rows_per_seq = ffn_conv[l], Lp
        x, y = _ffn_down(u3.reshape(T, 2 * D_FF), st, W["fcw"], W["fcb"], W["w_down"], x1, ln_final[None],
                         rows_per_seq)
        end = lv if sample else Lp
        outs = (gS, pm3[:, end - (CONV_W - 1):end, 0:3 * MIX_W], mC, mn, mm[..., 0], hS,
                u3[:, end - (FFN_CONV_W - 1):end, :])
        for acc, o in zip(new, outs):
            acc.append(o)
    return y.reshape(Bt, Lp, D_MODEL), tuple(jnp.stack(s, axis=0) for s in new)


def kernel(x_prompt, x_sample, state_gdn_S, state_gdn_conv, state_mlstm_C, state_mlstm_n, state_mlstm_m, state_hgrn_S, state_ffn_conv, ln_mix, w_in, gdn_conv_w, gdn_A_log, gdn_dt_bias, gdn_norm, m_ibias, m_fbias, m_norm, hgrn_lb, hgrn_norm, w_br, w_out, ln_ffn, w_up, ffn_conv_w, ffn_conv_b, w_down, ln_final):
    depth = w_in.shape[0]
    lb_all = jnp.cumsum(jax.nn.softmax(hgrn_lb.astype(F32), axis=0), axis=0)
    lb_all = lb_all - lb_all[0]
    layers = [_layer_weights(l, ln_mix, w_in, gdn_conv_w, gdn_A_log, gdn_dt_bias, gdn_norm, m_ibias,
                             m_fbias, m_norm, lb_all, hgrn_norm, w_br, w_out, ln_ffn, w_up, ffn_conv_w,
                             ffn_conv_b, w_down) for l in range(depth)]

    B, L, _ = x_prompt.shape
    H, Dh = N_HEADS, HEAD_DIM
    zeros = (jnp.zeros((depth, B, H, Dh, Dh), F32), jnp.zeros((depth, B, CONV_W - 1, 3 * MIX_W), F32),
             jnp.zeros((depth, B, H, Dh, Dh), F32), jnp.zeros((depth, B, H, Dh), F32),
             jnp.zeros((depth, B, H), F32), jnp.zeros((depth, B, H, Dh, Dh), F32),
             jnp.zeros((depth, B, FFN_CONV_W - 1, 2 * D_FF), F32))
    y_p, st_p = _trunk(x_prompt, zeros, layers, ln_final, lv=CHUNK, c=CHUNK, bb=1, sample=False)

    Bs, Ls, _ = x_sample.shape
    xs = jnp.pad(x_sample, ((0, 0), (0, SAMPLE_PAD - Ls), (0, 0)))
    st_in = (state_gdn_S, state_gdn_conv, state_mlstm_C, state_mlstm_n, state_mlstm_m, state_hgrn_S,
             state_ffn_conv)
    y_s, st_s = _trunk(xs, st_in, layers, ln_final, lv=Ls, c=SAMPLE_PAD, bb=4, sample=True)
    return (y_p, y_s[:, :Ls]) + st_p + st_s
```

```python
import functools
import math

import jax
import jax.numpy as jnp
from jax import lax
from jax.experimental import pallas as pl
from jax.experimental.pallas import tpu as pltpu

D_MODEL = 1024
N_HEADS = 4
HEAD_DIM = 128
MIX_W = N_HEADS * HEAD_DIM
N_BRANCH = 3
CONV_W = 4
FFN_CONV_W = 3
D_FF = 2816
CHUNK = 64
EPS = 1e-6
N_MIX = 6144
N_MAIN = 9216
SAMPLE_PAD = 8
SUB = 16
FFN_COLS = 256

F32 = jnp.float32
BF16 = jnp.bfloat16
VMEM_LIMIT = 56 * 1024 * 1024


def _bmm(a, b):
    return jnp.einsum('gtd,gde->gte', a.astype(BF16), b.astype(BF16), preferred_element_type=F32)


def _bmm_nt(a, b):
    return jnp.einsum('gtd,gsd->gts', a.astype(BF16), b.astype(BF16), preferred_element_type=F32)


def _bmm_tn(a, b):
    return jnp.einsum('gsd,gse->gde', a.astype(BF16), b.astype(BF16), preferred_element_type=F32)


def _split2(a):
    hi = a.astype(BF16)
    return hi, (a - hi.astype(F32)).astype(BF16)


def _bmm_split(a, b):
    a_hi, a_lo = a
    b_hi, b_lo = b
    n = a_hi.shape[1]
    p = jnp.einsum('gtk,gks->gts', jnp.concatenate([a_hi, a_lo], axis=1), b_hi, preferred_element_type=F32)
    return p[:, :n] + p[:, n:] + jnp.einsum('gtk,gks->gts', a_hi, b_lo, preferred_element_type=F32)


def _cumsum_rows(tri_bf, x):
    n = x.shape[1]
    x1 = x.astype(BF16)
    r1 = x - x1.astype(F32)
    x2 = r1.astype(BF16)
    x3 = (r1 - x2.astype(F32)).astype(BF16)
    p = jnp.dot(tri_bf, jnp.concatenate([x1, x2, x3], axis=1), preferred_element_type=F32)
    return p[:, 0:n] + p[:, n:2 * n] + p[:, 2 * n:3 * n]


def _rms(x, g):
    return x * lax.rsqrt(jnp.mean(x * x, -1, keepdims=True) + EPS) * g


def _params(n_axes):
    return pltpu.CompilerParams(dimension_semantics=("arbitrary",) * n_axes,
                                vmem_limit_bytes=VMEM_LIMIT)


def _in_kernel(x_ref, ln_ref, wm_ref, ws_ref, pm_ref, ps_ref, h_ref):
    @pl.when(pl.program_id(1) == 0)
    def _():
        hb = _rms(x_ref[...], ln_ref[...]).astype(BF16)
        h_ref[...] = hb
        ps_ref[...] = jnp.dot(hb, ws_ref[...], preferred_element_type=F32)

    pm_ref[...] = jnp.dot(h_ref[...], wm_ref[...], preferred_element_type=F32)


def _in_proj(x, ln, w_main, w_small, l, tm, tn=1024):
    T = x.shape[0]
    return pl.pallas_call(
        _in_kernel,
        grid=(T // tm, N_MAIN // tn),
        in_specs=[pl.BlockSpec((tm, D_MODEL), lambda i, j: (i, 0)),
                  pl.BlockSpec((1, D_MODEL), lambda i, j: (0, 0)),
                  pl.BlockSpec((None, D_MODEL, tn), lambda i, j: (l, 0, j)),
                  pl.BlockSpec((None, D_MODEL, 128), lambda i, j: (l, 0, 0))],
        out_specs=[pl.BlockSpec((tm, tn), lambda i, j: (i, j)),
                   pl.BlockSpec((tm, 128), lambda i, j: (i, 0))],
        out_shape=[jax.ShapeDtypeStruct((T, N_MAIN), F32),
                   jax.ShapeDtypeStruct((T, 128), F32)],
        scratch_shapes=[pltpu.VMEM((tm, D_MODEL), BF16)],
        compiler_params=_params(2),
        name="in_proj",
    )(x, ln, w_main, w_small)


def _mix_kernel(pm_ref, ps_ref, gS_ref, gconv_ref, mC_ref, mn_ref, mm_ref, hS_ref,
                cw_ref, sp_ref, nrm_ref, lb_ref,
                om_ref, gS_o, mC_o, mn_o, mm_o, hS_o,
                xpad_ref, *, bb, c, lv):
    G = bb * N_HEADS
    gh = [(bi, h) for bi in range(bb) for h in range(N_HEADS)]

    @pl.when(pl.program_id(1) == 0)
    def _():
        gS_o[...] = gS_ref[...]
        mC_o[...] = mC_ref[...]
        mn_o[...] = mn_ref[...]
        mm_o[...] = mm_ref[...]
        hS_o[...] = hS_ref[...]
        xpad_ref[:, 5:8, :] = gconv_ref[...]

    r2 = lax.broadcasted_iota(jnp.int32, (c, c), 0)
    s2 = lax.broadcasted_iota(jnp.int32, (c, c), 1)
    causal = r2 >= s2
    strict = r2 > s2
    tri_bf = causal.astype(BF16)
    eye = (r2 == s2).astype(F32)
    rowc = lax.broadcasted_iota(jnp.int32, (c, 1), 0)
    live = rowc < lv
    lane = lax.broadcasted_iota(jnp.int32, (c, 128), 1)
    sub = min(SUB, c)
    nblk = c // sub
    n_sq = int(math.log2(c)) - 1

    def heads(col0):
        return jnp.stack([pm_ref[bi, :, col0 + h * HEAD_DIM:col0 + (h + 1) * HEAD_DIM] for bi, h in gh])

    def head_rows(ref, r):
        return jnp.stack([ref[r:r + 1, h * HEAD_DIM:(h + 1) * HEAD_DIM] for _, h in gh])

    def put(col0, val):
        for g, (bi, h) in enumerate(gh):
            om_ref[bi, :, col0 + h * HEAD_DIM:col0 + (h + 1) * HEAD_DIM] = val[g]

    alog = sp_ref[0:1, :]
    bias = sp_ref[1:2, :]
    gates, cums, gates_t, cums_t, convs = [], [], [], [], []
    for bi in range(bb):
        ps = ps_ref[bi]
        z = ps + bias
        gt = jnp.where(lane < 4, jax.nn.sigmoid(ps),
                       jnp.where(lane < 8, -jnp.exp(alog) * jax.nn.softplus(z),
                                 jnp.where(lane < 12, z, jax.nn.log_sigmoid(z))))
        cm = _cumsum_rows(tri_bf, gt)
        gates.append(gt)
        cums.append(cm)
        gates_t.append(gt.T)
        cums_t.append(cm.T)
        xpad_ref[bi, 8:8 + c, :] = pm_ref[bi, :, 0:3 * MIX_W]
        acc = cw_ref[0:1, :] * xpad_ref[bi, 5:5 + c, :]
        for j in range(1, CONV_W):
            acc = acc + cw_ref[j:j + 1, :] * xpad_ref[bi, 5 + j:5 + j + c, :]
        convs.append(jax.nn.silu(acc))
        xpad_ref[bi, 5:8, :] = xpad_ref[bi, 5 + c:8 + c, :]

    col = lambda arrs, j: jnp.stack([arrs[bi][:, j + h:j + h + 1] for bi, h in gh])
    row = lambda arrs, j: jnp.stack([arrs[bi][j + h:j + h + 1, :] for bi, h in gh])
    part = lambda p: jnp.stack([convs[bi][:, p * MIX_W + h * HEAD_DIM:p * MIX_W + (h + 1) * HEAD_DIM]
                                for bi, h in gh])

    q = part(0)
    k = part(1)
    v = part(2)
    q = q * lax.rsqrt(jnp.sum(q * q, -1, keepdims=True) + EPS) * HEAD_DIM ** -0.5
    k = k * lax.rsqrt(jnp.sum(k * k, -1, keepdims=True) + EPS)
    beta = col(gates, 0)
    Gc = col(cums, 4)
    Gr = row(cums_t, 4)
    decay = jnp.exp(jnp.where(causal, Gc - Gr, -jnp.inf))
    S = gS_o[...].reshape(G, HEAD_DIM, HEAD_DIM)
    qk2 = jnp.concatenate([q, k], axis=1)
    P = _bmm_nt(qk2, k)
    R = _bmm(qk2, S)
    QK, KK = P[:, :c], P[:, c:]
    QS, KS = R[:, :c], R[:, c:]
    eG = jnp.exp(Gc)
    rhs = beta * (v - eG * KS)
    Mp = -jnp.where(strict, beta * decay * KK, 0.0)
    Tinv = eye + Mp
    Ms = _split2(Mp)
    for _ in range(n_sq):
        Mp = _bmm_split(Ms, Ms)
        Ms = _split2(Mp)
        Tinv = Tinv + _bmm_split(_split2(Tinv), Ms)
    u = _bmm_split(_split2(Tinv), _split2(rhs))
    o = eG * QS + _bmm(QK * decay, u)
    Gl = Gc[:, lv - 1:lv, :]
    wl = jnp.where(live, jnp.exp(Gl - Gc), 0.0)
    gS_o[...] = (jnp.exp(Gl) * S + _bmm_tn(k * wl, u)).reshape(bb, N_HEADS, HEAD_DIM, HEAD_DIM)
    put(0, _rms(o, head_rows(nrm_ref, 0)) * jax.nn.silu(heads(3 * MIX_W)))

    q = heads(4 * MIX_W)
    k = heads(5 * MIX_W) * HEAD_DIM ** -0.5
    v = heads(6 * MIX_W)
    ig_c = col(gates, 8)
    ig_r = row(gates_t, 8)
    Fc = col(cums, 12)
    Fr = row(cums_t, 12)
    m0 = jnp.stack([mm_o[bi, h:h + 1, 0:1] for bi, h in gh])
    n_row = jnp.stack([mn_o[bi, h:h + 1, :] for bi, h in gh])
    C = mC_o[...].reshape(G, HEAD_DIM, HEAD_DIM)
    logD = jnp.where(causal, Fc - Fr + ig_r, -jnp.inf)
    b = Fc + m0
    mt = jnp.maximum(b, jnp.max(logD, -1, keepdims=True))
    s = _bmm_nt(q, k) * jnp.exp(logD - mt)
    inter = jnp.exp(b - mt)
    num = _bmm(s, v) + inter * _bmm(q, C)
    den = jnp.sum(s, -1, keepdims=True) + inter * jnp.sum(q * n_row, -1, keepdims=True)
    hm = num / jnp.maximum(jnp.abs(den), jnp.exp(-mt))
    m_new = mt[:, lv - 1:lv, :]
    Fl = Fc[:, lv - 1:lv, :]
    wl = jnp.where(live, jnp.exp(Fl - Fc + ig_c - m_new), 0.0)
    d0 = jnp.exp(Fl + m0 - m_new)
    kw = k * wl
    mC_o[...] = (d0 * C + _bmm_tn(kw, v)).reshape(bb, N_HEADS, HEAD_DIM, HEAD_DIM)
    n_new = d0 * n_row + jnp.sum(kw, axis=1, keepdims=True)
    m_b = jnp.broadcast_to(m_new, (G, 1, HEAD_DIM))
    for g, (bi, h) in enumerate(gh):
        mn_o[bi, h:h + 1, :] = n_new[g]
        mm_o[bi, h:h + 1, :] = m_b[g]
    put(MIX_W, _rms(hm, head_rows(nrm_ref, 1)) * jax.nn.sigmoid(heads(7 * MIX_W)))

    hq = jax.nn.silu(heads(8 * MIX_W))
    hi = heads(10 * MIX_W)
    lbh = lb_ref[...]
    Gs_l, kk_l = [], []
    for bi in range(bb):
        fg = lbh + (1.0 - lbh) * jax.nn.sigmoid(pm_ref[bi, :, 9 * MIX_W:10 * MIX_W])
        kk_l.append(1.0 - fg)
        Gs_l.append(_cumsum_rows(tri_bf, jnp.log(fg)))
    kk = jnp.stack([kk_l[bi][:, h * HEAD_DIM:(h + 1) * HEAD_DIM] for bi, h in gh])
    Gm = jnp.stack([Gs_l[bi][:, h * HEAD_DIM:(h + 1) * HEAD_DIM] for bi, h in gh])
    S = hS_o[...].reshape(G, HEAD_DIM, HEAD_DIM)
    blk0 = jnp.bitwise_and(rowc, -sub)
    A = jnp.zeros((G, c, c), F32)
    for dlt in range(sub):
        ks = kk if dlt == 0 else pltpu.roll(kk, dlt, 1)
        Gs = Gm if dlt == 0 else pltpu.roll(Gm, dlt, 1)
        d = jnp.sum(hq * ks * jnp.exp(jnp.minimum(Gm - Gs, 0.0)), -1, keepdims=True)
        A = jnp.where((s2 == r2 - dlt) & (rowc - dlt >= blk0), d, A)
    if nblk > 1:
        Gref = jnp.concatenate(
            [jnp.broadcast_to(Gm[:, I * sub:I * sub + 1, :], (G, sub, HEAD_DIM)) for I in range(nblk)], axis=1)
        qt = hq * jnp.exp(Gm - Gref)
        pieces = [jnp.zeros((G, sub, c), F32)]
        for I in range(1, nblk):
            kt = kk * jnp.exp(jnp.minimum(Gm[:, I * sub:I * sub + 1, :] - Gm, 0.0))
            pieces.append(_bmm_nt(qt[:, I * sub:(I + 1) * sub], kt))
        A = jnp.where(s2 < jnp.bitwise_and(r2, -sub), jnp.concatenate(pieces, axis=1), A)
    o = _bmm(hq * jnp.exp(Gm), S) + _bmm(A, hi)
    Gl = Gm[:, lv - 1:lv, :]
    kw = jnp.where(live, kk * jnp.exp(Gl - Gm), 0.0)
    dcol = jnp.exp(jnp.swapaxes(jnp.broadcast_to(Gl, (G, 8, HEAD_DIM)), 1, 2)[:, :, 0:1])
    hS_o[...] = (dcol * S + _bmm_tn(kw, hi)).reshape(bb, N_HEADS, HEAD_DIM, HEAD_DIM)
    put(2 * MIX_W, _rms(o, head_rows(nrm_ref, 2)) * jax.nn.silu(heads(11 * MIX_W)))


def _mixers(pm3, ps3, gS, gconv, mC, mn, mm, hS, cw, sp, nrm, lb, l, *, bb, c, lv):
    Bt, Lp, _ = pm3.shape
    kern = functools.partial(_mix_kernel, bb=bb, c=c, lv=lv)
    st4 = pl.BlockSpec((bb, N_HEADS, HEAD_DIM, HEAD_DIM), lambda b, j: (b, 0, 0, 0))
    st3 = pl.BlockSpec((bb, N_HEADS, HEAD_DIM), lambda b, j: (b, 0, 0))
    in4 = pl.BlockSpec((None, bb, N_HEADS, HEAD_DIM, HEAD_DIM), lambda b, j: (l, b, 0, 0, 0))
    in3 = pl.BlockSpec((None, bb, N_HEADS, HEAD_DIM), lambda b, j: (l, b, 0, 0))
    full = lambda a: pl.BlockSpec(a.shape, lambda b, j: (0,) * a.ndim)
    return pl.pallas_call(
        kern,
        grid=(Bt // bb, Lp // c),
        in_specs=[pl.BlockSpec((bb, c, N_MIX), lambda b, j: (b, j, 0)),
                  pl.BlockSpec((bb, c, 128), lambda b, j: (b, j, 0)),
                  in4,
                  pl.BlockSpec((None, bb, CONV_W - 1, 3 * MIX_W), lambda b, j: (l, b, 0, 0)),
                  in4, in3, in3, in4,
                  full(cw), full(sp), full(nrm), full(lb)],
        out_specs=[pl.BlockSpec((bb, c, 3 * MIX_W), lambda b, j: (b, j, 0)),
                   st4, st4, st3, st3, st4],
        out_shape=[jax.ShapeDtypeStruct((Bt, Lp, 3 * MIX_W), F32),
                   jax.ShapeDtypeStruct(gS.shape[1:], F32),
                   jax.ShapeDtypeStruct(mC.shape[1:], F32),
                   jax.ShapeDtypeStruct(mn.shape[1:], F32),
                   jax.ShapeDtypeStruct(mm.shape[1:], F32),
                   jax.ShapeDtypeStruct(hS.shape[1:], F32)],
        scratch_shapes=[pltpu.VMEM((bb, c + 8, 3 * MIX_W), F32)],
        compiler_params=_params(2),
        name="mixers",
    )(pm3, ps3, gS, gconv, mC, mn, mm, hS, cw, sp, nrm, lb)


def _merge_kernel(om_ref, pg_ref, x_ref, wbr_ref, wout_ref, ln_ref, x1_ref, hf_ref):
    acc = None
    for n in range(N_BRANCH):
        br = jnp.dot(om_ref[:, n * MIX_W:(n + 1) * MIX_W].astype(BF16), wbr_ref[n],
                     preferred_element_type=F32)
        t = jax.nn.sigmoid(pg_ref[:, n * D_MODEL:(n + 1) * D_MODEL]) * br
        acc = t if acc is None else acc + t
    x1 = x_ref[...] + jnp.dot(acc.astype(BF16), wout_ref[...], preferred_element_type=F32)
    x1_ref[...] = x1
    hf_ref[...] = _rms(x1, ln_ref[...]).astype(BF16)


def _merge(om, pm, x, w_br, w_out, ln, l, tm=256):
    T = x.shape[0]
    return pl.pallas_call(
        _merge_kernel,
        grid=(T // tm,),
        in_specs=[pl.BlockSpec((tm, 3 * MIX_W), lambda i: (i, 0)),
                  pl.BlockSpec((tm, N_BRANCH * D_MODEL), lambda i: (i, N_MIX // (N_BRANCH * D_MODEL))),
                  pl.BlockSpec((tm, D_MODEL), lambda i: (i, 0)),
                  pl.BlockSpec((None, N_BRANCH, MIX_W, D_MODEL), lambda i: (l, 0, 0, 0)),
                  pl.BlockSpec((None, D_MODEL, D_MODEL), lambda i: (l, 0, 0)),
                  pl.BlockSpec((1, D_MODEL), lambda i: (0, 0))],
        out_specs=[pl.BlockSpec((tm, D_MODEL), lambda i: (i, 0)),
                   pl.BlockSpec((tm, D_MODEL), lambda i: (i, 0))],
        out_shape=[jax.ShapeDtypeStruct((T, D_MODEL), F32),
                   jax.ShapeDtypeStruct((T, D_MODEL), BF16)],
        compiler_params=_params(1),
        name="merge",
    )(om, pm, x, w_br, w_out, ln)


def _up_kernel(h_ref, w_ref, u_ref):
    u_ref[...] = jnp.dot(h_ref[...], w_ref[...], preferred_element_type=F32)


def _ffn_up(hf, w_up, l, tm=1024, tn=512):
    T = hf.shape[0]
    N = w_up.shape[2]
    return pl.pallas_call(
        _up_kernel,
        grid=(T // tm, N // tn),
        in_specs=[pl.BlockSpec((tm, D_MODEL), lambda i, j: (i, 0)),
                  pl.BlockSpec((None, D_MODEL, tn), lambda i, j: (l, 0, j))],
        out_specs=pl.BlockSpec((tm, tn), lambda i, j: (i, j)),
        out_shape=jax.ShapeDtypeStruct((T, N), F32),
        compiler_params=_params(2),
        name="ffn_up",
    )(hf, w_up)


def _ffn_kernel(u_ref, st_ref, cw_ref, cb_ref, wd_ref, x1_ref, ln_ref, x2_ref, y_ref, *carry,
                tm, nseq, tiles_per_seq):
    rows = tm // nseq
    if tiles_per_seq > 1:
        prev_ref, = carry

        @pl.when(pl.program_id(0) % tiles_per_seq == 0)
        def _():
            prev_ref[:, 0:2, :] = st_ref[...]
    else:
        prev_ref = st_ref

    row8 = lax.broadcasted_iota(jnp.int32, (1, 8, 1), 1)

    def conv(col0):
        cs = slice(col0, col0 + FFN_COLS)
        w0, w1, w2, cb = cw_ref[0:1, cs], cw_ref[1:2, cs], cw_ref[2:3, cs], cb_ref[:, cs]
        uc = u_ref[:, cs].reshape(nseq, rows, FFN_COLS)
        r1 = pltpu.roll(uc, 1, 1)
        r2 = pltpu.roll(uc, 2, 1)
        p0 = prev_ref[:, 0:1, cs]
        p1 = prev_ref[:, 1:2, cs]
        s1 = jnp.where(row8 == 0, p1, r1[:, 0:8])
        s2 = jnp.where(row8 == 0, p0, jnp.where(row8 == 1, p1, r2[:, 0:8]))
        y = w0 * s2 + w1 * s1 + w2 * uc[:, 0:8] + cb
        if rows > 8:
            y = jnp.concatenate([y, w0 * r2[:, 8:] + w1 * r1[:, 8:] + w2 * uc[:, 8:] + cb], axis=1)
        return y.reshape(tm, FFN_COLS)

    acc = x1_ref[...]
    for jc in range(D_FF // FFN_COLS):
        ua = conv(jc * FFN_COLS)
        ub = conv(D_FF + jc * FFN_COLS)
        act = (jax.nn.silu(ua) * ub).astype(BF16)
        acc = acc + jnp.dot(act, wd_ref[jc * FFN_COLS:(jc + 1) * FFN_COLS, :], preferred_element_type=F32)
    if tiles_per_seq > 1:
        prev_ref[0, 0:2, :] = u_ref[tm - 2:tm, :]
    x2_ref[...] = acc
    y_ref[...] = _rms(acc, ln_ref[...])


def _ffn_down(u, st, cw, cb, w_down, x1, ln, l, rows_per_seq, tm):
    T = u.shape[0]
    nseq = max(1, tm // rows_per_seq)
    tiles_per_seq = max(1, rows_per_seq // tm)
    kern = functools.partial(_ffn_kernel, tm=tm, nseq=nseq, tiles_per_seq=tiles_per_seq)
    return pl.pallas_call(
        kern,
        grid=(T // tm,),
        in_specs=[pl.BlockSpec((tm, 2 * D_FF), lambda i: (i, 0)),
                  pl.BlockSpec((nseq, FFN_CONV_W - 1, 2 * D_FF), lambda i: (i // tiles_per_seq, 0, 0)),
                  pl.BlockSpec((FFN_CONV_W, 2 * D_FF), lambda i: (0, 0)),
                  pl.BlockSpec((1, 2 * D_FF), lambda i: (0, 0)),
                  pl.BlockSpec((None, D_FF, D_MODEL), lambda i: (l, 0, 0)),
                  pl.BlockSpec((tm, D_MODEL), lambda i: (i, 0)),
                  pl.BlockSpec((1, D_MODEL), lambda i: (0, 0))],
        out_specs=[pl.BlockSpec((tm, D_MODEL), lambda i: (i, 0)),
                   pl.BlockSpec((tm, D_MODEL), lambda i: (i, 0))],
        out_shape=[jax.ShapeDtypeStruct((T, D_MODEL), F32),
                   jax.ShapeDtypeStruct((T, D_MODEL), F32)],
        scratch_shapes=[pltpu.VMEM((1, 8, 2 * D_FF), F32)] if tiles_per_seq > 1 else [],
        compiler_params=_params(1),
        name="ffn_down",
    )(u, st, cw, cb, w_down, x1, ln)


def _prep_weights(ln_mix, w_in, gdn_conv_w, gdn_A_log, gdn_dt_bias, gdn_norm, m_ibias, m_fbias,
                  m_norm, lb_all, hgrn_norm, w_br, w_out, ln_ffn, w_up, ffn_conv_w, ffn_conv_b, w_down):
    depth = w_in.shape[0]
    w_main = jnp.concatenate([w_in[:, :, 0:2048], w_in[:, :, 2056:4104], w_in[:, :, 4112:]], axis=2).astype(BF16)
    w_small = jnp.concatenate([w_in[:, :, 2048:2056], w_in[:, :, 4104:4112],
                               jnp.zeros((depth, D_MODEL, 128 - 4 * N_HEADS), F32)], axis=2).astype(BF16)
    z4 = jnp.zeros((depth, N_HEADS), F32)
    pad = jnp.zeros((depth, 128 - 4 * N_HEADS), F32)
    sp = jnp.zeros((depth, 8, 128), F32)
    sp = sp.at[:, 0].set(jnp.concatenate([z4, gdn_A_log, z4, z4, pad], axis=1))
    sp = sp.at[:, 1].set(jnp.concatenate([z4, gdn_dt_bias, m_ibias, m_fbias, pad], axis=1))
    return dict(
        ln_mix=ln_mix, w_main=w_main, w_small=w_small, cw=gdn_conv_w, sp=sp,
        nrm=jnp.stack([gdn_norm, m_norm, hgrn_norm], axis=1), lb=lb_all,
        w_br=w_br.astype(BF16), w_out=w_out.astype(BF16), ln_ffn=ln_ffn,
        w_up=w_up.astype(BF16), fcw=ffn_conv_w, fcb=ffn_conv_b, w_down=w_down.astype(BF16))


def _trunk(x3, states, W, ln_final, *, lv, c, bb, sample):
    Bt, Lp, _ = x3.shape
    T = Bt * Lp
    x = x3.reshape(T, D_MODEL)
    gdn_S, gdn_conv, m_C, m_n, m_m, h_S, ffn_conv = states
    depth = W["w_main"].shape[0]
    mm_in = jnp.broadcast_to(m_m[..., None], m_m.shape + (HEAD_DIM,))
    new = [[] for _ in range(7)]
    y = None
    for l in range(depth):
        pm, ps = _in_proj(x, W["ln_mix"][l][None], W["w_main"], W["w_small"], l, tm=min(T, 1024))
        pm3 = pm.reshape(Bt, Lp, N_MAIN)
        om, gS, mC, mn, mm, hS = _mixers(
            pm3, ps.reshape(Bt, Lp, 128), gdn_S, gdn_conv, m_C, m_n, mm_in, h_S,
            W["cw"][l], W["sp"][l], W["nrm"][l], W["lb"][l][None], l, bb=bb, c=c, lv=lv)
        x1, hf = _merge(om.reshape(T, 3 * MIX_W), pm, x, W["w_br"], W["w_out"], W["ln_ffn"][l][None], l)
        u = _ffn_up(hf, W["w_up"], l)
        x, y = _ffn_down(u, ffn_conv[l], W["fcw"][l], W["fcb"][l][None], W["w_down"], x1, ln_final[None], l,
                         rows_per_seq=Lp, tm=128 if sample else 256)
        end = lv if sample else Lp
        u3 = u.reshape(Bt, Lp, 2 * D_FF)
        outs = (gS, pm3[:, end - (CONV_W - 1):end, 0:3 * MIX_W], mC, mn, mm[..., 0], hS,
                u3[:, end - (FFN_CONV_W - 1):end, :])
        for acc, o in zip(new, outs):
            acc.append(o)
    return y.reshape(Bt, Lp, D_MODEL), tuple(jnp.stack(s, axis=0) for s in new)


def kernel(x_prompt, x_sample, state_gdn_S, state_gdn_conv, state_mlstm_C, state_mlstm_n, state_mlstm_m, state_hgrn_S, state_ffn_conv, ln_mix, w_in, gdn_conv_w, gdn_A_log, gdn_dt_bias, gdn_norm, m_ibias, m_fbias, m_norm, hgrn_lb, hgrn_norm, w_br, w_out, ln_ffn, w_up, ffn_conv_w, ffn_conv_b, w_down, ln_final):
    depth = w_in.shape[0]
    lb_all = jnp.cumsum(jax.nn.softmax(hgrn_lb.astype(F32), axis=0), axis=0)
    lb_all = lb_all - lb_all[0]
    layers = _prep_weights(ln_mix, w_in, gdn_conv_w, gdn_A_log, gdn_dt_bias, gdn_norm, m_ibias,
                           m_fbias, m_norm, lb_all, hgrn_norm, w_br, w_out, ln_ffn, w_up, ffn_conv_w,
                           ffn_conv_b, w_down)

    B, L, _ = x_prompt.shape
    H, Dh = N_HEADS, HEAD_DIM
    zeros = (jnp.zeros((depth, B, H, Dh, Dh), F32), jnp.zeros((depth, B, CONV_W - 1, 3 * MIX_W), F32),
             jnp.zeros((depth, B, H, Dh, Dh), F32), jnp.zeros((depth, B, H, Dh), F32),
             jnp.zeros((depth, B, H), F32), jnp.zeros((depth, B, H, Dh, Dh), F32),
             jnp.zeros((depth, B, FFN_CONV_W - 1, 2 * D_FF), F32))
    y_p, st_p = _trunk(x_prompt, zeros, layers, ln_final, lv=CHUNK, c=CHUNK, bb=2, sample=False)

    Bs, Ls, _ = x_sample.shape
    xs = jnp.pad(x_sample, ((0, 0), (0, SAMPLE_PAD - Ls), (0, 0)))
    st_in = (state_gdn_S, state_gdn_conv, state_mlstm_C, state_mlstm_n, state_mlstm_m, state_hgrn_S,
             state_ffn_conv)
    y_s, st_s = _trunk(xs, st_in, layers, ln_final, lv=Ls, c=SAMPLE_PAD, bb=8, sample=True)
    return (y_p, y_s[:, :Ls]) + st_p + st_s
```

```python
import functools
import math

import jax
import jax.numpy as jnp
from jax import lax
from jax.experimental import pallas as pl
from jax.experimental.pallas import tpu as pltpu

D_MODEL = 1024
N_HEADS = 4
HEAD_DIM = 128
MIX_W = N_HEADS * HEAD_DIM
N_BRANCH = 3
CONV_W = 4
FFN_CONV_W = 3
D_FF = 2816
CHUNK = 64
EPS = 1e-6
N_MIX = 6144
N_MAIN = 9216
SAMPLE_PAD = 8
SUB = 8
FFN_COLS = 256
LOG2E = 1.4426950408889634
LN2 = 0.6931471805599453

F32 = jnp.float32
BF16 = jnp.bfloat16
VMEM_LIMIT = 56 * 1024 * 1024


def _bmm(a, b):
    return jnp.einsum('gtd,gde->gte', a.astype(BF16), b.astype(BF16), preferred_element_type=F32)


def _bmm_nt(a, b):
    return jnp.einsum('gtd,gsd->gts', a.astype(BF16), b.astype(BF16), preferred_element_type=F32)


def _bmm_tn(a, b):
    return jnp.einsum('gsd,gse->gde', a.astype(BF16), b.astype(BF16), preferred_element_type=F32)


def _split2(a):
    hi = a.astype(BF16)
    return hi, (a - hi.astype(F32)).astype(BF16)


def _bmm_split(a, b):
    a_hi, a_lo = a
    b_hi, b_lo = b
    n = a_hi.shape[1]
    p = jnp.einsum('gtk,gks->gts', jnp.concatenate([a_hi, a_lo], axis=1), b_hi, preferred_element_type=F32)
    return p[:, :n] + p[:, n:] + jnp.einsum('gtk,gks->gts', a_hi, b_lo, preferred_element_type=F32)


def _cumsum_rows(tri_bf, x):
    n = x.shape[1]
    x1 = x.astype(BF16)
    r1 = x - x1.astype(F32)
    x2 = r1.astype(BF16)
    x3 = (r1 - x2.astype(F32)).astype(BF16)
    p = jnp.dot(tri_bf, jnp.concatenate([x1, x2, x3], axis=1), preferred_element_type=F32)
    return p[:, 0:n] + p[:, n:2 * n] + p[:, 2 * n:3 * n]


def _rms(x, g):
    return x * lax.rsqrt(jnp.mean(x * x, -1, keepdims=True) + EPS) * g


def _params(n_axes):
    return pltpu.CompilerParams(dimension_semantics=("arbitrary",) * n_axes,
                                vmem_limit_bytes=VMEM_LIMIT)


def _in_kernel(x_ref, ln_ref, wm_ref, ws_ref, pm_ref, ps_ref, h_ref):
    @pl.when(pl.program_id(1) == 0)
    def _():
        hb = _rms(x_ref[...], ln_ref[...]).astype(BF16)
        h_ref[...] = hb
        ps_ref[...] = jnp.dot(hb, ws_ref[...], preferred_element_type=F32)

    pm_ref[...] = jnp.dot(h_ref[...], wm_ref[...], preferred_element_type=F32)


def _in_proj(x, ln, w_main, w_small, l, tm, tn=1024):
    T = x.shape[0]
    return pl.pallas_call(
        _in_kernel,
        grid=(T // tm, N_MAIN // tn),
        in_specs=[pl.BlockSpec((tm, D_MODEL), lambda i, j: (i, 0)),
                  pl.BlockSpec((1, D_MODEL), lambda i, j: (0, 0)),
                  pl.BlockSpec((None, D_MODEL, tn), lambda i, j: (l, 0, j)),
                  pl.BlockSpec((None, D_MODEL, 128), lambda i, j: (l, 0, 0))],
        out_specs=[pl.BlockSpec((tm, tn), lambda i, j: (i, j)),
                   pl.BlockSpec((tm, 128), lambda i, j: (i, 0))],
        out_shape=[jax.ShapeDtypeStruct((T, N_MAIN), F32),
                   jax.ShapeDtypeStruct((T, 128), F32)],
        scratch_shapes=[pltpu.VMEM((tm, D_MODEL), BF16)],
        compiler_params=_params(2),
        name="in_proj",
    )(x, ln, w_main, w_small)


def _mix_kernel(pm_ref, ps_ref, gS_ref, gconv_ref, mC_ref, mn_ref, mm_ref, hS_ref,
                cw_ref, sp_ref, nrm_ref, lb_ref,
                om_ref, gS_o, mC_o, mn_o, mm_o, hS_o,
                xpad_ref, *, bb, c, lv):
    G = bb * N_HEADS
    gh = [(bi, h) for bi in range(bb) for h in range(N_HEADS)]

    @pl.when(pl.program_id(1) == 0)
    def _():
        gS_o[...] = gS_ref[...]
        mC_o[...] = mC_ref[...]
        mn_o[...] = mn_ref[...]
        mm_o[...] = mm_ref[...]
        hS_o[...] = hS_ref[...]
        xpad_ref[:, 5:8, :] = gconv_ref[...]

    r2 = lax.broadcasted_iota(jnp.int32, (c, c), 0)
    s2 = lax.broadcasted_iota(jnp.int32, (c, c), 1)
    causal = r2 >= s2
    strict = r2 > s2
    tri_bf = causal.astype(BF16)
    eye = (r2 == s2).astype(F32)
    rowc = lax.broadcasted_iota(jnp.int32, (c, 1), 0)
    live = rowc < lv
    lane = lax.broadcasted_iota(jnp.int32, (c, 128), 1)
    sub = min(SUB, c)
    nblk = c // sub
    n_sq = int(math.log2(c)) - 1

    def heads(col0):
        return jnp.stack([pm_ref[bi, :, col0 + h * HEAD_DIM:col0 + (h + 1) * HEAD_DIM] for bi, h in gh])

    def head_rows(ref, r):
        return jnp.stack([ref[r:r + 1, h * HEAD_DIM:(h + 1) * HEAD_DIM] for _, h in gh])

    def put(col0, val):
        for g, (bi, h) in enumerate(gh):
            om_ref[bi, :, col0 + h * HEAD_DIM:col0 + (h + 1) * HEAD_DIM] = val[g]

    alog = sp_ref[0:1, :]
    bias = sp_ref[1:2, :]
    gates, gates2, cums2, gates2_t, cums2_t, convs = [], [], [], [], [], []
    for bi in range(bb):
        ps = ps_ref[bi]
        z = ps + bias
        gt = jnp.where(lane < 4, jax.nn.sigmoid(ps),
                       jnp.where(lane < 8, -jnp.exp(alog) * jax.nn.softplus(z),
                                 jnp.where(lane < 12, z, jax.nn.log_sigmoid(z))))
        gt2 = gt * LOG2E
        cm2 = _cumsum_rows(tri_bf, gt) * LOG2E
        gates.append(gt)
        gates2.append(gt2)
        cums2.append(cm2)
        gates2_t.append(gt2.T)
        cums2_t.append(cm2.T)
        xpad_ref[bi, 8:8 + c, :] = pm_ref[bi, :, 0:3 * MIX_W]
        acc = cw_ref[0:1, :] * xpad_ref[bi, 5:5 + c, :]
        for j in range(1, CONV_W):
            acc = acc + cw_ref[j:j + 1, :] * xpad_ref[bi, 5 + j:5 + j + c, :]
        convs.append(jax.nn.silu(acc))
        xpad_ref[bi, 5:8, :] = xpad_ref[bi, 5 + c:8 + c, :]

    col = lambda arrs, j: jnp.stack([arrs[bi][:, j + h:j + h + 1] for bi, h in gh])
    row = lambda arrs, j: jnp.stack([arrs[bi][j + h:j + h + 1, :] for bi, h in gh])
    part = lambda p: jnp.stack([convs[bi][:, p * MIX_W + h * HEAD_DIM:p * MIX_W + (h + 1) * HEAD_DIM]
                                for bi, h in gh])


    def gdn():
        q = part(0)
        k = part(1)
        v = part(2)
        q = q * lax.rsqrt(jnp.sum(q * q, -1, keepdims=True) + EPS) * HEAD_DIM ** -0.5
        k = k * lax.rsqrt(jnp.sum(k * k, -1, keepdims=True) + EPS)
        beta = col(gates, 0)
        Gc = col(cums2, 4)
        Gr = row(cums2_t, 4)
        decay = jnp.exp2(jnp.where(causal, Gc - Gr, -jnp.inf))
        S = gS_o[...].reshape(G, HEAD_DIM, HEAD_DIM)
        qk2 = jnp.concatenate([q, k], axis=1)
        P = _bmm_nt(qk2, k)
        R = _bmm(qk2, S)
        QK, KK = P[:, :c], P[:, c:]
        QS, KS = R[:, :c], R[:, c:]
        eG = jnp.exp2(Gc)
        rhs = beta * (v - eG * KS)
        Mp = -jnp.where(strict, beta * decay * KK, 0.0)
        Tinv = eye + Mp
        Ms = _split2(Mp)
        yield
        for _ in range(n_sq):
            Mp = _bmm_split(Ms, Ms)
            Ms = _split2(Mp)
            Tinv = Tinv + _bmm_split(_split2(Tinv), Ms)
            yield
        u = _bmm_split(_split2(Tinv), _split2(rhs))
        yield
        o = eG * QS + _bmm(QK * decay, u)
        Gl = Gc[:, lv - 1:lv, :]
        wl = jnp.where(live, jnp.exp2(Gl - Gc), 0.0)
        gS_o[...] = (jnp.exp2(Gl) * S + _bmm_tn(k * wl, u)).reshape(bb, N_HEADS, HEAD_DIM, HEAD_DIM)
        put(0, _rms(o, head_rows(nrm_ref, 0)) * jax.nn.silu(heads(3 * MIX_W)))

    def mlstm():
        q = heads(4 * MIX_W)
        k = heads(5 * MIX_W) * HEAD_DIM ** -0.5
        v = heads(6 * MIX_W)
        ig_c = col(gates2, 8)
        ig_r = row(gates2_t, 8)
        Fc = col(cums2, 12)
        Fr = row(cums2_t, 12)
        m0 = jnp.stack([mm_o[bi, h:h + 1, 0:1] for bi, h in gh])
        n_row = jnp.stack([mn_o[bi, h:h + 1, :] for bi, h in gh])
        C = mC_o[...].reshape(G, HEAD_DIM, HEAD_DIM)
        logD = jnp.where(causal, Fc - Fr + ig_r, -jnp.inf)
        m02 = m0 * LOG2E
        b = Fc + m02
        mt = jnp.maximum(b, jnp.max(logD, -1, keepdims=True))
        s = _bmm_nt(q, k) * jnp.exp2(logD - mt)
        inter = jnp.exp2(b - mt)
        yield
        num = _bmm(s, v) + inter * _bmm(q, C)
        den = jnp.sum(s, -1, keepdims=True) + inter * jnp.sum(q * n_row, -1, keepdims=True)
        hm = num / jnp.maximum(jnp.abs(den), jnp.exp2(-mt))
        yield
        mt_l = mt[:, lv - 1:lv, :]
        Fl = Fc[:, lv - 1:lv, :]
        wl = jnp.where(live, jnp.exp2(Fl - Fc + ig_c - mt_l), 0.0)
        d0 = jnp.exp2(Fl + m02 - mt_l)
        kw = k * wl
        mC_o[...] = (d0 * C + _bmm_tn(kw, v)).reshape(bb, N_HEADS, HEAD_DIM, HEAD_DIM)
        n_new = d0 * n_row + jnp.sum(kw, axis=1, keepdims=True)
        m_b = jnp.broadcast_to(mt_l * LN2, (G, 1, HEAD_DIM))
        for g, (bi, h) in enumerate(gh):
            mn_o[bi, h:h + 1, :] = n_new[g]
            mm_o[bi, h:h + 1, :] = m_b[g]
        yield
        put(MIX_W, _rms(hm, head_rows(nrm_ref, 1)) * jax.nn.sigmoid(heads(7 * MIX_W)))

    def hgrn():
        hq = jax.nn.silu(heads(8 * MIX_W))
        hi = heads(10 * MIX_W)
        lbh = lb_ref[...]
        Gs_l, kk_l = [], []
        for bi in range(bb):
            fg = lbh + (1.0 - lbh) * jax.nn.sigmoid(pm_ref[bi, :, 9 * MIX_W:10 * MIX_W])
            kk_l.append(1.0 - fg)
            Gs_l.append(_cumsum_rows(tri_bf, jnp.log2(fg)))
        kk = jnp.stack([kk_l[bi][:, h * HEAD_DIM:(h + 1) * HEAD_DIM] for bi, h in gh])
        Gm = jnp.stack([Gs_l[bi][:, h * HEAD_DIM:(h + 1) * HEAD_DIM] for bi, h in gh])
        S = hS_o[...].reshape(G, HEAD_DIM, HEAD_DIM)
        yield
        tiles = lambda a: a.reshape(G * nblk, sub, HEAD_DIM)
        hq_t, kk_t, Gm_t = tiles(hq), tiles(kk), tiles(Gm)
        A = jnp.zeros((G, c, c), F32)
        for dlt in range(sub):
            ks = kk_t if dlt == 0 else pltpu.roll(kk_t, dlt, 1)
            Gs = Gm_t if dlt == 0 else pltpu.roll(Gm_t, dlt, 1)
            d = jnp.sum(hq_t * ks * jnp.exp2(jnp.minimum(Gm_t - Gs, 0.0)), -1, keepdims=True)
            A = jnp.where((s2 == r2 - dlt) & (jnp.bitwise_and(r2, sub - 1) >= dlt), d.reshape(G, c, 1), A)
            if dlt % 2 == 1:
                yield
        if nblk > 1:
            Gref = jnp.concatenate(
                [jnp.broadcast_to(Gm[:, I * sub:I * sub + 1, :], (G, sub, HEAD_DIM)) for I in range(nblk)], axis=1)
            qt = hq * jnp.exp2(Gm - Gref)
            pieces = [jnp.zeros((G, sub, c), F32)]
            for I in range(1, nblk):
                kt = kk * jnp.exp2(jnp.minimum(Gm[:, I * sub:I * sub + 1, :] - Gm, 0.0))
                pieces.append(_bmm_nt(qt[:, I * sub:(I + 1) * sub], kt))
                if I % 2 == 1:
                    yield
            A = jnp.where(s2 < jnp.bitwise_and(r2, -sub), jnp.concatenate(pieces, axis=1), A)
        o = _bmm(hq * jnp.exp2(Gm), S) + _bmm(A, hi)
        Gl = Gm[:, lv - 1:lv, :]
        kw = jnp.where(live, kk * jnp.exp2(Gl - Gm), 0.0)
        dcol = jnp.exp2(jnp.swapaxes(jnp.broadcast_to(Gl, (G, 8, HEAD_DIM)), 1, 2)[:, :, 0:1])
        hS_o[...] = (dcol * S + _bmm_tn(kw, hi)).reshape(bb, N_HEADS, HEAD_DIM, HEAD_DIM)
        yield
        put(2 * MIX_W, _rms(o, head_rows(nrm_ref, 2)) * jax.nn.silu(heads(11 * MIX_W)))

    streams = [gdn(), hgrn(), mlstm()]
    while streams:
        for st in list(streams):
            if next(st, "done") == "done":
                streams.remove(st)


def _mixers(pm3, ps3, gS, gconv, mC, mn, mm, hS, cw, sp, nrm, lb, l, *, bb, c, lv):
    Bt, Lp, _ = pm3.shape
    kern = functools.partial(_mix_kernel, bb=bb, c=c, lv=lv)
    st4 = pl.BlockSpec((bb, N_HEADS, HEAD_DIM, HEAD_DIM), lambda b, j: (b, 0, 0, 0))
    st3 = pl.BlockSpec((bb, N_HEADS, HEAD_DIM), lambda b, j: (b, 0, 0))
    in4 = pl.BlockSpec((None, bb, N_HEADS, HEAD_DIM, HEAD_DIM), lambda b, j: (l, b, 0, 0, 0))
    in3 = pl.BlockSpec((None, bb, N_HEADS, HEAD_DIM), lambda b, j: (l, b, 0, 0))
    full = lambda a: pl.BlockSpec(a.shape, lambda b, j: (0,) * a.ndim)
    return pl.pallas_call(
        kern,
        grid=(Bt // bb, Lp // c),
        in_specs=[pl.BlockSpec((bb, c, N_MIX), lambda b, j: (b, j, 0)),
                  pl.BlockSpec((bb, c, 128), lambda b, j: (b, j, 0)),
                  in4,
                  pl.BlockSpec((None, bb, CONV_W - 1, 3 * MIX_W), lambda b, j: (l, b, 0, 0)),
                  in4, in3, in3, in4,
                  full(cw), full(sp), full(nrm), full(lb)],
        out_specs=[pl.BlockSpec((bb, c, 3 * MIX_W), lambda b, j: (b, j, 0)),
                   st4, st4, st3, st3, st4],
        out_shape=[jax.ShapeDtypeStruct((Bt, Lp, 3 * MIX_W), F32),
                   jax.ShapeDtypeStruct(gS.shape[1:], F32),
                   jax.ShapeDtypeStruct(mC.shape[1:], F32),
                   jax.ShapeDtypeStruct(mn.shape[1:], F32),
                   jax.ShapeDtypeStruct(mm.shape[1:], F32),
                   jax.ShapeDtypeStruct(hS.shape[1:], F32)],
        scratch_shapes=[pltpu.VMEM((bb, c + 8, 3 * MIX_W), F32)],
        compiler_params=_params(2),
        name="mixers",
    )(pm3, ps3, gS, gconv, mC, mn, mm, hS, cw, sp, nrm, lb)


def _merge_kernel(om_ref, pg_ref, x_ref, wbr_ref, wout_ref, ln_ref, x1_ref, hf_ref):
    acc = None
    for n in range(N_BRANCH):
        br = jnp.dot(om_ref[:, n * MIX_W:(n + 1) * MIX_W].astype(BF16), wbr_ref[n],
                     preferred_element_type=F32)
        t = jax.nn.sigmoid(pg_ref[:, n * D_MODEL:(n + 1) * D_MODEL]) * br
        acc = t if acc is None else acc + t
    x1 = x_ref[...] + jnp.dot(acc.astype(BF16), wout_ref[...], preferred_element_type=F32)
    x1_ref[...] = x1
    hf_ref[...] = _rms(x1, ln_ref[...]).astype(BF16)


def _merge(om, pm, x, w_br, w_out, ln, l, tm=256):
    T = x.shape[0]
    return pl.pallas_call(
        _merge_kernel,
        grid=(T // tm,),
        in_specs=[pl.BlockSpec((tm, 3 * MIX_W), lambda i: (i, 0)),
                  pl.BlockSpec((tm, N_BRANCH * D_MODEL), lambda i: (i, N_MIX // (N_BRANCH * D_MODEL))),
                  pl.BlockSpec((tm, D_MODEL), lambda i: (i, 0)),
                  pl.BlockSpec((None, N_BRANCH, MIX_W, D_MODEL), lambda i: (l, 0, 0, 0)),
                  pl.BlockSpec((None, D_MODEL, D_MODEL), lambda i: (l, 0, 0)),
                  pl.BlockSpec((1, D_MODEL), lambda i: (0, 0))],
        out_specs=[pl.BlockSpec((tm, D_MODEL), lambda i: (i, 0)),
                   pl.BlockSpec((tm, D_MODEL), lambda i: (i, 0))],
        out_shape=[jax.ShapeDtypeStruct((T, D_MODEL), F32),
                   jax.ShapeDtypeStruct((T, D_MODEL), BF16)],
        compiler_params=_params(1),
        name="merge",
    )(om, pm, x, w_br, w_out, ln)


def _up_kernel(h_ref, w_ref, u_ref):
    u_ref[...] = jnp.dot(h_ref[...], w_ref[...], preferred_element_type=F32)


def _ffn_up(hf, w_up, l, tm=1024, tn=512):
    T = hf.shape[0]
    N = w_up.shape[2]
    return pl.pallas_call(
        _up_kernel,
        grid=(T // tm, N // tn),
        in_specs=[pl.BlockSpec((tm, D_MODEL), lambda i, j: (i, 0)),
                  pl.BlockSpec((None, D_MODEL, tn), lambda i, j: (l, 0, j))],
        out_specs=pl.BlockSpec((tm, tn), lambda i, j: (i, j)),
        out_shape=jax.ShapeDtypeStruct((T, N), F32),
        compiler_params=_params(2),
        name="ffn_up",
    )(hf, w_up)


def _ffn_kernel(u_ref, st_ref, cw_ref, cb_ref, wd_ref, x1_ref, ln_ref, x2_ref, y_ref, *carry,
                tm, nseq, tiles_per_seq):
    rows = tm // nseq
    if tiles_per_seq > 1:
        prev_ref, = carry

        @pl.when(pl.program_id(0) % tiles_per_seq == 0)
        def _():
            prev_ref[:, 0:2, :] = st_ref[...]
    else:
        prev_ref = st_ref

    row8 = lax.broadcasted_iota(jnp.int32, (1, 8, 1), 1)

    def conv(col0):
        cs = slice(col0, col0 + FFN_COLS)
        w0, w1, w2, cb = cw_ref[0:1, cs], cw_ref[1:2, cs], cw_ref[2:3, cs], cb_ref[:, cs]
        uc = u_ref[:, cs].reshape(nseq, rows, FFN_COLS)
        r1 = pltpu.roll(uc, 1, 1)
        r2 = pltpu.roll(uc, 2, 1)
        p0 = prev_ref[:, 0:1, cs]
        p1 = prev_ref[:, 1:2, cs]
        s1 = jnp.where(row8 == 0, p1, r1[:, 0:8])
        s2 = jnp.where(row8 == 0, p0, jnp.where(row8 == 1, p1, r2[:, 0:8]))
        y = w0 * s2 + w1 * s1 + w2 * uc[:, 0:8] + cb
        if rows > 8:
            y = jnp.concatenate([y, w0 * r2[:, 8:] + w1 * r1[:, 8:] + w2 * uc[:, 8:] + cb], axis=1)
        return y.reshape(tm, FFN_COLS)

    acc = x1_ref[...]
    for jc in range(D_FF // FFN_COLS):
        ua = conv(jc * FFN_COLS)
        ub = conv(D_FF + jc * FFN_COLS)
        act = (jax.nn.silu(ua) * ub).astype(BF16)
        acc = acc + jnp.dot(act, wd_ref[jc * FFN_COLS:(jc + 1) * FFN_COLS, :], preferred_element_type=F32)
    if tiles_per_seq > 1:
        prev_ref[0, 0:2, :] = u_ref[tm - 2:tm, :]
    x2_ref[...] = acc
    y_ref[...] = _rms(acc, ln_ref[...])


def _ffn_down(u, st, cw, cb, w_down, x1, ln, l, rows_per_seq, tm):
    T = u.shape[0]
    nseq = max(1, tm // rows_per_seq)
    tiles_per_seq = max(1, rows_per_seq // tm)
    kern = functools.partial(_ffn_kernel, tm=tm, nseq=nseq, tiles_per_seq=tiles_per_seq)
    return pl.pallas_call(
        kern,
        grid=(T // tm,),
        in_specs=[pl.BlockSpec((tm, 2 * D_FF), lambda i: (i, 0)),
                  pl.BlockSpec((nseq, FFN_CONV_W - 1, 2 * D_FF), lambda i: (i // tiles_per_seq, 0, 0)),
                  pl.BlockSpec((FFN_CONV_W, 2 * D_FF), lambda i: (0, 0)),
                  pl.BlockSpec((1, 2 * D_FF), lambda i: (0, 0)),
                  pl.BlockSpec((None, D_FF, D_MODEL), lambda i: (l, 0, 0)),
                  pl.BlockSpec((tm, D_MODEL), lambda i: (i, 0)),
                  pl.BlockSpec((1, D_MODEL), lambda i: (0, 0))],
        out_specs=[pl.BlockSpec((tm, D_MODEL), lambda i: (i, 0)),
                   pl.BlockSpec((tm, D_MODEL), lambda i: (i, 0))],
        out_shape=[jax.ShapeDtypeStruct((T, D_MODEL), F32),
                   jax.ShapeDtypeStruct((T, D_MODEL), F32)],
        scratch_shapes=[pltpu.VMEM((1, 8, 2 * D_FF), F32)] if tiles_per_seq > 1 else [],
        compiler_params=_params(1),
        name="ffn_down",
    )(u, st, cw, cb, w_down, x1, ln)


def _prep_weights(ln_mix, w_in, gdn_conv_w, gdn_A_log, gdn_dt_bias, gdn_norm, m_ibias, m_fbias,
                  m_norm, lb_all, hgrn_norm, w_br, w_out, ln_ffn, w_up, ffn_conv_w, ffn_conv_b, w_down):
    depth = w_in.shape[0]
    w_main = jnp.concatenate([w_in[:, :, 0:2048], w_in[:, :, 2056:4104], w_in[:, :, 4112:]], axis=2).astype(BF16)
    w_small = jnp.concatenate([w_in[:, :, 2048:2056], w_in[:, :, 4104:4112],
                               jnp.zeros((depth, D_MODEL, 128 - 4 * N_HEADS), F32)], axis=2).astype(BF16)
    z4 = jnp.zeros((depth, N_HEADS), F32)
    pad = jnp.zeros((depth, 128 - 4 * N_HEADS), F32)
    sp = jnp.zeros((depth, 8, 128), F32)
    sp = sp.at[:, 0].set(jnp.concatenate([z4, gdn_A_log, z4, z4, pad], axis=1))
    sp = sp.at[:, 1].set(jnp.concatenate([z4, gdn_dt_bias, m_ibias, m_fbias, pad], axis=1))
    return dict(
        ln_mix=ln_mix, w_main=w_main, w_small=w_small, cw=gdn_conv_w, sp=sp,
        nrm=jnp.stack([gdn_norm, m_norm, hgrn_norm], axis=1), lb=lb_all,
        w_br=w_br.astype(BF16), w_out=w_out.astype(BF16), ln_ffn=ln_ffn,
        w_up=w_up.astype(BF16), fcw=ffn_conv_w, fcb=ffn_conv_b, w_down=w_down.astype(BF16))


def _trunk(x3, states, W, ln_final, *, lv, c, bb, sample):
    Bt, Lp, _ = x3.shape
    T = Bt * Lp
    x = x3.reshape(T, D_MODEL)
    gdn_S, gdn_conv, m_C, m_n, m_m, h_S, ffn_conv = states
    depth = W["w_main"].shape[0]
    mm_in = jnp.broadcast_to(m_m[..., None], m_m.shape + (HEAD_DIM,))
    new = [[] for _ in range(7)]
    y = None
    for l in range(depth):
        pm, ps = _in_proj(x, W["ln_mix"][l][None], W["w_main"], W["w_small"], l, tm=min(T, 1024))
        pm3 = pm.reshape(Bt, Lp, N_MAIN)
        om, gS, mC, mn, mm, hS = _mixers(
            pm3, ps.reshape(Bt, Lp, 128), gdn_S, gdn_conv, m_C, m_n, mm_in, h_S,
            W["cw"][l], W["sp"][l], W["nrm"][l], W["lb"][l][None], l, bb=bb, c=c, lv=lv)
        x1, hf = _merge(om.reshape(T, 3 * MIX_W), pm, x, W["w_br"], W["w_out"], W["ln_ffn"][l][None], l)
        u = _ffn_up(hf, W["w_up"], l)
        x, y = _ffn_down(u, ffn_conv[l], W["fcw"][l], W["fcb"][l][None], W["w_down"], x1, ln_final[None], l,
                         rows_per_seq=Lp, tm=128 if sample else 256)
        end = lv if sample else Lp
        u3 = u.reshape(Bt, Lp, 2 * D_FF)
        outs = (gS, pm3[:, end - (CONV_W - 1):end, 0:3 * MIX_W], mC, mn, mm[..., 0], hS,
                u3[:, end - (FFN_CONV_W - 1):end, :])
        for acc, o in zip(new, outs):
            acc.append(o)
    return y.reshape(Bt, Lp, D_MODEL), tuple(jnp.stack(s, axis=0) for s in new)


def kernel(x_prompt, x_sample, state_gdn_S, state_gdn_conv, state_mlstm_C, state_mlstm_n, state_mlstm_m, state_hgrn_S, state_ffn_conv, ln_mix, w_in, gdn_conv_w, gdn_A_log, gdn_dt_bias, gdn_norm, m_ibias, m_fbias, m_norm, hgrn_lb, hgrn_norm, w_br, w_out, ln_ffn, w_up, ffn_conv_w, ffn_conv_b, w_down, ln_final):
    depth = w_in.shape[0]
    lb_all = jnp.cumsum(jax.nn.softmax(hgrn_lb.astype(F32), axis=0), axis=0)
    lb_all = lb_all - lb_all[0]
    layers = _prep_weights(ln_mix, w_in, gdn_conv_w, gdn_A_log, gdn_dt_bias, gdn_norm, m_ibias,
                           m_fbias, m_norm, lb_all, hgrn_norm, w_br, w_out, ln_ffn, w_up, ffn_conv_w,
                           ffn_conv_b, w_down)

    B, L, _ = x_prompt.shape
    H, Dh = N_HEADS, HEAD_DIM
    zeros = (jnp.zeros((depth, B, H, Dh, Dh), F32), jnp.zeros((depth, B, CONV_W - 1, 3 * MIX_W), F32),
             jnp.zeros((depth, B, H, Dh, Dh), F32), jnp.zeros((depth, B, H, Dh), F32),
             jnp.zeros((depth, B, H), F32), jnp.zeros((depth, B, H, Dh, Dh), F32),
             jnp.zeros((depth, B, FFN_CONV_W - 1, 2 * D_FF), F32))
    y_p, st_p = _trunk(x_prompt, zeros, layers, ln_final, lv=CHUNK, c=CHUNK, bb=2, sample=False)

    Bs, Ls, _ = x_sample.shape
    xs = jnp.pad(x_sample, ((0, 0), (0, SAMPLE_PAD - Ls), (0, 0)))
    st_in = (state_gdn_S, state_gdn_conv, state_mlstm_C, state_mlstm_n, state_mlstm_m, state_hgrn_S,
             state_ffn_conv)
    y_s, st_s = _trunk(xs, st_in, layers, ln_final, lv=Ls, c=SAMPLE_PAD, bb=8, sample=True)
    return (y_p, y_s[:, :Ls]) + st_p + st_s
```

```python
import functools
import math

import jax
import jax.numpy as jnp
from jax import lax
from jax.experimental import pallas as pl
from jax.experimental.pallas import tpu as pltpu

D_MODEL = 1024
N_HEADS = 4
HEAD_DIM = 128
MIX_W = N_HEADS * HEAD_DIM
N_BRANCH = 3
CONV_W = 4
FFN_CONV_W = 3
D_FF = 2816
CHUNK = 64
EPS = 1e-6
N_MIX = 6144
N_MAIN = 9216
SAMPLE_PAD = 8
SUB = 8
FFN_COLS = 256
LOG2E = 1.4426950408889634
LN2 = 0.6931471805599453

F32 = jnp.float32
BF16 = jnp.bfloat16
VMEM_LIMIT = 56 * 1024 * 1024


def _bmm(a, b):
    return jnp.einsum('gtd,gde->gte', a.astype(BF16), b.astype(BF16), preferred_element_type=F32)


def _bmm_nt(a, b):
    return jnp.einsum('gtd,gsd->gts', a.astype(BF16), b.astype(BF16), preferred_element_type=F32)


def _bmm_tn(a, b):
    return jnp.einsum('gsd,gse->gde', a.astype(BF16), b.astype(BF16), preferred_element_type=F32)


def _split2(a):
    hi = a.astype(BF16)
    return hi, (a - hi.astype(F32)).astype(BF16)


def _bmm_split(a, b):
    a_hi, a_lo = a
    b_hi, b_lo = b
    n = a_hi.shape[1]
    p = jnp.einsum('gtk,gks->gts', jnp.concatenate([a_hi, a_lo], axis=1), b_hi, preferred_element_type=F32)
    return p[:, :n] + p[:, n:] + jnp.einsum('gtk,gks->gts', a_hi, b_lo, preferred_element_type=F32)


def _cumsum_rows(tri_bf, x):
    n = x.shape[1]
    x1 = x.astype(BF16)
    r1 = x - x1.astype(F32)
    x2 = r1.astype(BF16)
    x3 = (r1 - x2.astype(F32)).astype(BF16)
    p = jnp.dot(tri_bf, jnp.concatenate([x1, x2, x3], axis=1), preferred_element_type=F32)
    return p[:, 0:n] + p[:, n:2 * n] + p[:, 2 * n:3 * n]


def _rms(x, g):
    return x * lax.rsqrt(jnp.mean(x * x, -1, keepdims=True) + EPS) * g


def _params(n_axes):
    return pltpu.CompilerParams(dimension_semantics=("arbitrary",) * n_axes,
                                vmem_limit_bytes=VMEM_LIMIT)


def _in_kernel(x_ref, ln_ref, wm_ref, ws_ref, pm_ref, ps_ref, h_ref):
    @pl.when(pl.program_id(1) == 0)
    def _():
        hb = _rms(x_ref[...], ln_ref[...]).astype(BF16)
        h_ref[...] = hb
        ps_ref[...] = jnp.dot(hb, ws_ref[...], preferred_element_type=F32)

    pm_ref[...] = jnp.dot(h_ref[...], wm_ref[...], preferred_element_type=F32)


def _in_proj(x, ln, w_main, w_small, l, tm, tn=1024):
    T = x.shape[0]
    return pl.pallas_call(
        _in_kernel,
        grid=(T // tm, N_MAIN // tn),
        in_specs=[pl.BlockSpec((tm, D_MODEL), lambda i, j: (i, 0)),
                  pl.BlockSpec((1, D_MODEL), lambda i, j: (0, 0)),
                  pl.BlockSpec((None, D_MODEL, tn), lambda i, j: (l, 0, j)),
                  pl.BlockSpec((None, D_MODEL, 128), lambda i, j: (l, 0, 0))],
        out_specs=[pl.BlockSpec((tm, tn), lambda i, j: (i, j)),
                   pl.BlockSpec((tm, 128), lambda i, j: (i, 0))],
        out_shape=[jax.ShapeDtypeStruct((T, N_MAIN), F32),
                   jax.ShapeDtypeStruct((T, 128), F32)],
        scratch_shapes=[pltpu.VMEM((tm, D_MODEL), BF16)],
        compiler_params=_params(2),
        name="in_proj",
    )(x, ln, w_main, w_small)


def _mix_kernel(pm_ref, ps_ref, gS_ref, gconv_ref, mC_ref, mn_ref, mm_ref, hS_ref,
                cw_ref, sp_ref, nrm_ref, lb_ref,
                om_ref, gS_o, mC_o, mn_o, mm_o, hS_o,
                xpad_ref, *, bb, c, lv):
    G = bb * N_HEADS
    gh = [(bi, h) for bi in range(bb) for h in range(N_HEADS)]

    @pl.when(pl.program_id(1) == 0)
    def _():
        gS_o[...] = gS_ref[...]
        mC_o[...] = mC_ref[...]
        mn_o[...] = mn_ref[...]
        mm_o[...] = mm_ref[...]
        hS_o[...] = hS_ref[...]
        xpad_ref[:, 5:8, :] = gconv_ref[...]

    r2 = lax.broadcasted_iota(jnp.int32, (c, c), 0)
    s2 = lax.broadcasted_iota(jnp.int32, (c, c), 1)
    causal = r2 >= s2
    strict = r2 > s2
    tri_bf = causal.astype(BF16)
    eye = (r2 == s2).astype(F32)
    rowc = lax.broadcasted_iota(jnp.int32, (c, 1), 0)
    live = rowc < lv
    lane = lax.broadcasted_iota(jnp.int32, (c, 128), 1)
    sub = min(SUB, c)
    nblk = c // sub
    n_sq = int(math.log2(c)) - 1

    def heads(col0):
        return jnp.stack([pm_ref[bi, :, col0 + h * HEAD_DIM:col0 + (h + 1) * HEAD_DIM] for bi, h in gh])

    def head_rows(ref, r):
        return jnp.stack([ref[r:r + 1, h * HEAD_DIM:(h + 1) * HEAD_DIM] for _, h in gh])

    def put(col0, val):
        for g, (bi, h) in enumerate(gh):
            om_ref[bi, :, col0 + h * HEAD_DIM:col0 + (h + 1) * HEAD_DIM] = val[g].astype(om_ref.dtype)

    alog = sp_ref[0:1, :]
    bias = sp_ref[1:2, :]
    gates, gates2, cums2, gates2_t, cums2_t, convs = [], [], [], [], [], []
    for bi in range(bb):
        ps = ps_ref[bi]
        z = ps + bias
        gt = jnp.where(lane < 4, jax.nn.sigmoid(ps),
                       jnp.where(lane < 8, -jnp.exp(alog) * jax.nn.softplus(z),
                                 jnp.where(lane < 12, z, jax.nn.log_sigmoid(z))))
        gt2 = gt * LOG2E
        cm2 = _cumsum_rows(tri_bf, gt) * LOG2E
        gates.append(gt)
        gates2.append(gt2)
        cums2.append(cm2)
        gates2_t.append(gt2.T)
        cums2_t.append(cm2.T)
        xpad_ref[bi, 8:8 + c, :] = pm_ref[bi, :, 0:3 * MIX_W]
        acc = cw_ref[0:1, :] * xpad_ref[bi, 5:5 + c, :]
        for j in range(1, CONV_W):
            acc = acc + cw_ref[j:j + 1, :] * xpad_ref[bi, 5 + j:5 + j + c, :]
        convs.append(jax.nn.silu(acc))
        xpad_ref[bi, 5:8, :] = xpad_ref[bi, 5 + c:8 + c, :]

    col = lambda arrs, j: jnp.stack([arrs[bi][:, j + h:j + h + 1] for bi, h in gh])
    row = lambda arrs, j: jnp.stack([arrs[bi][j + h:j + h + 1, :] for bi, h in gh])
    part = lambda p: jnp.stack([convs[bi][:, p * MIX_W + h * HEAD_DIM:p * MIX_W + (h + 1) * HEAD_DIM]
                                for bi, h in gh])


    def gdn():
        q = part(0)
        k = part(1)
        v = part(2)
        q = q * lax.rsqrt(jnp.sum(q * q, -1, keepdims=True) + EPS) * HEAD_DIM ** -0.5
        k = k * lax.rsqrt(jnp.sum(k * k, -1, keepdims=True) + EPS)
        beta = col(gates, 0)
        Gc = col(cums2, 4)
        Gr = row(cums2_t, 4)
        decay = jnp.exp2(jnp.where(causal, Gc - Gr, -jnp.inf))
        S = gS_o[...].reshape(G, HEAD_DIM, HEAD_DIM)
        qk2 = jnp.concatenate([q, k], axis=1)
        P = _bmm_nt(qk2, k)
        R = _bmm(qk2, S)
        QK, KK = P[:, :c], P[:, c:]
        QS, KS = R[:, :c], R[:, c:]
        eG = jnp.exp2(Gc)
        rhs = beta * (v - eG * KS)
        Mp = -jnp.where(strict, beta * decay * KK, 0.0)
        Tinv = eye + Mp
        Ms = _split2(Mp)
        yield
        for _ in range(n_sq):
            Mp = _bmm_split(Ms, Ms)
            Ms = _split2(Mp)
            Tinv = Tinv + _bmm_split(_split2(Tinv), Ms)
            yield
        u = _bmm_split(_split2(Tinv), _split2(rhs))
        yield
        o = eG * QS + _bmm(QK * decay, u)
        Gl = Gc[:, lv - 1:lv, :]
        wl = jnp.where(live, jnp.exp2(Gl - Gc), 0.0)
        gS_o[...] = (jnp.exp2(Gl) * S + _bmm_tn(k * wl, u)).reshape(bb, N_HEADS, HEAD_DIM, HEAD_DIM)
        put(0, _rms(o, head_rows(nrm_ref, 0)) * jax.nn.silu(heads(3 * MIX_W)))

    def mlstm():
        q = heads(4 * MIX_W)
        k = heads(5 * MIX_W) * HEAD_DIM ** -0.5
        v = heads(6 * MIX_W)
        ig_c = col(gates2, 8)
        ig_r = row(gates2_t, 8)
        Fc = col(cums2, 12)
        Fr = row(cums2_t, 12)
        m0 = jnp.stack([mm_o[bi, h:h + 1, 0:1] for bi, h in gh])
        n_row = jnp.stack([mn_o[bi, h:h + 1, :] for bi, h in gh])
        C = mC_o[...].reshape(G, HEAD_DIM, HEAD_DIM)
        logD = jnp.where(causal, Fc - Fr + ig_r, -jnp.inf)
        m02 = m0 * LOG2E
        b = Fc + m02
        mt = jnp.maximum(b, jnp.max(logD, -1, keepdims=True))
        s = _bmm_nt(q, k) * jnp.exp2(logD - mt)
        inter = jnp.exp2(b - mt)
        yield
        num = _bmm(s, v) + inter * _bmm(q, C)
        den = jnp.sum(s, -1, keepdims=True) + inter * jnp.sum(q * n_row, -1, keepdims=True)
        hm = num / jnp.maximum(jnp.abs(den), jnp.exp2(-mt))
        yield
        mt_l = mt[:, lv - 1:lv, :]
        Fl = Fc[:, lv - 1:lv, :]
        wl = jnp.where(live, jnp.exp2(Fl - Fc + ig_c - mt_l), 0.0)
        d0 = jnp.exp2(Fl + m02 - mt_l)
        kw = k * wl
        mC_o[...] = (d0 * C + _bmm_tn(kw, v)).reshape(bb, N_HEADS, HEAD_DIM, HEAD_DIM)
        n_new = d0 * n_row + jnp.sum(kw, axis=1, keepdims=True)
        m_b = jnp.broadcast_to(mt_l * LN2, (G, 1, HEAD_DIM))
        for g, (bi, h) in enumerate(gh):
            mn_o[bi, h:h + 1, :] = n_new[g]
            mm_o[bi, h:h + 1, :] = m_b[g]
        yield
        put(MIX_W, _rms(hm, head_rows(nrm_ref, 1)) * jax.nn.sigmoid(heads(7 * MIX_W)))

    def hgrn():
        hq = jax.nn.silu(heads(8 * MIX_W))
        hi = heads(10 * MIX_W)
        lbh = lb_ref[...]
        Gs_l, kk_l = [], []
        for bi in range(bb):
            fg = lbh + (1.0 - lbh) * jax.nn.sigmoid(pm_ref[bi, :, 9 * MIX_W:10 * MIX_W])
            kk_l.append(1.0 - fg)
            Gs_l.append(_cumsum_rows(tri_bf, jnp.log2(fg)))
        kk = jnp.stack([kk_l[bi][:, h * HEAD_DIM:(h + 1) * HEAD_DIM] for bi, h in gh])
        Gm = jnp.stack([Gs_l[bi][:, h * HEAD_DIM:(h + 1) * HEAD_DIM] for bi, h in gh])
        S = hS_o[...].reshape(G, HEAD_DIM, HEAD_DIM)
        yield
        tiles = lambda a: a.reshape(G * nblk, sub, HEAD_DIM)
        hq_t, kk_t, Gm_t = tiles(hq), tiles(kk), tiles(Gm)
        A = jnp.zeros((G, c, c), F32)
        for dlt in range(sub):
            ks = kk_t if dlt == 0 else pltpu.roll(kk_t, dlt, 1)
            Gs = Gm_t if dlt == 0 else pltpu.roll(Gm_t, dlt, 1)
            d = jnp.sum(hq_t * ks * jnp.exp2(jnp.minimum(Gm_t - Gs, 0.0)), -1, keepdims=True)
            A = jnp.where((s2 == r2 - dlt) & (jnp.bitwise_and(r2, sub - 1) >= dlt), d.reshape(G, c, 1), A)
            if dlt % 2 == 1:
                yield
        if nblk > 1:
            Gref = jnp.concatenate(
                [jnp.broadcast_to(Gm[:, I * sub:I * sub + 1, :], (G, sub, HEAD_DIM)) for I in range(nblk)], axis=1)
            qt = hq * jnp.exp2(Gm - Gref)
            pieces = [jnp.zeros((G, sub, c), F32)]
            for I in range(1, nblk):
                kt = kk * jnp.exp2(jnp.minimum(Gm[:, I * sub:I * sub + 1, :] - Gm, 0.0))
                pieces.append(_bmm_nt(qt[:, I * sub:(I + 1) * sub], kt))
                if I % 2 == 1:
                    yield
            A = jnp.where(s2 < jnp.bitwise_and(r2, -sub), jnp.concatenate(pieces, axis=1), A)
        o = _bmm(hq * jnp.exp2(Gm), S) + _bmm(A, hi)
        Gl = Gm[:, lv - 1:lv, :]
        kw = jnp.where(live, kk * jnp.exp2(Gl - Gm), 0.0)
        dcol = jnp.exp2(jnp.swapaxes(jnp.broadcast_to(Gl, (G, 8, HEAD_DIM)), 1, 2)[:, :, 0:1])
        hS_o[...] = (dcol * S + _bmm_tn(kw, hi)).reshape(bb, N_HEADS, HEAD_DIM, HEAD_DIM)
        yield
        put(2 * MIX_W, _rms(o, head_rows(nrm_ref, 2)) * jax.nn.silu(heads(11 * MIX_W)))

    streams = [gdn(), hgrn(), mlstm()]
    while streams:
        for st in list(streams):
            if next(st, "done") == "done":
                streams.remove(st)


def _mixers(pm3, ps3, gS, gconv, mC, mn, mm, hS, cw, sp, nrm, lb, l, *, bb, c, lv):
    Bt, Lp, _ = pm3.shape
    kern = functools.partial(_mix_kernel, bb=bb, c=c, lv=lv)
    st4 = pl.BlockSpec((bb, N_HEADS, HEAD_DIM, HEAD_DIM), lambda b, j: (b, 0, 0, 0))
    st3 = pl.BlockSpec((bb, N_HEADS, HEAD_DIM), lambda b, j: (b, 0, 0))
    in4 = pl.BlockSpec((None, bb, N_HEADS, HEAD_DIM, HEAD_DIM), lambda b, j: (l, b, 0, 0, 0))
    in3 = pl.BlockSpec((None, bb, N_HEADS, HEAD_DIM), lambda b, j: (l, b, 0, 0))
    full = lambda a: pl.BlockSpec(a.shape, lambda b, j: (0,) * a.ndim)
    return pl.pallas_call(
        kern,
        grid=(Bt // bb, Lp // c),
        in_specs=[pl.BlockSpec((bb, c, N_MIX), lambda b, j: (b, j, 0)),
                  pl.BlockSpec((bb, c, 128), lambda b, j: (b, j, 0)),
                  in4,
                  pl.BlockSpec((None, bb, CONV_W - 1, 3 * MIX_W), lambda b, j: (l, b, 0, 0)),
                  in4, in3, in3, in4,
                  full(cw), full(sp), full(nrm), full(lb)],
        out_specs=[pl.BlockSpec((bb, c, 3 * MIX_W), lambda b, j: (b, j, 0)),
                   st4, st4, st3, st3, st4],
        out_shape=[jax.ShapeDtypeStruct((Bt, Lp, 3 * MIX_W), BF16),
                   jax.ShapeDtypeStruct(gS.shape[1:], F32),
                   jax.ShapeDtypeStruct(mC.shape[1:], F32),
                   jax.ShapeDtypeStruct(mn.shape[1:], F32),
                   jax.ShapeDtypeStruct(mm.shape[1:], F32),
                   jax.ShapeDtypeStruct(hS.shape[1:], F32)],
        scratch_shapes=[pltpu.VMEM((bb, c + 8, 3 * MIX_W), F32)],
        compiler_params=_params(2),
        name="mixers",
    )(pm3, ps3, gS, gconv, mC, mn, mm, hS, cw, sp, nrm, lb)


def _chan_kernel(om_ref, pg_ref, x_ref, st_ref, wbr_ref, wout_ref, lnf_ref, wup_ref, cw_ref, cb_ref,
                 wd_ref, lno_ref, out_ref, ust_ref, *carry, tm, nseq, tiles_per_seq, st_end, final):
    rows = tm // nseq
    acc = None
    for n in range(N_BRANCH):
        br = jnp.dot(om_ref[:, n * MIX_W:(n + 1) * MIX_W], wbr_ref[n], preferred_element_type=F32)
        t = jax.nn.sigmoid(pg_ref[:, n * D_MODEL:(n + 1) * D_MODEL]) * br
        acc = t if acc is None else acc + t
    x1 = x_ref[...] + jnp.dot(acc.astype(BF16), wout_ref[...], preferred_element_type=F32)
    hf = _rms(x1, lnf_ref[...]).astype(BF16)

    if tiles_per_seq > 1:
        prev_ref, = carry

        @pl.when(pl.program_id(0) % tiles_per_seq == 0)
        def _():
            prev_ref[:, 0:2, :] = st_ref[...]
    else:
        prev_ref = st_ref

    row8 = lax.broadcasted_iota(jnp.int32, (1, 8, 1), 1)

    def up(col0):
        return jnp.dot(hf, wup_ref[:, col0:col0 + FFN_COLS], preferred_element_type=F32)

    def conv(col0, u_raw):
        cs = slice(col0, col0 + FFN_COLS)
        w0, w1, w2, cb = cw_ref[0:1, cs], cw_ref[1:2, cs], cw_ref[2:3, cs], cb_ref[:, cs]
        uc = u_raw.reshape(nseq, rows, FFN_COLS)
        r1 = pltpu.roll(uc, 1, 1)
        r2 = pltpu.roll(uc, 2, 1)
        p0 = prev_ref[:, 0:1, cs]
        p1 = prev_ref[:, 1:2, cs]
        s1 = jnp.where(row8 == 0, p1, r1[:, 0:8])
        s2 = jnp.where(row8 == 0, p0, jnp.where(row8 == 1, p1, r2[:, 0:8]))
        y = w0 * s2 + w1 * s1 + w2 * uc[:, 0:8] + cb
        if rows > 8:
            y = jnp.concatenate([y, w0 * r2[:, 8:] + w1 * r1[:, 8:] + w2 * uc[:, 8:] + cb], axis=1)
        ust_ref[:, :, cs] = uc[:, st_end - 2:st_end, :]
        if tiles_per_seq > 1:
            prev_ref[:, 0:2, cs] = uc[:, rows - 2:rows, :]
        return y.reshape(tm, FFN_COLS)

    n_chunks = D_FF // FFN_COLS
    out = x1
    nxt = up(0), up(D_FF)
    for jc in range(n_chunks):
        cur = nxt
        if jc + 1 < n_chunks:
            nxt = up((jc + 1) * FFN_COLS), up(D_FF + (jc + 1) * FFN_COLS)
        ua = conv(jc * FFN_COLS, cur[0])
        ub = conv(D_FF + jc * FFN_COLS, cur[1])
        act = (jax.nn.silu(ua) * ub).astype(BF16)
        out = out + jnp.dot(act, wd_ref[jc * FFN_COLS:(jc + 1) * FFN_COLS, :], preferred_element_type=F32)
    out_ref[...] = _rms(out, lno_ref[...]) if final else out


def _chan(om, pm, x, st, W, ln_final, l, *, rows_per_seq, st_end, tm, final):
    T = x.shape[0]
    nseq = max(1, tm // rows_per_seq)
    tiles_per_seq = max(1, rows_per_seq // tm)
    kern = functools.partial(_chan_kernel, tm=tm, nseq=nseq, tiles_per_seq=tiles_per_seq,
                             st_end=st_end, final=final)
    once = dict(pipeline_mode=pl.Buffered(1))
    st_spec = pl.BlockSpec((nseq, FFN_CONV_W - 1, 2 * D_FF), lambda i: (i // tiles_per_seq, 0, 0))
    return pl.pallas_call(
        kern,
        grid=(T // tm,),
        in_specs=[pl.BlockSpec((tm, 3 * MIX_W), lambda i: (i, 0)),
                  pl.BlockSpec((tm, N_BRANCH * D_MODEL), lambda i: (i, N_MIX // (N_BRANCH * D_MODEL))),
                  pl.BlockSpec((tm, D_MODEL), lambda i: (i, 0)),
                  st_spec,
                  pl.BlockSpec((None, N_BRANCH, MIX_W, D_MODEL), lambda i: (l, 0, 0, 0), **once),
                  pl.BlockSpec((None, D_MODEL, D_MODEL), lambda i: (l, 0, 0), **once),
                  pl.BlockSpec((1, D_MODEL), lambda i: (0, 0)),
                  pl.BlockSpec((None, D_MODEL, 2 * D_FF), lambda i: (l, 0, 0), **once),
                  pl.BlockSpec((FFN_CONV_W, 2 * D_FF), lambda i: (0, 0)),
                  pl.BlockSpec((1, 2 * D_FF), lambda i: (0, 0)),
                  pl.BlockSpec((None, D_FF, D_MODEL), lambda i: (l, 0, 0), **once),
                  pl.BlockSpec((1, D_MODEL), lambda i: (0, 0))],
        out_specs=[pl.BlockSpec((tm, D_MODEL), lambda i: (i, 0)), st_spec],
        out_shape=[jax.ShapeDtypeStruct((T, D_MODEL), F32),
                   jax.ShapeDtypeStruct(st.shape, F32)],
        scratch_shapes=[pltpu.VMEM((1, 8, 2 * D_FF), F32)] if tiles_per_seq > 1 else [],
        compiler_params=_params(1),
        name="chan",
    )(om, pm, x, st, W["w_br"], W["w_out"], W["ln_ffn"][l][None], W["w_up"], W["fcw"][l], W["fcb"][l][None],
      W["w_down"], ln_final[None])


def _prep_weights(ln_mix, w_in, gdn_conv_w, gdn_A_log, gdn_dt_bias, gdn_norm, m_ibias, m_fbias,
                  m_norm, lb_all, hgrn_norm, w_br, w_out, ln_ffn, w_up, ffn_conv_w, ffn_conv_b, w_down):
    depth = w_in.shape[0]
    w_main = jnp.concatenate([w_in[:, :, 0:2048], w_in[:, :, 2056:4104], w_in[:, :, 4112:]], axis=2).astype(BF16)
    w_small = jnp.concatenate([w_in[:, :, 2048:2056], w_in[:, :, 4104:4112],
                               jnp.zeros((depth, D_MODEL, 128 - 4 * N_HEADS), F32)], axis=2).astype(BF16)
    z4 = jnp.zeros((depth, N_HEADS), F32)
    pad = jnp.zeros((depth, 128 - 4 * N_HEADS), F32)
    sp = jnp.zeros((depth, 8, 128), F32)
    sp = sp.at[:, 0].set(jnp.concatenate([z4, gdn_A_log, z4, z4, pad], axis=1))
    sp = sp.at[:, 1].set(jnp.concatenate([z4, gdn_dt_bias, m_ibias, m_fbias, pad], axis=1))
    return dict(
        ln_mix=ln_mix, w_main=w_main, w_small=w_small, cw=gdn_conv_w, sp=sp,
        nrm=jnp.stack([gdn_norm, m_norm, hgrn_norm], axis=1), lb=lb_all,
        w_br=w_br.astype(BF16), w_out=w_out.astype(BF16), ln_ffn=ln_ffn,
        w_up=w_up.astype(BF16), fcw=ffn_conv_w, fcb=ffn_conv_b, w_down=w_down.astype(BF16))


def _trunk(x3, states, W, ln_final, *, lv, c, bb, sample):
    Bt, Lp, _ = x3.shape
    T = Bt * Lp
    x = x3.reshape(T, D_MODEL)
    gdn_S, gdn_conv, m_C, m_n, m_m, h_S, ffn_conv = states
    depth = W["w_main"].shape[0]
    mm_in = jnp.broadcast_to(m_m[..., None], m_m.shape + (HEAD_DIM,))
    new = [[] for _ in range(7)]
    for l in range(depth):
        pm, ps = _in_proj(x, W["ln_mix"][l][None], W["w_main"], W["w_small"], l, tm=min(T, 1024))
        pm3 = pm.reshape(Bt, Lp, N_MAIN)
        om, gS, mC, mn, mm, hS = _mixers(
            pm3, ps.reshape(Bt, Lp, 128), gdn_S, gdn_conv, m_C, m_n, mm_in, h_S,
            W["cw"][l], W["sp"][l], W["nrm"][l], W["lb"][l][None], l, bb=bb, c=c, lv=lv)
        tm = 128 if sample else 256
        x, ust = _chan(om.reshape(T, 3 * MIX_W), pm, x, ffn_conv[l], W, ln_final, l, rows_per_seq=Lp,
                       st_end=lv if sample else tm, tm=tm, final=l == depth - 1)
        end = lv if sample else Lp
        outs = (gS, pm3[:, end - (CONV_W - 1):end, 0:3 * MIX_W], mC, mn, mm[..., 0], hS, ust)
        for acc, o in zip(new, outs):
            acc.append(o)
    return x.reshape(Bt, Lp, D_MODEL), tuple(jnp.stack(s, axis=0) for s in new)


def kernel(x_prompt, x_sample, state_gdn_S, state_gdn_conv, state_mlstm_C, state_mlstm_n, state_mlstm_m, state_hgrn_S, state_ffn_conv, ln_mix, w_in, gdn_conv_w, gdn_A_log, gdn_dt_bias, gdn_norm, m_ibias, m_fbias, m_norm, hgrn_lb, hgrn_norm, w_br, w_out, ln_ffn, w_up, ffn_conv_w, ffn_conv_b, w_down, ln_final):
    depth = w_in.shape[0]
    lb_all = jnp.cumsum(jax.nn.softmax(hgrn_lb.astype(F32), axis=0), axis=0)
    lb_all = lb_all - lb_all[0]
    layers = _prep_weights(ln_mix, w_in, gdn_conv_w, gdn_A_log, gdn_dt_bias, gdn_norm, m_ibias,
                           m_fbias, m_norm, lb_all, hgrn_norm, w_br, w_out, ln_ffn, w_up, ffn_conv_w,
                           ffn_conv_b, w_down)

    B, L, _ = x_prompt.shape
    H, Dh = N_HEADS, HEAD_DIM
    zeros = (jnp.zeros((depth, B, H, Dh, Dh), F32), jnp.zeros((depth, B, CONV_W - 1, 3 * MIX_W), F32),
             jnp.zeros((depth, B, H, Dh, Dh), F32), jnp.zeros((depth, B, H, Dh), F32),
             jnp.zeros((depth, B, H), F32), jnp.zeros((depth, B, H, Dh, Dh), F32),
             jnp.zeros((depth, B, FFN_CONV_W - 1, 2 * D_FF), F32))
    y_p, st_p = _trunk(x_prompt, zeros, layers, ln_final, lv=CHUNK, c=CHUNK, bb=2, sample=False)

    Bs, Ls, _ = x_sample.shape
    xs = jnp.pad(x_sample, ((0, 0), (0, SAMPLE_PAD - Ls), (0, 0)))
    st_in = (state_gdn_S, state_gdn_conv, state_mlstm_C, state_mlstm_n, state_mlstm_m, state_hgrn_S,
             state_ffn_conv)
    y_s, st_s = _trunk(xs, st_in, layers, ln_final, lv=Ls, c=SAMPLE_PAD, bb=8, sample=True)
    return (y_p, y_s[:, :Ls]) + st_p + st_s
```

```python
import functools
import math

import jax
import jax.numpy as jnp
from jax import lax
from jax.experimental import pallas as pl
from jax.experimental.pallas import tpu as pltpu

D_MODEL = 1024
N_HEADS = 4
HEAD_DIM = 128
MIX_W = N_HEADS * HEAD_DIM
N_BRANCH = 3
CONV_W = 4
FFN_CONV_W = 3
D_FF = 2816
CHUNK = 64
EPS = 1e-6
N_MIX = 6144
SAMPLE_PAD = 8
SUB = 8
FFN_COLS = 2816
LOG2E = 1.4426950408889634
LN2 = 0.6931471805599453

F32 = jnp.float32
BF16 = jnp.bfloat16
VMEM_LIMIT = 56 * 1024 * 1024


def _bmm(a, b):
    return jnp.einsum('gtd,gde->gte', a.astype(BF16), b.astype(BF16), preferred_element_type=F32)


def _bmm_nt(a, b):
    return jnp.einsum('gtd,gsd->gts', a.astype(BF16), b.astype(BF16), preferred_element_type=F32)


def _bmm_tn(a, b):
    return jnp.einsum('gsd,gse->gde', a.astype(BF16), b.astype(BF16), preferred_element_type=F32)


def _split2(a):
    hi = a.astype(BF16)
    return hi, (a - hi.astype(F32)).astype(BF16)


def _bmm_split(a, b):
    a_hi, a_lo = a
    b_hi, b_lo = b
    n = a_hi.shape[1]
    p = jnp.einsum('gtk,gks->gts', jnp.concatenate([a_hi, a_lo], axis=1), b_hi, preferred_element_type=F32)
    return p[:, :n] + p[:, n:] + jnp.einsum('gtk,gks->gts', a_hi, b_lo, preferred_element_type=F32)


def _cumsum_rows(tri_bf, x):
    n = x.shape[1]
    x1 = x.astype(BF16)
    r1 = x - x1.astype(F32)
    x2 = r1.astype(BF16)
    x3 = (r1 - x2.astype(F32)).astype(BF16)
    p = jnp.dot(tri_bf, jnp.concatenate([x1, x2, x3], axis=1), preferred_element_type=F32)
    return p[:, 0:n] + p[:, n:2 * n] + p[:, 2 * n:3 * n]


def _rms(x, g):
    return x * lax.rsqrt(jnp.mean(x * x, -1, keepdims=True) + EPS) * g


def _params(n_axes):
    return pltpu.CompilerParams(dimension_semantics=("arbitrary",) * n_axes,
                                vmem_limit_bytes=VMEM_LIMIT)


def _in_kernel(x_ref, ln_ref, wm_ref, ws_ref, pm_ref, ps_ref, h_ref):
    @pl.when(pl.program_id(1) == 0)
    def _():
        hb = _rms(x_ref[...], ln_ref[...]).astype(BF16)
        h_ref[...] = hb
        ps_ref[...] = jnp.dot(hb, ws_ref[...], preferred_element_type=F32)

    pm_ref[...] = jnp.dot(h_ref[...], wm_ref[...], preferred_element_type=F32)


def _in_proj(x, ln, w_main, w_small, l, tm, tn=1024):
    T = x.shape[0]
    return pl.pallas_call(
        _in_kernel,
        grid=(T // tm, N_MIX // tn),
        in_specs=[pl.BlockSpec((tm, D_MODEL), lambda i, j: (i, 0)),
                  pl.BlockSpec((1, D_MODEL), lambda i, j: (0, 0)),
                  pl.BlockSpec((None, D_MODEL, tn), lambda i, j: (l, 0, j)),
                  pl.BlockSpec((None, D_MODEL, 128), lambda i, j: (l, 0, 0))],
        out_specs=[pl.BlockSpec((tm, tn), lambda i, j: (i, j)),
                   pl.BlockSpec((tm, 128), lambda i, j: (i, 0))],
        out_shape=[jax.ShapeDtypeStruct((T, N_MIX), F32),
                   jax.ShapeDtypeStruct((T, 128), F32)],
        scratch_shapes=[pltpu.VMEM((tm, D_MODEL), BF16)],
        compiler_params=_params(2),
        name="in_proj",
    )(x, ln, w_main, w_small)


def _mix_kernel(pm_ref, ps_ref, gS_ref, gconv_ref, mC_ref, mn_ref, mm_ref, hS_ref,
                cw_ref, sp_ref, nrm_ref, lb_ref,
                om_ref, gS_o, mC_o, mn_o, mm_o, hS_o,
                xpad_ref, *, bb, c, lv):
    G = bb * N_HEADS
    gh = [(bi, h) for bi in range(bb) for h in range(N_HEADS)]

    @pl.when(pl.program_id(1) == 0)
    def _():
        gS_o[...] = gS_ref[...]
        mC_o[...] = mC_ref[...]
        mn_o[...] = mn_ref[...]
        mm_o[...] = mm_ref[...]
        hS_o[...] = hS_ref[...]
        xpad_ref[:, 5:8, :] = gconv_ref[...]

    r2 = lax.broadcasted_iota(jnp.int32, (c, c), 0)
    s2 = lax.broadcasted_iota(jnp.int32, (c, c), 1)
    causal = r2 >= s2
    strict = r2 > s2
    tri_bf = causal.astype(BF16)
    eye = (r2 == s2).astype(F32)
    rowc = lax.broadcasted_iota(jnp.int32, (c, 1), 0)
    live = rowc < lv
    lane = lax.broadcasted_iota(jnp.int32, (c, 128), 1)
    sub = min(SUB, c)
    nblk = c // sub
    n_sq = int(math.log2(c)) - 1

    def heads(col0):
        return jnp.stack([pm_ref[bi, :, col0 + h * HEAD_DIM:col0 + (h + 1) * HEAD_DIM] for bi, h in gh])

    def head_rows(ref, r):
        return jnp.stack([ref[r:r + 1, h * HEAD_DIM:(h + 1) * HEAD_DIM] for _, h in gh])

    def put(col0, val):
        for g, (bi, h) in enumerate(gh):
            om_ref[bi, :, col0 + h * HEAD_DIM:col0 + (h + 1) * HEAD_DIM] = val[g].astype(om_ref.dtype)

    alog = sp_ref[0:1, :]
    bias = sp_ref[1:2, :]
    gates, gates2, cums2, gates2_t, cums2_t, convs = [], [], [], [], [], []
    for bi in range(bb):
        ps = ps_ref[bi]
        z = ps + bias
        gt = jnp.where(lane < 4, jax.nn.sigmoid(ps),
                       jnp.where(lane < 8, -jnp.exp(alog) * jax.nn.softplus(z),
                                 jnp.where(lane < 12, z, jax.nn.log_sigmoid(z))))
        gt2 = gt * LOG2E
        cm2 = _cumsum_rows(tri_bf, gt) * LOG2E
        gates.append(gt)
        gates2.append(gt2)
        cums2.append(cm2)
        gates2_t.append(gt2.T)
        cums2_t.append(cm2.T)
        xpad_ref[bi, 8:8 + c, :] = pm_ref[bi, :, 0:3 * MIX_W]
        acc = cw_ref[0:1, :] * xpad_ref[bi, 5:5 + c, :]
        for j in range(1, CONV_W):
            acc = acc + cw_ref[j:j + 1, :] * xpad_ref[bi, 5 + j:5 + j + c, :]
        convs.append(jax.nn.silu(acc))
        xpad_ref[bi, 5:8, :] = xpad_ref[bi, 5 + c:8 + c, :]

    col = lambda arrs, j: jnp.stack([arrs[bi][:, j + h:j + h + 1] for bi, h in gh])
    row = lambda arrs, j: jnp.stack([arrs[bi][j + h:j + h + 1, :] for bi, h in gh])
    part = lambda p: jnp.stack([convs[bi][:, p * MIX_W + h * HEAD_DIM:p * MIX_W + (h + 1) * HEAD_DIM]
                                for bi, h in gh])


    def gdn():
        q = part(0)
        k = part(1)
        v = part(2)
        q = q * lax.rsqrt(jnp.sum(q * q, -1, keepdims=True) + EPS) * HEAD_DIM ** -0.5
        k = k * lax.rsqrt(jnp.sum(k * k, -1, keepdims=True) + EPS)
        beta = col(gates, 0)
        Gc = col(cums2, 4)
        Gr = row(cums2_t, 4)
        decay = jnp.exp2(jnp.where(causal, Gc - Gr, -jnp.inf))
        S = gS_o[...].reshape(G, HEAD_DIM, HEAD_DIM)
        qk2 = jnp.concatenate([q, k], axis=1)
        P = _bmm_nt(qk2, k)
        R = _bmm(qk2, S)
        QK, KK = P[:, :c], P[:, c:]
        QS, KS = R[:, :c], R[:, c:]
        eG = jnp.exp2(Gc)
        rhs = beta * (v - eG * KS)
        Mp = -jnp.where(strict, beta * decay * KK, 0.0)
        Tinv = eye + Mp
        Ms = _split2(Mp)
        yield
        for _ in range(n_sq):
            Mp = _bmm_split(Ms, Ms)
            Ms = _split2(Mp)
            Tinv = Tinv + _bmm_split(_split2(Tinv), Ms)
            yield
        u = _bmm_split(_split2(Tinv), _split2(rhs))
        yield
        o = eG * QS + _bmm(QK * decay, u)
        Gl = Gc[:, lv - 1:lv, :]
        wl = jnp.where(live, jnp.exp2(Gl - Gc), 0.0)
        gS_o[...] = (jnp.exp2(Gl) * S + _bmm_tn(k * wl, u)).reshape(bb, N_HEADS, HEAD_DIM, HEAD_DIM)
        put(0, _rms(o, head_rows(nrm_ref, 0)) * jax.nn.silu(heads(3 * MIX_W)))

    def mlstm():
        q = heads(4 * MIX_W)
        k = heads(5 * MIX_W) * HEAD_DIM ** -0.5
        v = heads(6 * MIX_W)
        ig_c = col(gates2, 8)
        ig_r = row(gates2_t, 8)
        Fc = col(cums2, 12)
        Fr = row(cums2_t, 12)
        m0 = jnp.stack([mm_o[bi, h:h + 1, 0:1] for bi, h in gh])
        n_row = jnp.stack([mn_o[bi, h:h + 1, :] for bi, h in gh])
        C = mC_o[...].reshape(G, HEAD_DIM, HEAD_DIM)
        logD = jnp.where(causal, Fc - Fr + ig_r, -jnp.inf)
        m02 = m0 * LOG2E
        b = Fc + m02
        mt = jnp.maximum(b, jnp.max(logD, -1, keepdims=True))
        s = _bmm_nt(q, k) * jnp.exp2(logD - mt)
        inter = jnp.exp2(b - mt)
        yield
        num = _bmm(s, v) + inter * _bmm(q, C)
        den = jnp.sum(s, -1, keepdims=True) + inter * jnp.sum(q * n_row, -1, keepdims=True)
        hm = num / jnp.maximum(jnp.abs(den), jnp.exp2(-mt))
        yield
        mt_l = mt[:, lv - 1:lv, :]
        Fl = Fc[:, lv - 1:lv, :]
        wl = jnp.where(live, jnp.exp2(Fl - Fc + ig_c - mt_l), 0.0)
        d0 = jnp.exp2(Fl + m02 - mt_l)
        kw = k * wl
        mC_o[...] = (d0 * C + _bmm_tn(kw, v)).reshape(bb, N_HEADS, HEAD_DIM, HEAD_DIM)
        n_new = d0 * n_row + jnp.sum(kw, axis=1, keepdims=True)
        m_b = jnp.broadcast_to(mt_l * LN2, (G, 1, HEAD_DIM))
        for g, (bi, h) in enumerate(gh):
            mn_o[bi, h:h + 1, :] = n_new[g]
            mm_o[bi, h:h + 1, :] = m_b[g]
        yield
        put(MIX_W, _rms(hm, head_rows(nrm_ref, 1)) * jax.nn.sigmoid(heads(7 * MIX_W)))

    def hgrn():
        hq = jax.nn.silu(heads(8 * MIX_W))
        hi = heads(10 * MIX_W)
        lbh = lb_ref[...]
        Gs_l, kk_l = [], []
        for bi in range(bb):
            fg = lbh + (1.0 - lbh) * jax.nn.sigmoid(pm_ref[bi, :, 9 * MIX_W:10 * MIX_W])
            kk_l.append(1.0 - fg)
            Gs_l.append(_cumsum_rows(tri_bf, jnp.log2(fg)))
        kk = jnp.stack([kk_l[bi][:, h * HEAD_DIM:(h + 1) * HEAD_DIM] for bi, h in gh])
        Gm = jnp.stack([Gs_l[bi][:, h * HEAD_DIM:(h + 1) * HEAD_DIM] for bi, h in gh])
        S = hS_o[...].reshape(G, HEAD_DIM, HEAD_DIM)
        yield
        tiles = lambda a: a.reshape(G * nblk, sub, HEAD_DIM)
        hq_t, kk_t, Gm_t = tiles(hq), tiles(kk), tiles(Gm)
        A = jnp.zeros((G, c, c), F32)
        for dlt in range(sub):
            ks = kk_t if dlt == 0 else pltpu.roll(kk_t, dlt, 1)
            Gs = Gm_t if dlt == 0 else pltpu.roll(Gm_t, dlt, 1)
            d = jnp.sum(hq_t * ks * jnp.exp2(jnp.minimum(Gm_t - Gs, 0.0)), -1, keepdims=True)
            A = jnp.where((s2 == r2 - dlt) & (jnp.bitwise_and(r2, sub - 1) >= dlt), d.reshape(G, c, 1), A)
            if dlt % 2 == 1:
                yield
        if nblk > 1:
            Gref = jnp.concatenate(
                [jnp.broadcast_to(Gm[:, I * sub:I * sub + 1, :], (G, sub, HEAD_DIM)) for I in range(nblk)], axis=1)
            qt = hq * jnp.exp2(Gm - Gref)
            pieces = [jnp.zeros((G, sub, c), F32)]
            for I in range(1, nblk):
                kt = kk * jnp.exp2(jnp.minimum(Gm[:, I * sub:I * sub + 1, :] - Gm, 0.0))
                pieces.append(_bmm_nt(qt[:, I * sub:(I + 1) * sub], kt))
                if I % 2 == 1:
                    yield
            A = jnp.where(s2 < jnp.bitwise_and(r2, -sub), jnp.concatenate(pieces, axis=1), A)
        o = _bmm(hq * jnp.exp2(Gm), S) + _bmm(A, hi)
        Gl = Gm[:, lv - 1:lv, :]
        kw = jnp.where(live, kk * jnp.exp2(Gl - Gm), 0.0)
        dcol = jnp.exp2(jnp.swapaxes(jnp.broadcast_to(Gl, (G, 8, HEAD_DIM)), 1, 2)[:, :, 0:1])
        hS_o[...] = (dcol * S + _bmm_tn(kw, hi)).reshape(bb, N_HEADS, HEAD_DIM, HEAD_DIM)
        yield
        put(2 * MIX_W, _rms(o, head_rows(nrm_ref, 2)) * jax.nn.silu(heads(11 * MIX_W)))

    streams = [gdn(), hgrn(), mlstm()]
    while streams:
        for st in list(streams):
            if next(st, "done") == "done":
                streams.remove(st)


def _mixers(pm3, ps3, gS, gconv, mC, mn, mm, hS, cw, sp, nrm, lb, l, *, bb, c, lv):
    Bt, Lp, _ = pm3.shape
    kern = functools.partial(_mix_kernel, bb=bb, c=c, lv=lv)
    st4 = pl.BlockSpec((bb, N_HEADS, HEAD_DIM, HEAD_DIM), lambda b, j: (b, 0, 0, 0))
    st3 = pl.BlockSpec((bb, N_HEADS, HEAD_DIM), lambda b, j: (b, 0, 0))
    in4 = pl.BlockSpec((None, bb, N_HEADS, HEAD_DIM, HEAD_DIM), lambda b, j: (l, b, 0, 0, 0))
    in3 = pl.BlockSpec((None, bb, N_HEADS, HEAD_DIM), lambda b, j: (l, b, 0, 0))
    full = lambda a: pl.BlockSpec(a.shape, lambda b, j: (0,) * a.ndim)
    return pl.pallas_call(
        kern,
        grid=(Bt // bb, Lp // c),
        in_specs=[pl.BlockSpec((bb, c, N_MIX), lambda b, j: (b, j, 0)),
                  pl.BlockSpec((bb, c, 128), lambda b, j: (b, j, 0)),
                  in4,
                  pl.BlockSpec((None, bb, CONV_W - 1, 3 * MIX_W), lambda b, j: (l, b, 0, 0)),
                  in4, in3, in3, in4,
                  full(cw), full(sp), full(nrm), full(lb)],
        out_specs=[pl.BlockSpec((bb, c, 3 * MIX_W), lambda b, j: (b, j, 0)),
                   st4, st4, st3, st3, st4],
        out_shape=[jax.ShapeDtypeStruct((Bt, Lp, 3 * MIX_W), BF16),
                   jax.ShapeDtypeStruct(gS.shape[1:], F32),
                   jax.ShapeDtypeStruct(mC.shape[1:], F32),
                   jax.ShapeDtypeStruct(mn.shape[1:], F32),
                   jax.ShapeDtypeStruct(mm.shape[1:], F32),
                   jax.ShapeDtypeStruct(hS.shape[1:], F32)],
        scratch_shapes=[pltpu.VMEM((bb, c + 8, 3 * MIX_W), F32)],
        compiler_params=_params(2),
        name="mixers",
    )(pm3, ps3, gS, gconv, mC, mn, mm, hS, cw, sp, nrm, lb)


def _chan_kernel(om_ref, x_ref, st_ref, lnm_ref, wg_ref, wbr_ref, wout_ref, lnf_ref, wup_ref, cw_ref, cb_ref,
                 wd_ref, lno_ref, out_ref, ust_ref, *carry, tm, nseq, tiles_per_seq, st_end, final):
    rows = tm // nseq
    x = x_ref[...]
    hm = _rms(x, lnm_ref[...]).astype(BF16)
    acc = None
    for n in range(N_BRANCH):
        br = jnp.dot(om_ref[:, n * MIX_W:(n + 1) * MIX_W], wbr_ref[n], preferred_element_type=F32)
        pg = jnp.dot(hm, wg_ref[:, n * D_MODEL:(n + 1) * D_MODEL], preferred_element_type=F32)
        t = jax.nn.sigmoid(pg) * br
        acc = t if acc is None else acc + t
    x1 = x + jnp.dot(acc.astype(BF16), wout_ref[...], preferred_element_type=F32)
    hf = _rms(x1, lnf_ref[...]).astype(BF16)

    if tiles_per_seq > 1:
        prev_ref, = carry

        @pl.when(pl.program_id(0) % tiles_per_seq == 0)
        def _():
            prev_ref[:, 0:2, :] = st_ref[...]
    else:
        prev_ref = st_ref

    row8 = lax.broadcasted_iota(jnp.int32, (1, 8, 1), 1)

    def up(col0):
        return jnp.dot(hf, wup_ref[:, col0:col0 + FFN_COLS], preferred_element_type=F32)

    def conv(col0, u_raw):
        cs = slice(col0, col0 + FFN_COLS)
        w0, w1, w2, cb = cw_ref[0:1, cs], cw_ref[1:2, cs], cw_ref[2:3, cs], cb_ref[:, cs]
        uc = u_raw.reshape(nseq, rows, FFN_COLS)
        r1 = pltpu.roll(uc, 1, 1)
        r2 = pltpu.roll(uc, 2, 1)
        p0 = prev_ref[:, 0:1, cs]
        p1 = prev_ref[:, 1:2, cs]
        s1 = jnp.where(row8 == 0, p1, r1[:, 0:8])
        s2 = jnp.where(row8 == 0, p0, jnp.where(row8 == 1, p1, r2[:, 0:8]))
        y = w0 * s2 + w1 * s1 + w2 * uc[:, 0:8] + cb
        if rows > 8:
            y = jnp.concatenate([y, w0 * r2[:, 8:] + w1 * r1[:, 8:] + w2 * uc[:, 8:] + cb], axis=1)
        ust_ref[:, :, cs] = uc[:, st_end - 2:st_end, :]
        if tiles_per_seq > 1:
            prev_ref[:, 0:2, cs] = uc[:, rows - 2:rows, :]
        return y.reshape(tm, FFN_COLS)

    n_chunks = D_FF // FFN_COLS
    out = x1
    nxt = up(0), up(D_FF)
    for jc in range(n_chunks):
        cur = nxt
        if jc + 1 < n_chunks:
            nxt = up((jc + 1) * FFN_COLS), up(D_FF + (jc + 1) * FFN_COLS)
        ua = conv(jc * FFN_COLS, cur[0])
        ub = conv(D_FF + jc * FFN_COLS, cur[1])
        act = (jax.nn.silu(ua) * ub).astype(BF16)
        out = out + jnp.dot(act, wd_ref[jc * FFN_COLS:(jc + 1) * FFN_COLS, :], preferred_element_type=F32)
    out_ref[...] = _rms(out, lno_ref[...]) if final else out


def _chan(om, x, st, W, ln_final, l, *, rows_per_seq, st_end, tm, final):
    T = x.shape[0]
    nseq = max(1, tm // rows_per_seq)
    tiles_per_seq = max(1, rows_per_seq // tm)
    kern = functools.partial(_chan_kernel, tm=tm, nseq=nseq, tiles_per_seq=tiles_per_seq,
                             st_end=st_end, final=final)
    once = dict(pipeline_mode=pl.Buffered(1))
    st_spec = pl.BlockSpec((nseq, FFN_CONV_W - 1, 2 * D_FF), lambda i: (i // tiles_per_seq, 0, 0))
    return pl.pallas_call(
        kern,
        grid=(T // tm,),
        in_specs=[pl.BlockSpec((tm, 3 * MIX_W), lambda i: (i, 0)),
                  pl.BlockSpec((tm, D_MODEL), lambda i: (i, 0)),
                  st_spec,
                  pl.BlockSpec((1, D_MODEL), lambda i: (0, 0)),
                  pl.BlockSpec((None, D_MODEL, N_BRANCH * D_MODEL),
                               lambda i: (l, 0, N_MIX // (N_BRANCH * D_MODEL)), **once),
                  pl.BlockSpec((None, N_BRANCH, MIX_W, D_MODEL), lambda i: (l, 0, 0, 0), **once),
                  pl.BlockSpec((None, D_MODEL, D_MODEL), lambda i: (l, 0, 0), **once),
                  pl.BlockSpec((1, D_MODEL), lambda i: (0, 0)),
                  pl.BlockSpec((None, D_MODEL, 2 * D_FF), lambda i: (l, 0, 0), **once),
                  pl.BlockSpec((FFN_CONV_W, 2 * D_FF), lambda i: (0, 0)),
                  pl.BlockSpec((1, 2 * D_FF), lambda i: (0, 0)),
                  pl.BlockSpec((None, D_FF, D_MODEL), lambda i: (l, 0, 0), **once),
                  pl.BlockSpec((1, D_MODEL), lambda i: (0, 0))],
        out_specs=[pl.BlockSpec((tm, D_MODEL), lambda i: (i, 0)), st_spec],
        out_shape=[jax.ShapeDtypeStruct((T, D_MODEL), F32),
                   jax.ShapeDtypeStruct(st.shape, F32)],
        scratch_shapes=[pltpu.VMEM((1, 8, 2 * D_FF), F32)] if tiles_per_seq > 1 else [],
        compiler_params=_params(1),
        name="chan",
    )(om, x, st, W["ln_mix"][l][None], W["w_main"], W["w_br"], W["w_out"], W["ln_ffn"][l][None], W["w_up"],
      W["fcw"][l], W["fcb"][l][None], W["w_down"], ln_final[None])


def _prep_weights(ln_mix, w_in, gdn_conv_w, gdn_A_log, gdn_dt_bias, gdn_norm, m_ibias, m_fbias,
                  m_norm, lb_all, hgrn_norm, w_br, w_out, ln_ffn, w_up, ffn_conv_w, ffn_conv_b, w_down):
    depth = w_in.shape[0]
    w_main = jnp.concatenate([w_in[:, :, 0:2048], w_in[:, :, 2056:4104], w_in[:, :, 4112:]], axis=2).astype(BF16)
    w_small = jnp.concatenate([w_in[:, :, 2048:2056], w_in[:, :, 4104:4112],
                               jnp.zeros((depth, D_MODEL, 128 - 4 * N_HEADS), F32)], axis=2).astype(BF16)
    z4 = jnp.zeros((depth, N_HEADS), F32)
    pad = jnp.zeros((depth, 128 - 4 * N_HEADS), F32)
    sp = jnp.zeros((depth, 8, 128), F32)
    sp = sp.at[:, 0].set(jnp.concatenate([z4, gdn_A_log, z4, z4, pad], axis=1))
    sp = sp.at[:, 1].set(jnp.concatenate([z4, gdn_dt_bias, m_ibias, m_fbias, pad], axis=1))
    return dict(
        ln_mix=ln_mix, w_main=w_main, w_small=w_small, cw=gdn_conv_w, sp=sp,
        nrm=jnp.stack([gdn_norm, m_norm, hgrn_norm], axis=1), lb=lb_all,
        w_br=w_br.astype(BF16), w_out=w_out.astype(BF16), ln_ffn=ln_ffn,
        w_up=w_up.astype(BF16), fcw=ffn_conv_w, fcb=ffn_conv_b, w_down=w_down.astype(BF16))


def _trunk(x3, states, W, ln_final, *, lv, c, bb, sample):
    Bt, Lp, _ = x3.shape
    T = Bt * Lp
    x = x3.reshape(T, D_MODEL)
    gdn_S, gdn_conv, m_C, m_n, m_m, h_S, ffn_conv = states
    depth = W["w_main"].shape[0]
    mm_in = jnp.broadcast_to(m_m[..., None], m_m.shape + (HEAD_DIM,))
    new = [[] for _ in range(7)]
    for l in range(depth):
        pm, ps = _in_proj(x, W["ln_mix"][l][None], W["w_main"], W["w_small"], l, tm=min(T, 1024))
        pm3 = pm.reshape(Bt, Lp, N_MIX)
        om, gS, mC, mn, mm, hS = _mixers(
            pm3, ps.reshape(Bt, Lp, 128), gdn_S, gdn_conv, m_C, m_n, mm_in, h_S,
            W["cw"][l], W["sp"][l], W["nrm"][l], W["lb"][l][None], l, bb=bb, c=c, lv=lv)
        tm = 128 if sample else 256
        x, ust = _chan(om.reshape(T, 3 * MIX_W), x, ffn_conv[l], W, ln_final, l, rows_per_seq=Lp,
                       st_end=lv if sample else tm, tm=tm, final=l == depth - 1)
        end = lv if sample else Lp
        outs = (gS, pm3[:, end - (CONV_W - 1):end, 0:3 * MIX_W], mC, mn, mm[..., 0], hS, ust)
        for acc, o in zip(new, outs):
            acc.append(o)
    return x.reshape(Bt, Lp, D_MODEL), tuple(jnp.stack(s, axis=0) for s in new)


def kernel(x_prompt, x_sample, state_gdn_S, state_gdn_conv, state_mlstm_C, state_mlstm_n, state_mlstm_m, state_hgrn_S, state_ffn_conv, ln_mix, w_in, gdn_conv_w, gdn_A_log, gdn_dt_bias, gdn_norm, m_ibias, m_fbias, m_norm, hgrn_lb, hgrn_norm, w_br, w_out, ln_ffn, w_up, ffn_conv_w, ffn_conv_b, w_down, ln_final):
    depth = w_in.shape[0]
    lb_all = jnp.cumsum(jax.nn.softmax(hgrn_lb.astype(F32), axis=0), axis=0)
    lb_all = lb_all - lb_all[0]
    layers = _prep_weights(ln_mix, w_in, gdn_conv_w, gdn_A_log, gdn_dt_bias, gdn_norm, m_ibias,
                           m_fbias, m_norm, lb_all, hgrn_norm, w_br, w_out, ln_ffn, w_up, ffn_conv_w,
                           ffn_conv_b, w_down)

    B, L, _ = x_prompt.shape
    H, Dh = N_HEADS, HEAD_DIM
    zeros = (jnp.zeros((depth, B, H, Dh, Dh), F32), jnp.zeros((depth, B, CONV_W - 1, 3 * MIX_W), F32),
             jnp.zeros((depth, B, H, Dh, Dh), F32), jnp.zeros((depth, B, H, Dh), F32),
             jnp.zeros((depth, B, H), F32), jnp.zeros((depth, B, H, Dh, Dh), F32),
             jnp.zeros((depth, B, FFN_CONV_W - 1, 2 * D_FF), F32))
    y_p, st_p = _trunk(x_prompt, zeros, layers, ln_final, lv=CHUNK, c=CHUNK, bb=2, sample=False)

    Bs, Ls, _ = x_sample.shape
    xs = jnp.pad(x_sample, ((0, 0), (0, SAMPLE_PAD - Ls), (0, 0)))
    st_in = (state_gdn_S, state_gdn_conv, state_mlstm_C, state_mlstm_n, state_mlstm_m, state_hgrn_S,
             state_ffn_conv)
    y_s, st_s = _trunk(xs, st_in, layers, ln_final, lv=Ls, c=SAMPLE_PAD, bb=8, sample=True)
    return (y_p, y_s[:, :Ls]) + st_p + st_s
```

```python
import functools
import math

import jax
import jax.numpy as jnp
from jax import lax
from jax.experimental import pallas as pl
from jax.experimental.pallas import tpu as pltpu

D_MODEL = 1024
N_HEADS = 4
HEAD_DIM = 128
MIX_W = N_HEADS * HEAD_DIM
N_BRANCH = 3
CONV_W = 4
FFN_CONV_W = 3
D_FF = 2816
CHUNK = 64
EPS = 1e-6
N_MIX = 6144
SAMPLE_PAD = 8
SUB = 8
FFN_COLS = 2816
LOG2E = 1.4426950408889634
LN2 = 0.6931471805599453

F32 = jnp.float32
BF16 = jnp.bfloat16
VMEM_LIMIT = 56 * 1024 * 1024


def _bmm(a, b):
    return jnp.einsum('gtd,gde->gte', a.astype(BF16), b.astype(BF16), preferred_element_type=F32)


def _bmm_nt(a, b):
    return jnp.einsum('gtd,gsd->gts', a.astype(BF16), b.astype(BF16), preferred_element_type=F32)


def _bmm_tn(a, b):
    return jnp.einsum('gsd,gse->gde', a.astype(BF16), b.astype(BF16), preferred_element_type=F32)


def _split2(a):
    hi = a.astype(BF16)
    return hi, (a - hi.astype(F32)).astype(BF16)


def _bmm_split(a, b):
    a_hi, a_lo = a
    b_hi, b_lo = b
    n = a_hi.shape[1]
    p = jnp.einsum('gtk,gks->gts', jnp.concatenate([a_hi, a_lo], axis=1), b_hi, preferred_element_type=F32)
    return p[:, :n] + p[:, n:] + jnp.einsum('gtk,gks->gts', a_hi, b_lo, preferred_element_type=F32)


def _cumsum_rows(tri_bf, x):
    n = x.shape[1]
    x1 = x.astype(BF16)
    r1 = x - x1.astype(F32)
    x2 = r1.astype(BF16)
    x3 = (r1 - x2.astype(F32)).astype(BF16)
    p = jnp.dot(tri_bf, jnp.concatenate([x1, x2, x3], axis=1), preferred_element_type=F32)
    return p[:, 0:n] + p[:, n:2 * n] + p[:, 2 * n:3 * n]


def _rms(x, g):
    return x * lax.rsqrt(jnp.mean(x * x, -1, keepdims=True) + EPS) * g


def _params(n_axes):
    return pltpu.CompilerParams(dimension_semantics=("arbitrary",) * n_axes,
                                vmem_limit_bytes=VMEM_LIMIT)


def _mix_kernel(x_ref, ln_ref, wm_ref, ws_ref, gS_ref, gconv_ref, mC_ref, mn_ref, mm_ref, hS_ref,
                cw_ref, sp_ref, nrm_ref, lb_ref,
                om_ref, gcs_o, gS_o, mC_o, mn_o, mm_o, hS_o,
                xpad_ref, pm_ref, ps_ref, *, bb, c, lv):
    G = bb * N_HEADS
    gh = [(bi, h) for bi in range(bb) for h in range(N_HEADS)]
    rows = lambda bi: slice(bi * c, (bi + 1) * c)

    @pl.when(pl.program_id(1) == 0)
    def _():
        gS_o[...] = gS_ref[...]
        mC_o[...] = mC_ref[...]
        mn_o[...] = mn_ref[...]
        mm_o[...] = mm_ref[...]
        hS_o[...] = hS_ref[...]
        xpad_ref[:, 5:8, :] = gconv_ref[...]

    hb = _rms(x_ref[...].reshape(bb * c, D_MODEL), ln_ref[...]).astype(BF16)

    def proj(c0, c1):
        pm_ref[:, c0:c1] = jnp.dot(hb, wm_ref[:, c0:c1], preferred_element_type=F32)

    ps_ref[...] = jnp.dot(hb, ws_ref[...], preferred_element_type=F32)
    proj(0, 4 * MIX_W)
    for bi in range(bb):
        gcs_o[bi] = pm_ref[bi * c + lv - (CONV_W - 1):bi * c + lv, 0:3 * MIX_W]

    r2 = lax.broadcasted_iota(jnp.int32, (c, c), 0)
    s2 = lax.broadcasted_iota(jnp.int32, (c, c), 1)
    causal = r2 >= s2
    strict = r2 > s2
    tri_bf = causal.astype(BF16)
    eye = (r2 == s2).astype(F32)
    rowc = lax.broadcasted_iota(jnp.int32, (c, 1), 0)
    live = rowc < lv
    lane = lax.broadcasted_iota(jnp.int32, (c, 128), 1)
    sub = min(SUB, c)
    nblk = c // sub
    n_sq = int(math.log2(c)) - 1

    def heads(col0):
        return jnp.stack([pm_ref[rows(bi), col0 + h * HEAD_DIM:col0 + (h + 1) * HEAD_DIM] for bi, h in gh])

    def head_rows(ref, r):
        return jnp.stack([ref[r:r + 1, h * HEAD_DIM:(h + 1) * HEAD_DIM] for _, h in gh])

    def put(col0, val):
        for g, (bi, h) in enumerate(gh):
            om_ref[bi, :, col0 + h * HEAD_DIM:col0 + (h + 1) * HEAD_DIM] = val[g].astype(om_ref.dtype)

    alog = sp_ref[0:1, :]
    bias = sp_ref[1:2, :]
    gates, gates2, cums2, gates2_t, cums2_t, convs = [], [], [], [], [], []
    for bi in range(bb):
        ps = ps_ref[rows(bi), :]
        z = ps + bias
        gt = jnp.where(lane < 4, jax.nn.sigmoid(ps),
                       jnp.where(lane < 8, -jnp.exp(alog) * jax.nn.softplus(z),
                                 jnp.where(lane < 12, z, jax.nn.log_sigmoid(z))))
        gt2 = gt * LOG2E
        cm2 = _cumsum_rows(tri_bf, gt) * LOG2E
        gates.append(gt)
        gates2.append(gt2)
        cums2.append(cm2)
        gates2_t.append(gt2.T)
        cums2_t.append(cm2.T)
        xpad_ref[bi, 8:8 + c, :] = pm_ref[rows(bi), 0:3 * MIX_W]
        acc = cw_ref[0:1, :] * xpad_ref[bi, 5:5 + c, :]
        for j in range(1, CONV_W):
            acc = acc + cw_ref[j:j + 1, :] * xpad_ref[bi, 5 + j:5 + j + c, :]
        convs.append(jax.nn.silu(acc))
        xpad_ref[bi, 5:8, :] = xpad_ref[bi, 5 + c:8 + c, :]

    col = lambda arrs, j: jnp.stack([arrs[bi][:, j + h:j + h + 1] for bi, h in gh])
    row = lambda arrs, j: jnp.stack([arrs[bi][j + h:j + h + 1, :] for bi, h in gh])
    part = lambda p: jnp.stack([convs[bi][:, p * MIX_W + h * HEAD_DIM:p * MIX_W + (h + 1) * HEAD_DIM]
                                for bi, h in gh])


    def gdn():
        q = part(0)
        k = part(1)
        v = part(2)
        q = q * lax.rsqrt(jnp.sum(q * q, -1, keepdims=True) + EPS) * HEAD_DIM ** -0.5
        k = k * lax.rsqrt(jnp.sum(k * k, -1, keepdims=True) + EPS)
        beta = col(gates, 0)
        Gc = col(cums2, 4)
        Gr = row(cums2_t, 4)
        decay = jnp.exp2(jnp.where(causal, Gc - Gr, -jnp.inf))
        S = gS_o[...].reshape(G, HEAD_DIM, HEAD_DIM)
        qk2 = jnp.concatenate([q, k], axis=1)
        P = _bmm_nt(qk2, k)
        R = _bmm(qk2, S)
        QK, KK = P[:, :c], P[:, c:]
        QS, KS = R[:, :c], R[:, c:]
        eG = jnp.exp2(Gc)
        rhs = beta * (v - eG * KS)
        Mp = -jnp.where(strict, beta * decay * KK, 0.0)
        Tinv = eye + Mp
        Ms = _split2(Mp)
        yield
        for _ in range(n_sq):
            Mp = _bmm_split(Ms, Ms)
            Ms = _split2(Mp)
            Tinv = Tinv + _bmm_split(_split2(Tinv), Ms)
            yield
        u = _bmm_split(_split2(Tinv), _split2(rhs))
        yield
        o = eG * QS + _bmm(QK * decay, u)
        Gl = Gc[:, lv - 1:lv, :]
        wl = jnp.where(live, jnp.exp2(Gl - Gc), 0.0)
        gS_o[...] = (jnp.exp2(Gl) * S + _bmm_tn(k * wl, u)).reshape(bb, N_HEADS, HEAD_DIM, HEAD_DIM)
        put(0, _rms(o, head_rows(nrm_ref, 0)) * jax.nn.silu(heads(3 * MIX_W)))

    def mlstm():
        yield
        q = heads(4 * MIX_W)
        k = heads(5 * MIX_W) * HEAD_DIM ** -0.5
        v = heads(6 * MIX_W)
        ig_c = col(gates2, 8)
        ig_r = row(gates2_t, 8)
        Fc = col(cums2, 12)
        Fr = row(cums2_t, 12)
        m0 = jnp.stack([mm_o[bi, h:h + 1, 0:1] for bi, h in gh])
        n_row = jnp.stack([mn_o[bi, h:h + 1, :] for bi, h in gh])
        C = mC_o[...].reshape(G, HEAD_DIM, HEAD_DIM)
        logD = jnp.where(causal, Fc - Fr + ig_r, -jnp.inf)
        m02 = m0 * LOG2E
        b = Fc + m02
        mt = jnp.maximum(b, jnp.max(logD, -1, keepdims=True))
        s = _bmm_nt(q, k) * jnp.exp2(logD - mt)
        inter = jnp.exp2(b - mt)
        yield
        num = _bmm(s, v) + inter * _bmm(q, C)
        den = jnp.sum(s, -1, keepdims=True) + inter * jnp.sum(q * n_row, -1, keepdims=True)
        hm = num / jnp.maximum(jnp.abs(den), jnp.exp2(-mt))
        yield
        mt_l = mt[:, lv - 1:lv, :]
        Fl = Fc[:, lv - 1:lv, :]
        wl = jnp.where(live, jnp.exp2(Fl - Fc + ig_c - mt_l), 0.0)
        d0 = jnp.exp2(Fl + m02 - mt_l)
        kw = k * wl
        mC_o[...] = (d0 * C + _bmm_tn(kw, v)).reshape(bb, N_HEADS, HEAD_DIM, HEAD_DIM)
        n_new = d0 * n_row + jnp.sum(kw, axis=1, keepdims=True)
        m_b = jnp.broadcast_to(mt_l * LN2, (G, 1, HEAD_DIM))
        for g, (bi, h) in enumerate(gh):
            mn_o[bi, h:h + 1, :] = n_new[g]
            mm_o[bi, h:h + 1, :] = m_b[g]
        yield
        put(MIX_W, _rms(hm, head_rows(nrm_ref, 1)) * jax.nn.sigmoid(heads(7 * MIX_W)))

    def hgrn():
        hq = jax.nn.silu(heads(8 * MIX_W))
        hi = heads(10 * MIX_W)
        lbh = lb_ref[...]
        Gs_l, kk_l = [], []
        for bi in range(bb):
            fg = lbh + (1.0 - lbh) * jax.nn.sigmoid(pm_ref[rows(bi), 9 * MIX_W:10 * MIX_W])
            kk_l.append(1.0 - fg)
            Gs_l.append(_cumsum_rows(tri_bf, jnp.log2(fg)))
        kk = jnp.stack([kk_l[bi][:, h * HEAD_DIM:(h + 1) * HEAD_DIM] for bi, h in gh])
        Gm = jnp.stack([Gs_l[bi][:, h * HEAD_DIM:(h + 1) * HEAD_DIM] for bi, h in gh])
        S = hS_o[...].reshape(G, HEAD_DIM, HEAD_DIM)
        yield
        tiles = lambda a: a.reshape(G * nblk, sub, HEAD_DIM)
        hq_t, kk_t, Gm_t = tiles(hq), tiles(kk), tiles(Gm)
        A = jnp.zeros((G, c, c), F32)
        for dlt in range(sub):
            ks = kk_t if dlt == 0 else pltpu.roll(kk_t, dlt, 1)
            Gs = Gm_t if dlt == 0 else pltpu.roll(Gm_t, dlt, 1)
            d = jnp.sum(hq_t * ks * jnp.exp2(jnp.minimum(Gm_t - Gs, 0.0)), -1, keepdims=True)
            A = jnp.where((s2 == r2 - dlt) & (jnp.bitwise_and(r2, sub - 1) >= dlt), d.reshape(G, c, 1), A)
            if dlt % 2 == 1:
                yield
        if nblk > 1:
            Gref = jnp.concatenate(
                [jnp.broadcast_to(Gm[:, I * sub:I * sub + 1, :], (G, sub, HEAD_DIM)) for I in range(nblk)], axis=1)
            qt = hq * jnp.exp2(Gm - Gref)
            pieces = [jnp.zeros((G, sub, c), F32)]
            for I in range(1, nblk):
                kt = kk * jnp.exp2(jnp.minimum(Gm[:, I * sub:I * sub + 1, :] - Gm, 0.0))
                pieces.append(_bmm_nt(qt[:, I * sub:(I + 1) * sub], kt))
                if I % 2 == 1:
                    yield
            A = jnp.where(s2 < jnp.bitwise_and(r2, -sub), jnp.concatenate(pieces, axis=1), A)
        o = _bmm(hq * jnp.exp2(Gm), S) + _bmm(A, hi)
        Gl = Gm[:, lv - 1:lv, :]
        kw = jnp.where(live, kk * jnp.exp2(Gl - Gm), 0.0)
        dcol = jnp.exp2(jnp.swapaxes(jnp.broadcast_to(Gl, (G, 8, HEAD_DIM)), 1, 2)[:, :, 0:1])
        hS_o[...] = (dcol * S + _bmm_tn(kw, hi)).reshape(bb, N_HEADS, HEAD_DIM, HEAD_DIM)
        yield
        put(2 * MIX_W, _rms(o, head_rows(nrm_ref, 2)) * jax.nn.silu(heads(11 * MIX_W)))

    def proj_rest():
        proj(8 * MIX_W, 12 * MIX_W)
        yield
        proj(4 * MIX_W, 8 * MIX_W)

    streams = [gdn(), proj_rest(), hgrn(), mlstm()]
    while streams:
        for st in list(streams):
            if next(st, "done") == "done":
                streams.remove(st)


def _mixers(x3, ln, w_main, w_small, gS, gconv, mC, mn, mm, hS, cw, sp, nrm, lb, l, *, bb, c, lv):
    Bt, Lp, _ = x3.shape
    kern = functools.partial(_mix_kernel, bb=bb, c=c, lv=lv)
    st4 = pl.BlockSpec((bb, N_HEADS, HEAD_DIM, HEAD_DIM), lambda b, j: (b, 0, 0, 0))
    st3 = pl.BlockSpec((bb, N_HEADS, HEAD_DIM), lambda b, j: (b, 0, 0))
    in4 = pl.BlockSpec((None, bb, N_HEADS, HEAD_DIM, HEAD_DIM), lambda b, j: (l, b, 0, 0, 0))
    in3 = pl.BlockSpec((None, bb, N_HEADS, HEAD_DIM), lambda b, j: (l, b, 0, 0))
    full = lambda a: pl.BlockSpec(a.shape, lambda b, j: (0,) * a.ndim)
    once = dict(pipeline_mode=pl.Buffered(1))
    return pl.pallas_call(
        kern,
        grid=(Bt // bb, Lp // c),
        in_specs=[pl.BlockSpec((bb, c, D_MODEL), lambda b, j: (b, j, 0)),
                  full(ln),
                  pl.BlockSpec((None, D_MODEL, N_MIX), lambda b, j: (l, 0, 0), **once),
                  pl.BlockSpec((None, D_MODEL, 128), lambda b, j: (l, 0, 0), **once),
                  in4,
                  pl.BlockSpec((None, bb, CONV_W - 1, 3 * MIX_W), lambda b, j: (l, b, 0, 0)),
                  in4, in3, in3, in4,
                  full(cw), full(sp), full(nrm), full(lb)],
        out_specs=[pl.BlockSpec((bb, c, 3 * MIX_W), lambda b, j: (b, j, 0)),
                   pl.BlockSpec((bb, CONV_W - 1, 3 * MIX_W), lambda b, j: (b, 0, 0)),
                   st4, st4, st3, st3, st4],
        out_shape=[jax.ShapeDtypeStruct((Bt, Lp, 3 * MIX_W), BF16),
                   jax.ShapeDtypeStruct(gconv.shape[1:], F32),
                   jax.ShapeDtypeStruct(gS.shape[1:], F32),
                   jax.ShapeDtypeStruct(mC.shape[1:], F32),
                   jax.ShapeDtypeStruct(mn.shape[1:], F32),
                   jax.ShapeDtypeStruct(mm.shape[1:], F32),
                   jax.ShapeDtypeStruct(hS.shape[1:], F32)],
        scratch_shapes=[pltpu.VMEM((bb, c + 8, 3 * MIX_W), F32),
                        pltpu.VMEM((bb * c, N_MIX), F32),
                        pltpu.VMEM((bb * c, 128), F32)],
        compiler_params=_params(2),
        name="mixers",
    )(x3, ln, w_main, w_small, gS, gconv, mC, mn, mm, hS, cw, sp, nrm, lb)


def _chan_kernel(om_ref, x_ref, st_ref, lnm_ref, wg_ref, wbr_ref, wout_ref, lnf_ref, wup_ref, cw_ref, cb_ref,
                 wd_ref, lno_ref, out_ref, ust_ref, *carry, tm, nseq, tiles_per_seq, st_end, final):
    rows = tm // nseq
    x = x_ref[...]
    hm = _rms(x, lnm_ref[...]).astype(BF16)
    acc = None
    for n in range(N_BRANCH):
        br = jnp.dot(om_ref[:, n * MIX_W:(n + 1) * MIX_W], wbr_ref[n], preferred_element_type=F32)
        pg = jnp.dot(hm, wg_ref[:, n * D_MODEL:(n + 1) * D_MODEL], preferred_element_type=F32)
        t = jax.nn.sigmoid(pg) * br
        acc = t if acc is None else acc + t
    x1 = x + jnp.dot(acc.astype(BF16), wout_ref[...], preferred_element_type=F32)
    hf = _rms(x1, lnf_ref[...]).astype(BF16)

    if tiles_per_seq > 1:
        prev_ref, = carry

        @pl.when(pl.program_id(0) % tiles_per_seq == 0)
        def _():
            prev_ref[:, 0:2, :] = st_ref[...]
    else:
        prev_ref = st_ref

    row8 = lax.broadcasted_iota(jnp.int32, (1, 8, 1), 1)

    def up(col0):
        return jnp.dot(hf, wup_ref[:, col0:col0 + FFN_COLS], preferred_element_type=F32)

    def conv(col0, u_raw):
        cs = slice(col0, col0 + FFN_COLS)
        w0, w1, w2, cb = cw_ref[0:1, cs], cw_ref[1:2, cs], cw_ref[2:3, cs], cb_ref[:, cs]
        uc = u_raw.reshape(nseq, rows, FFN_COLS)
        r1 = pltpu.roll(uc, 1, 1)
        r2 = pltpu.roll(uc, 2, 1)
        p0 = prev_ref[:, 0:1, cs]
        p1 = prev_ref[:, 1:2, cs]
        s1 = jnp.where(row8 == 0, p1, r1[:, 0:8])
        s2 = jnp.where(row8 == 0, p0, jnp.where(row8 == 1, p1, r2[:, 0:8]))
        y = w0 * s2 + w1 * s1 + w2 * uc[:, 0:8] + cb
        if rows > 8:
            y = jnp.concatenate([y, w0 * r2[:, 8:] + w1 * r1[:, 8:] + w2 * uc[:, 8:] + cb], axis=1)
        ust_ref[:, :, cs] = uc[:, st_end - 2:st_end, :]
        if tiles_per_seq > 1:
            prev_ref[:, 0:2, cs] = uc[:, rows - 2:rows, :]
        return y.reshape(tm, FFN_COLS)

    n_chunks = D_FF // FFN_COLS
    out = x1
    nxt = up(0), up(D_FF)
    for jc in range(n_chunks):
        cur = nxt
        if jc + 1 < n_chunks:
            nxt = up((jc + 1) * FFN_COLS), up(D_FF + (jc + 1) * FFN_COLS)
        ua = conv(jc * FFN_COLS, cur[0])
        ub = conv(D_FF + jc * FFN_COLS, cur[1])
        act = (jax.nn.silu(ua) * ub).astype(BF16)
        out = out + jnp.dot(act, wd_ref[jc * FFN_COLS:(jc + 1) * FFN_COLS, :], preferred_element_type=F32)
    out_ref[...] = _rms(out, lno_ref[...]) if final else out


def _chan(om, x, st, W, ln_final, l, *, rows_per_seq, st_end, tm, final):
    T = x.shape[0]
    nseq = max(1, tm // rows_per_seq)
    tiles_per_seq = max(1, rows_per_seq // tm)
    kern = functools.partial(_chan_kernel, tm=tm, nseq=nseq, tiles_per_seq=tiles_per_seq,
                             st_end=st_end, final=final)
    once = dict(pipeline_mode=pl.Buffered(1))
    st_spec = pl.BlockSpec((nseq, FFN_CONV_W - 1, 2 * D_FF), lambda i: (i // tiles_per_seq, 0, 0))
    return pl.pallas_call(
        kern,
        grid=(T // tm,),
        in_specs=[pl.BlockSpec((tm, 3 * MIX_W), lambda i: (i, 0)),
                  pl.BlockSpec((tm, D_MODEL), lambda i: (i, 0)),
                  st_spec,
                  pl.BlockSpec((1, D_MODEL), lambda i: (0, 0)),
                  pl.BlockSpec((None, D_MODEL, N_BRANCH * D_MODEL),
                               lambda i: (l, 0, N_MIX // (N_BRANCH * D_MODEL)), **once),
                  pl.BlockSpec((None, N_BRANCH, MIX_W, D_MODEL), lambda i: (l, 0, 0, 0), **once),
                  pl.BlockSpec((None, D_MODEL, D_MODEL), lambda i: (l, 0, 0), **once),
                  pl.BlockSpec((1, D_MODEL), lambda i: (0, 0)),
                  pl.BlockSpec((None, D_MODEL, 2 * D_FF), lambda i: (l, 0, 0), **once),
                  pl.BlockSpec((FFN_CONV_W, 2 * D_FF), lambda i: (0, 0)),
                  pl.BlockSpec((1, 2 * D_FF), lambda i: (0, 0)),
                  pl.BlockSpec((None, D_FF, D_MODEL), lambda i: (l, 0, 0), **once),
                  pl.BlockSpec((1, D_MODEL), lambda i: (0, 0))],
        out_specs=[pl.BlockSpec((tm, D_MODEL), lambda i: (i, 0)), st_spec],
        out_shape=[jax.ShapeDtypeStruct((T, D_MODEL), F32),
                   jax.ShapeDtypeStruct(st.shape, F32)],
        scratch_shapes=[pltpu.VMEM((1, 8, 2 * D_FF), F32)] if tiles_per_seq > 1 else [],
        compiler_params=_params(1),
        name="chan",
    )(om, x, st, W["ln_mix"][l][None], W["w_main"], W["w_br"], W["w_out"], W["ln_ffn"][l][None], W["w_up"],
      W["fcw"][l], W["fcb"][l][None], W["w_down"], ln_final[None])


def _prep_weights(ln_mix, w_in, gdn_conv_w, gdn_A_log, gdn_dt_bias, gdn_norm, m_ibias, m_fbias,
                  m_norm, lb_all, hgrn_norm, w_br, w_out, ln_ffn, w_up, ffn_conv_w, ffn_conv_b, w_down):
    depth = w_in.shape[0]
    w_main = jnp.concatenate([w_in[:, :, 0:2048], w_in[:, :, 2056:4104], w_in[:, :, 4112:]], axis=2).astype(BF16)
    w_small = jnp.concatenate([w_in[:, :, 2048:2056], w_in[:, :, 4104:4112],
                               jnp.zeros((depth, D_MODEL, 128 - 4 * N_HEADS), F32)], axis=2).astype(BF16)
    z4 = jnp.zeros((depth, N_HEADS), F32)
    pad = jnp.zeros((depth, 128 - 4 * N_HEADS), F32)
    sp = jnp.zeros((depth, 8, 128), F32)
    sp = sp.at[:, 0].set(jnp.concatenate([z4, gdn_A_log, z4, z4, pad], axis=1))
    sp = sp.at[:, 1].set(jnp.concatenate([z4, gdn_dt_bias, m_ibias, m_fbias, pad], axis=1))
    return dict(
        ln_mix=ln_mix, w_main=w_main, w_small=w_small, cw=gdn_conv_w, sp=sp,
        nrm=jnp.stack([gdn_norm, m_norm, hgrn_norm], axis=1), lb=lb_all,
        w_br=w_br.astype(BF16), w_out=w_out.astype(BF16), ln_ffn=ln_ffn,
        w_up=w_up.astype(BF16), fcw=ffn_conv_w, fcb=ffn_conv_b, w_down=w_down.astype(BF16))


def _trunk(x3, states, W, ln_final, *, lv, c, bb, sample):
    Bt, Lp, _ = x3.shape
    T = Bt * Lp
    x = x3.reshape(T, D_MODEL)
    gdn_S, gdn_conv, m_C, m_n, m_m, h_S, ffn_conv = states
    depth = W["w_main"].shape[0]
    mm_in = jnp.broadcast_to(m_m[..., None], m_m.shape + (HEAD_DIM,))
    new = [[] for _ in range(7)]
    for l in range(depth):
        om, gcs, gS, mC, mn, mm, hS = _mixers(
            x.reshape(Bt, Lp, D_MODEL), W["ln_mix"][l][None], W["w_main"], W["w_small"],
            gdn_S, gdn_conv, m_C, m_n, mm_in, h_S,
            W["cw"][l], W["sp"][l], W["nrm"][l], W["lb"][l][None], l, bb=bb, c=c, lv=lv)
        tm = 128 if sample else 256
        x, ust = _chan(om.reshape(T, 3 * MIX_W), x, ffn_conv[l], W, ln_final, l, rows_per_seq=Lp,
                       st_end=lv if sample else tm, tm=tm, final=l == depth - 1)
        outs = (gS, gcs, mC, mn, mm[..., 0], hS, ust)
        for acc, o in zip(new, outs):
            acc.append(o)
    return x.reshape(Bt, Lp, D_MODEL), tuple(jnp.stack(s, axis=0) for s in new)


def kernel(x_prompt, x_sample, state_gdn_S, state_gdn_conv, state_mlstm_C, state_mlstm_n, state_mlstm_m, state_hgrn_S, state_ffn_conv, ln_mix, w_in, gdn_conv_w, gdn_A_log, gdn_dt_bias, gdn_norm, m_ibias, m_fbias, m_norm, hgrn_lb, hgrn_norm, w_br, w_out, ln_ffn, w_up, ffn_conv_w, ffn_conv_b, w_down, ln_final):
    depth = w_in.shape[0]
    lb_all = jnp.cumsum(jax.nn.softmax(hgrn_lb.astype(F32), axis=0), axis=0)
    lb_all = lb_all - lb_all[0]
    layers = _prep_weights(ln_mix, w_in, gdn_conv_w, gdn_A_log, gdn_dt_bias, gdn_norm, m_ibias,
                           m_fbias, m_norm, lb_all, hgrn_norm, w_br, w_out, ln_ffn, w_up, ffn_conv_w,
                           ffn_conv_b, w_down)

    B, L, _ = x_prompt.shape
    H, Dh = N_HEADS, HEAD_DIM
    zeros = (jnp.zeros((depth, B, H, Dh, Dh), F32), jnp.zeros((depth, B, CONV_W - 1, 3 * MIX_W), F32),
             jnp.zeros((depth, B, H, Dh, Dh), F32), jnp.zeros((depth, B, H, Dh), F32),
             jnp.zeros((depth, B, H), F32), jnp.zeros((depth, B, H, Dh, Dh), F32),
             jnp.zeros((depth, B, FFN_CONV_W - 1, 2 * D_FF), F32))
    y_p, st_p = _trunk(x_prompt, zeros, layers, ln_final, lv=CHUNK, c=CHUNK, bb=2, sample=False)

    Bs, Ls, _ = x_sample.shape
    xs = jnp.pad(x_sample, ((0, 0), (0, SAMPLE_PAD - Ls), (0, 0)))
    st_in = (state_gdn_S, state_gdn_conv, state_mlstm_C, state_mlstm_n, state_mlstm_m, state_hgrn_S,
             state_ffn_conv)
    y_s, st_s = _trunk(xs, st_in, layers, ln_final, lv=Ls, c=SAMPLE_PAD, bb=8, sample=True)
    return (y_p, y_s[:, :Ls]) + st_p + st_s
```

```python
import functools
import math

import jax
import jax.numpy as jnp
from jax import lax
from jax.experimental import pallas as pl
from jax.experimental.pallas import tpu as pltpu

D_MODEL = 1024
N_HEADS = 4
HEAD_DIM = 128
MIX_W = N_HEADS * HEAD_DIM
N_BRANCH = 3
CONV_W = 4
FFN_CONV_W = 3
D_FF = 2816
CHUNK = 64
EPS = 1e-6
N_MIX = 6144
SAMPLE_PAD = 8
SUB = 8
FFN_COLS = 2816
LOG2E = 1.4426950408889634
LN2 = 0.6931471805599453

F32 = jnp.float32
BF16 = jnp.bfloat16
VMEM_LIMIT = 56 * 1024 * 1024


def _bmm(a, b):
    return jnp.einsum('gtd,gde->gte', a.astype(BF16), b.astype(BF16), preferred_element_type=F32)


def _bmm_nt(a, b):
    return jnp.einsum('gtd,gsd->gts', a.astype(BF16), b.astype(BF16), preferred_element_type=F32)


def _bmm_tn(a, b):
    return jnp.einsum('gsd,gse->gde', a.astype(BF16), b.astype(BF16), preferred_element_type=F32)


def _split2(a):
    hi = a.astype(BF16)
    return hi, (a - hi.astype(F32)).astype(BF16)


def _bmm_split(a, b):
    a_hi, a_lo = a
    b_hi, b_lo = b
    n = a_hi.shape[1]
    p = jnp.einsum('gtk,gks->gts', jnp.concatenate([a_hi, a_lo], axis=1), b_hi, preferred_element_type=F32)
    return p[:, :n] + p[:, n:] + jnp.einsum('gtk,gks->gts', a_hi, b_lo, preferred_element_type=F32)


def _cumsum_rows(tri_bf, x):
    n = x.shape[1]
    x1 = x.astype(BF16)
    r1 = x - x1.astype(F32)
    x2 = r1.astype(BF16)
    x3 = (r1 - x2.astype(F32)).astype(BF16)
    p = jnp.dot(tri_bf, jnp.concatenate([x1, x2, x3], axis=1), preferred_element_type=F32)
    return p[:, 0:n] + p[:, n:2 * n] + p[:, 2 * n:3 * n]


def _rms(x, g):
    return x * lax.rsqrt(jnp.mean(x * x, -1, keepdims=True) + EPS) * g


def _params(n_axes):
    return pltpu.CompilerParams(dimension_semantics=("arbitrary",) * n_axes,
                                vmem_limit_bytes=VMEM_LIMIT)


def _mix_kernel(x_ref, ln_ref, wm_ref, ws_ref, gS_ref, gconv_ref, mC_ref, mn_ref, mm_ref, hS_ref,
                cw_ref, sp_ref, nrm_ref, lb_ref,
                om_ref, gcs_o, gS_o, mC_o, mn_o, mm_o, hS_o,
                xpad_ref, pm_ref, ps_ref, *, bb, c, lv):
    G = bb * N_HEADS
    gh = [(bi, h) for bi in range(bb) for h in range(N_HEADS)]
    rows = lambda bi: slice(bi * c, (bi + 1) * c)

    @pl.when(pl.program_id(1) == 0)
    def _():
        gS_o[...] = gS_ref[...]
        mC_o[...] = mC_ref[...]
        mn_o[...] = mn_ref[...]
        mm_o[...] = mm_ref[...]
        hS_o[...] = hS_ref[...]
        xpad_ref[:, 5:8, :] = gconv_ref[...]

    hb = _rms(x_ref[...].reshape(bb * c, D_MODEL), ln_ref[...]).astype(BF16)

    def proj(c0, c1):
        pm_ref[:, c0:c1] = jnp.dot(hb, wm_ref[:, c0:c1], preferred_element_type=F32)

    ps_ref[...] = jnp.dot(hb, ws_ref[...], preferred_element_type=F32)
    proj(0, 4 * MIX_W)
    for bi in range(bb):
        gcs_o[bi] = pm_ref[bi * c + lv - (CONV_W - 1):bi * c + lv, 0:3 * MIX_W]

    r2 = lax.broadcasted_iota(jnp.int32, (c, c), 0)
    s2 = lax.broadcasted_iota(jnp.int32, (c, c), 1)
    causal = r2 >= s2
    strict = r2 > s2
    tri_bf = causal.astype(BF16)
    eye = (r2 == s2).astype(F32)
    rowc = lax.broadcasted_iota(jnp.int32, (c, 1), 0)
    live = rowc < lv
    lane = lax.broadcasted_iota(jnp.int32, (c, 128), 1)
    sub = min(SUB, c)
    nblk = c // sub
    n_sq = int(math.log2(c)) - 1

    def heads(col0):
        return jnp.stack([pm_ref[rows(bi), col0 + h * HEAD_DIM:col0 + (h + 1) * HEAD_DIM] for bi, h in gh])

    def head_rows(ref, r):
        return jnp.stack([ref[r:r + 1, h * HEAD_DIM:(h + 1) * HEAD_DIM] for _, h in gh])

    def put(col0, val):
        for g, (bi, h) in enumerate(gh):
            om_ref[bi, :, col0 + h * HEAD_DIM:col0 + (h + 1) * HEAD_DIM] = val[g].astype(om_ref.dtype)

    alog = sp_ref[0:1, :]
    bias = sp_ref[1:2, :]
    gates, gates2, cums2, gates2_t, cums2_t, convs = [], [], [], [], [], []
    for bi in range(bb):
        ps = ps_ref[rows(bi), :]
        z = ps + bias
        gt = jnp.where(lane < 4, jax.nn.sigmoid(ps),
                       jnp.where(lane < 8, -jnp.exp(alog) * jax.nn.softplus(z),
                                 jnp.where(lane < 12, z, jax.nn.log_sigmoid(z))))
        gt2 = gt * LOG2E
        cm2 = _cumsum_rows(tri_bf, gt) * LOG2E
        gates.append(gt)
        gates2.append(gt2)
        cums2.append(cm2)
        gates2_t.append(gt2.T)
        cums2_t.append(cm2.T)
        xpad_ref[bi, 8:8 + c, :] = pm_ref[rows(bi), 0:3 * MIX_W]
        acc = cw_ref[0:1, :] * xpad_ref[bi, 5:5 + c, :]
        for j in range(1, CONV_W):
            acc = acc + cw_ref[j:j + 1, :] * xpad_ref[bi, 5 + j:5 + j + c, :]
        convs.append(jax.nn.silu(acc))
        xpad_ref[bi, 5:8, :] = xpad_ref[bi, 5 + c:8 + c, :]

    col = lambda arrs, j: jnp.stack([arrs[bi][:, j + h:j + h + 1] for bi, h in gh])
    row = lambda arrs, j: jnp.stack([arrs[bi][j + h:j + h + 1, :] for bi, h in gh])
    part = lambda p: jnp.stack([convs[bi][:, p * MIX_W + h * HEAD_DIM:p * MIX_W + (h + 1) * HEAD_DIM]
                                for bi, h in gh])


    def gdn():
        q = part(0)
        k = part(1)
        v = part(2)
        q = q * lax.rsqrt(jnp.sum(q * q, -1, keepdims=True) + EPS) * HEAD_DIM ** -0.5
        k = k * lax.rsqrt(jnp.sum(k * k, -1, keepdims=True) + EPS)
        beta = col(gates, 0)
        Gc = col(cums2, 4)
        Gr = row(cums2_t, 4)
        decay = jnp.exp2(jnp.where(causal, Gc - Gr, -jnp.inf))
        S = gS_o[...].reshape(G, HEAD_DIM, HEAD_DIM)
        qk2 = jnp.concatenate([q, k], axis=1)
        P = _bmm_nt(qk2, k)
        R = _bmm(qk2, S)
        QK, KK = P[:, :c], P[:, c:]
        QS, KS = R[:, :c], R[:, c:]
        eG = jnp.exp2(Gc)
        rhs = beta * (v - eG * KS)
        Mp = -jnp.where(strict, beta * decay * KK, 0.0)
        Tinv = eye + Mp
        Ms = _split2(Mp)
        yield
        for _ in range(n_sq):
            Mp = _bmm_split(Ms, Ms)
            Ms = _split2(Mp)
            Tinv = Tinv + _bmm_split(_split2(Tinv), Ms)
            yield
        u = _bmm_split(_split2(Tinv), _split2(rhs))
        yield
        o = eG * QS + _bmm(QK * decay, u)
        Gl = Gc[:, lv - 1:lv, :]
        wl = jnp.where(live, jnp.exp2(Gl - Gc), 0.0)
        gS_o[...] = (jnp.exp2(Gl) * S + _bmm_tn(k * wl, u)).reshape(bb, N_HEADS, HEAD_DIM, HEAD_DIM)
        put(0, _rms(o, head_rows(nrm_ref, 0)) * jax.nn.silu(heads(3 * MIX_W)))

    def mlstm():
        yield
        q = heads(4 * MIX_W)
        k = heads(5 * MIX_W) * HEAD_DIM ** -0.5
        v = heads(6 * MIX_W)
        ig_c = col(gates2, 8)
        ig_r = row(gates2_t, 8)
        Fc = col(cums2, 12)
        Fr = row(cums2_t, 12)
        m0 = jnp.stack([mm_o[bi, h:h + 1, 0:1] for bi, h in gh])
        n_row = jnp.stack([mn_o[bi, h:h + 1, :] for bi, h in gh])
        C = mC_o[...].reshape(G, HEAD_DIM, HEAD_DIM)
        logD = jnp.where(causal, Fc - Fr + ig_r, -jnp.inf)
        m02 = m0 * LOG2E
        b = Fc + m02
        mt = jnp.maximum(b, jnp.max(logD, -1, keepdims=True))
        s = _bmm_nt(q, k) * jnp.exp2(logD - mt)
        inter = jnp.exp2(b - mt)
        yield
        num = _bmm(s, v) + inter * _bmm(q, C)
        den = jnp.sum(s, -1, keepdims=True) + inter * jnp.sum(q * n_row, -1, keepdims=True)
        hm = num / jnp.maximum(jnp.abs(den), jnp.exp2(-mt))
        yield
        mt_l = mt[:, lv - 1:lv, :]
        Fl = Fc[:, lv - 1:lv, :]
        wl = jnp.where(live, jnp.exp2(Fl - Fc + ig_c - mt_l), 0.0)
        d0 = jnp.exp2(Fl + m02 - mt_l)
        kw = k * wl
        mC_o[...] = (d0 * C + _bmm_tn(kw, v)).reshape(bb, N_HEADS, HEAD_DIM, HEAD_DIM)
        n_new = d0 * n_row + jnp.sum(kw, axis=1, keepdims=True)
        m_b = jnp.broadcast_to(mt_l * LN2, (G, 1, HEAD_DIM))
        for g, (bi, h) in enumerate(gh):
            mn_o[bi, h:h + 1, :] = n_new[g]
            mm_o[bi, h:h + 1, :] = m_b[g]
        yield
        put(MIX_W, _rms(hm, head_rows(nrm_ref, 1)) * jax.nn.sigmoid(heads(7 * MIX_W)))

    def hgrn():
        hq = jax.nn.silu(heads(8 * MIX_W))
        hi = heads(10 * MIX_W)
        lbh = lb_ref[...]
        Gs_l, kk_l = [], []
        for bi in range(bb):
            fg = lbh + (1.0 - lbh) * jax.nn.sigmoid(pm_ref[rows(bi), 9 * MIX_W:10 * MIX_W])
            kk_l.append(1.0 - fg)
            Gs_l.append(_cumsum_rows(tri_bf, jnp.log2(fg)))
        kk = jnp.stack([kk_l[bi][:, h * HEAD_DIM:(h + 1) * HEAD_DIM] for bi, h in gh])
        Gm = jnp.stack([Gs_l[bi][:, h * HEAD_DIM:(h + 1) * HEAD_DIM] for bi, h in gh])
        S = hS_o[...].reshape(G, HEAD_DIM, HEAD_DIM)
        yield
        tiles = lambda a: a.reshape(G * nblk, sub, HEAD_DIM)
        hq_t, kk_t, Gm_t = tiles(hq), tiles(kk), tiles(Gm)
        A = jnp.zeros((G, c, c), F32)
        for dlt in range(sub):
            ks = kk_t if dlt == 0 else pltpu.roll(kk_t, dlt, 1)
            Gs = Gm_t if dlt == 0 else pltpu.roll(Gm_t, dlt, 1)
            d = jnp.sum(hq_t * ks * jnp.exp2(jnp.minimum(Gm_t - Gs, 0.0)), -1, keepdims=True)
            A = jnp.where((s2 == r2 - dlt) & (jnp.bitwise_and(r2, sub - 1) >= dlt), d.reshape(G, c, 1), A)
            if dlt % 2 == 1:
                yield
        if nblk > 1:
            Gref = jnp.concatenate(
                [jnp.broadcast_to(Gm[:, I * sub:I * sub + 1, :], (G, sub, HEAD_DIM)) for I in range(nblk)], axis=1)
            qt = hq * jnp.exp2(Gm - Gref)
            pieces = [jnp.zeros((G, sub, c), F32)]
            for I in range(1, nblk):
                kt = kk * jnp.exp2(jnp.minimum(Gm[:, I * sub:I * sub + 1, :] - Gm, 0.0))
                pieces.append(_bmm_nt(qt[:, I * sub:(I + 1) * sub], kt))
                if I % 2 == 1:
                    yield
            A = jnp.where(s2 < jnp.bitwise_and(r2, -sub), jnp.concatenate(pieces, axis=1), A)
        o = _bmm(hq * jnp.exp2(Gm), S) + _bmm(A, hi)
        Gl = Gm[:, lv - 1:lv, :]
        kw = jnp.where(live, kk * jnp.exp2(Gl - Gm), 0.0)
        dcol = jnp.exp2(jnp.swapaxes(jnp.broadcast_to(Gl, (G, 8, HEAD_DIM)), 1, 2)[:, :, 0:1])
        hS_o[...] = (dcol * S + _bmm_tn(kw, hi)).reshape(bb, N_HEADS, HEAD_DIM, HEAD_DIM)
        yield
        put(2 * MIX_W, _rms(o, head_rows(nrm_ref, 2)) * jax.nn.silu(heads(11 * MIX_W)))

    def proj_rest():
        proj(8 * MIX_W, 12 * MIX_W)
        yield
        proj(4 * MIX_W, 8 * MIX_W)

    streams = [gdn(), proj_rest(), hgrn(), mlstm()]
    while streams:
        for st in list(streams):
            if next(st, "done") == "done":
                streams.remove(st)


def _mixers(x3, ln, w_main, w_small, gS, gconv, mC, mn, mm, hS, cw, sp, nrm, lb, l, *, bb, c, lv):
    Bt, Lp, _ = x3.shape
    kern = functools.partial(_mix_kernel, bb=bb, c=c, lv=lv)
    st4 = pl.BlockSpec((bb, N_HEADS, HEAD_DIM, HEAD_DIM), lambda b, j: (b, 0, 0, 0))
    st3 = pl.BlockSpec((bb, N_HEADS, HEAD_DIM), lambda b, j: (b, 0, 0))
    in4 = pl.BlockSpec((None, bb, N_HEADS, HEAD_DIM, HEAD_DIM), lambda b, j: (l, b, 0, 0, 0))
    in3 = pl.BlockSpec((None, bb, N_HEADS, HEAD_DIM), lambda b, j: (l, b, 0, 0))
    full = lambda a: pl.BlockSpec(a.shape, lambda b, j: (0,) * a.ndim)
    once = dict(pipeline_mode=pl.Buffered(1))
    return pl.pallas_call(
        kern,
        grid=(Bt // bb, Lp // c),
        in_specs=[pl.BlockSpec((bb, c, D_MODEL), lambda b, j: (b, j, 0)),
                  full(ln),
                  pl.BlockSpec((None, D_MODEL, N_MIX), lambda b, j: (l, 0, 0), **once),
                  pl.BlockSpec((None, D_MODEL, 128), lambda b, j: (l, 0, 0), **once),
                  in4,
                  pl.BlockSpec((None, bb, CONV_W - 1, 3 * MIX_W), lambda b, j: (l, b, 0, 0)),
                  in4, in3, in3, in4,
                  full(cw), full(sp), full(nrm), full(lb)],
        out_specs=[pl.BlockSpec((bb, c, 3 * MIX_W), lambda b, j: (b, j, 0)),
                   pl.BlockSpec((bb, CONV_W - 1, 3 * MIX_W), lambda b, j: (b, 0, 0)),
                   st4, st4, st3, st3, st4],
        out_shape=[jax.ShapeDtypeStruct((Bt, Lp, 3 * MIX_W), BF16),
                   jax.ShapeDtypeStruct(gconv.shape[1:], F32),
                   jax.ShapeDtypeStruct(gS.shape[1:], F32),
                   jax.ShapeDtypeStruct(mC.shape[1:], F32),
                   jax.ShapeDtypeStruct(mn.shape[1:], F32),
                   jax.ShapeDtypeStruct(mm.shape[1:], F32),
                   jax.ShapeDtypeStruct(hS.shape[1:], F32)],
        scratch_shapes=[pltpu.VMEM((bb, c + 8, 3 * MIX_W), F32),
                        pltpu.VMEM((bb * c, N_MIX), F32),
                        pltpu.VMEM((bb * c, 128), F32)],
        compiler_params=_params(2),
        name="mixers",
    )(x3, ln, w_main, w_small, gS, gconv, mC, mn, mm, hS, cw, sp, nrm, lb)


def _chan_kernel(om_ref, x_ref, st_ref, lnm_ref, wg_ref, wbr_ref, wout_ref, lnf_ref, wup_ref, cw_ref, cb_ref,
                 wd_ref, lno_ref, out_ref, ust_ref, *carry, tm, nseq, tiles_per_seq, st_end, final):
    rows = tm // nseq
    x = x_ref[...]
    hm = _rms(x, lnm_ref[...]).astype(BF16)
    acc = None
    for n in range(N_BRANCH):
        br = jnp.dot(om_ref[:, n * MIX_W:(n + 1) * MIX_W], wbr_ref[n], preferred_element_type=F32)
        pg = jnp.dot(hm, wg_ref[:, n * D_MODEL:(n + 1) * D_MODEL], preferred_element_type=F32)
        t = jax.nn.sigmoid(pg) * br
        acc = t if acc is None else acc + t
    x1 = x + jnp.dot(acc.astype(BF16), wout_ref[...], preferred_element_type=F32)
    hf = _rms(x1, lnf_ref[...]).astype(BF16)

    if tiles_per_seq > 1:
        prev_ref, = carry

        @pl.when(pl.program_id(0) % tiles_per_seq == 0)
        def _():
            prev_ref[:, 0:2, :] = st_ref[...]
    else:
        prev_ref = st_ref

    row8 = lax.broadcasted_iota(jnp.int32, (1, 8, 1), 1)

    def up(col0):
        return jnp.dot(hf, wup_ref[:, col0:col0 + FFN_COLS], preferred_element_type=F32)

    def conv(col0, u_raw):
        cs = slice(col0, col0 + FFN_COLS)
        w0, w1, w2, cb = cw_ref[0:1, cs], cw_ref[1:2, cs], cw_ref[2:3, cs], cb_ref[:, cs]
        uc = u_raw.reshape(nseq, rows, FFN_COLS)
        r1 = pltpu.roll(uc, 1, 1)
        r2 = pltpu.roll(uc, 2, 1)
        p0 = prev_ref[:, 0:1, cs]
        p1 = prev_ref[:, 1:2, cs]
        s1 = jnp.where(row8 == 0, p1, r1[:, 0:8])
        s2 = jnp.where(row8 == 0, p0, jnp.where(row8 == 1, p1, r2[:, 0:8]))
        y = w0 * s2 + w1 * s1 + w2 * uc[:, 0:8] + cb
        if rows > 8:
            y = jnp.concatenate([y, w0 * r2[:, 8:] + w1 * r1[:, 8:] + w2 * uc[:, 8:] + cb], axis=1)
        ust_ref[:, :, cs] = uc[:, st_end - 2:st_end, :]
        if tiles_per_seq > 1:
            prev_ref[:, 0:2, cs] = uc[:, rows - 2:rows, :]
        return y.reshape(tm, FFN_COLS)

    n_chunks = D_FF // FFN_COLS
    out = x1
    nxt = up(0), up(D_FF)
    for jc in range(n_chunks):
        cur = nxt
        if jc + 1 < n_chunks:
            nxt = up((jc + 1) * FFN_COLS), up(D_FF + (jc + 1) * FFN_COLS)
        ua = conv(jc * FFN_COLS, cur[0])
        ub = conv(D_FF + jc * FFN_COLS, cur[1])
        act = (jax.nn.silu(ua) * ub).astype(BF16)
        out = out + jnp.dot(act, wd_ref[jc * FFN_COLS:(jc + 1) * FFN_COLS, :], preferred_element_type=F32)
    out_ref[...] = _rms(out, lno_ref[...]) if final else out


def _chan(om, x, st, W, ln_final, l, *, rows_per_seq, st_end, tm, final):
    T = x.shape[0]
    nseq = max(1, tm // rows_per_seq)
    tiles_per_seq = max(1, rows_per_seq // tm)
    kern = functools.partial(_chan_kernel, tm=tm, nseq=nseq, tiles_per_seq=tiles_per_seq,
                             st_end=st_end, final=final)
    once = dict(pipeline_mode=pl.Buffered(1))
    st_spec = pl.BlockSpec((nseq, FFN_CONV_W - 1, 2 * D_FF), lambda i: (i // tiles_per_seq, 0, 0))
    return pl.pallas_call(
        kern,
        grid=(T // tm,),
        in_specs=[pl.BlockSpec((tm, 3 * MIX_W), lambda i: (i, 0)),
                  pl.BlockSpec((tm, D_MODEL), lambda i: (i, 0)),
                  st_spec,
                  pl.BlockSpec((1, D_MODEL), lambda i: (0, 0)),
                  pl.BlockSpec((None, D_MODEL, N_BRANCH * D_MODEL),
                               lambda i: (l, 0, N_MIX // (N_BRANCH * D_MODEL)), **once),
                  pl.BlockSpec((None, N_BRANCH, MIX_W, D_MODEL), lambda i: (l, 0, 0, 0), **once),
                  pl.BlockSpec((None, D_MODEL, D_MODEL), lambda i: (l, 0, 0), **once),
                  pl.BlockSpec((1, D_MODEL), lambda i: (0, 0)),
                  pl.BlockSpec((None, D_MODEL, 2 * D_FF), lambda i: (l, 0, 0), **once),
                  pl.BlockSpec((FFN_CONV_W, 2 * D_FF), lambda i: (0, 0)),
                  pl.BlockSpec((1, 2 * D_FF), lambda i: (0, 0)),
                  pl.BlockSpec((None, D_FF, D_MODEL), lambda i: (l, 0, 0), **once),
                  pl.BlockSpec((1, D_MODEL), lambda i: (0, 0))],
        out_specs=[pl.BlockSpec((tm, D_MODEL), lambda i: (i, 0)), st_spec],
        out_shape=[jax.ShapeDtypeStruct((T, D_MODEL), F32),
                   jax.ShapeDtypeStruct(st.shape, F32)],
        scratch_shapes=[pltpu.VMEM((1, 8, 2 * D_FF), F32)] if tiles_per_seq > 1 else [],
        compiler_params=_params(1),
        name="chan",
    )(om, x, st, W["ln_mix"][l][None], W["w_main"], W["w_br"], W["w_out"], W["ln_ffn"][l][None], W["w_up"],
      W["fcw"][l], W["fcb"][l][None], W["w_down"], ln_final[None])


_W_IN_GROUPS = ((0, 2048), (2056, 2048), (4112, 5120))
_W_TILE = 512


def _reorder_kernel(a_ref, b_ref, o_ref, *, plan):
    j = pl.program_id(1)
    cat = jnp.concatenate([a_ref[...], b_ref[...]], axis=1)
    for lo, hi, s in plan:
        @pl.when((j >= lo) & (j < hi))
        def _():
            o_ref[...] = cat[:, s:s + _W_TILE].astype(BF16)


def _reorder_w_in(w_in):
    depth, d, _ = w_in.shape
    plan, out0 = [], 0
    for src0, n in _W_IN_GROUPS:
        plan.append((out0 // _W_TILE, (out0 + n) // _W_TILE, src0 - out0))
        out0 += n
    assert all(0 <= s < 128 for _, _, s in plan)
    return pl.pallas_call(
        functools.partial(_reorder_kernel, plan=tuple(plan)),
        grid=(depth, out0 // _W_TILE),
        in_specs=[pl.BlockSpec((None, d, _W_TILE), lambda l, j: (l, 0, j)),
                  pl.BlockSpec((None, d, 128), lambda l, j: (l, 0, (j + 1) * (_W_TILE // 128)))],
        out_specs=pl.BlockSpec((None, d, _W_TILE), lambda l, j: (l, 0, j)),
        out_shape=jax.ShapeDtypeStruct((depth, d, out0), BF16),
        compiler_params=_params(2),
        name="reorder_w_in",
    )(w_in, w_in)


def _prep_weights(ln_mix, w_in, gdn_conv_w, gdn_A_log, gdn_dt_bias, gdn_norm, m_ibias, m_fbias,
                  m_norm, lb_all, hgrn_norm, w_br, w_out, ln_ffn, w_up, ffn_conv_w, ffn_conv_b, w_down):
    depth = w_in.shape[0]
    w_main = _reorder_w_in(w_in)
    w_small = jnp.concatenate([w_in[:, :, 2048:2056], w_in[:, :, 4104:4112],
                               jnp.zeros((depth, D_MODEL, 128 - 4 * N_HEADS), F32)], axis=2).astype(BF16)
    z4 = jnp.zeros((depth, N_HEADS), F32)
    pad = jnp.zeros((depth, 128 - 4 * N_HEADS), F32)
    sp = jnp.zeros((depth, 8, 128), F32)
    sp = sp.at[:, 0].set(jnp.concatenate([z4, gdn_A_log, z4, z4, pad], axis=1))
    sp = sp.at[:, 1].set(jnp.concatenate([z4, gdn_dt_bias, m_ibias, m_fbias, pad], axis=1))
    return dict(
        ln_mix=ln_mix, w_main=w_main, w_small=w_small, cw=gdn_conv_w, sp=sp,
        nrm=jnp.stack([gdn_norm, m_norm, hgrn_norm], axis=1), lb=lb_all,
        w_br=w_br.astype(BF16), w_out=w_out.astype(BF16), ln_ffn=ln_ffn,
        w_up=w_up.astype(BF16), fcw=ffn_conv_w, fcb=ffn_conv_b, w_down=w_down.astype(BF16))


def _trunk(x3, states, W, ln_final, *, lv, c, bb, sample):
    Bt, Lp, _ = x3.shape
    T = Bt * Lp
    x = x3.reshape(T, D_MODEL)
    gdn_S, gdn_conv, m_C, m_n, m_m, h_S, ffn_conv = states
    depth = W["w_main"].shape[0]
    mm_in = jnp.broadcast_to(m_m[..., None], m_m.shape + (HEAD_DIM,))
    new = [[] for _ in range(7)]
    for l in range(depth):
        om, gcs, gS, mC, mn, mm, hS = _mixers(
            x.reshape(Bt, Lp, D_MODEL), W["ln_mix"][l][None], W["w_main"], W["w_small"],
            gdn_S, gdn_conv, m_C, m_n, mm_in, h_S,
            W["cw"][l], W["sp"][l], W["nrm"][l], W["lb"][l][None], l, bb=bb, c=c, lv=lv)
        tm = 128 if sample else 256
        x, ust = _chan(om.reshape(T, 3 * MIX_W), x, ffn_conv[l], W, ln_final, l, rows_per_seq=Lp,
                       st_end=lv if sample else tm, tm=tm, final=l == depth - 1)
        outs = (gS, gcs, mC, mn, mm[..., 0], hS, ust)
        for acc, o in zip(new, outs):
            acc.append(o)
    return x.reshape(Bt, Lp, D_MODEL), tuple(jnp.stack(s, axis=0) for s in new)


def kernel(x_prompt, x_sample, state_gdn_S, state_gdn_conv, state_mlstm_C, state_mlstm_n, state_mlstm_m, state_hgrn_S, state_ffn_conv, ln_mix, w_in, gdn_conv_w, gdn_A_log, gdn_dt_bias, gdn_norm, m_ibias, m_fbias, m_norm, hgrn_lb, hgrn_norm, w_br, w_out, ln_ffn, w_up, ffn_conv_w, ffn_conv_b, w_down, ln_final):
    depth = w_in.shape[0]
    lb_all = jnp.cumsum(jax.nn.softmax(hgrn_lb.astype(F32), axis=0), axis=0)
    lb_all = lb_all - lb_all[0]
    layers = _prep_weights(ln_mix, w_in, gdn_conv_w, gdn_A_log, gdn_dt_bias, gdn_norm, m_ibias,
                           m_fbias, m_norm, lb_all, hgrn_norm, w_br, w_out, ln_ffn, w_up, ffn_conv_w,
                           ffn_conv_b, w_down)

    B, L, _ = x_prompt.shape
    H, Dh = N_HEADS, HEAD_DIM
    zeros = (jnp.zeros((depth, B, H, Dh, Dh), F32), jnp.zeros((depth, B, CONV_W - 1, 3 * MIX_W), F32),
             jnp.zeros((depth, B, H, Dh, Dh), F32), jnp.zeros((depth, B, H, Dh), F32),
             jnp.zeros((depth, B, H), F32), jnp.zeros((depth, B, H, Dh, Dh), F32),
             jnp.zeros((depth, B, FFN_CONV_W - 1, 2 * D_FF), F32))
    y_p, st_p = _trunk(x_prompt, zeros, layers, ln_final, lv=CHUNK, c=CHUNK, bb=4, sample=False)

    Bs, Ls, _ = x_sample.shape
    xs = jnp.pad(x_sample, ((0, 0), (0, SAMPLE_PAD - Ls), (0, 0)))
    st_in = (state_gdn_S, state_gdn_conv, state_mlstm_C, state_mlstm_n, state_mlstm_m, state_hgrn_S,
             state_ffn_conv)
    y_s, st_s = _trunk(xs, st_in, layers, ln_final, lv=Ls, c=SAMPLE_PAD, bb=8, sample=True)
    return (y_p, y_s[:, :Ls]) + st_p + st_s
```

```python
import functools
import math

import jax
import jax.numpy as jnp
from jax import lax
from jax.experimental import pallas as pl
from jax.experimental.pallas import tpu as pltpu

D_MODEL = 1024
N_HEADS = 4
HEAD_DIM = 128
MIX_W = N_HEADS * HEAD_DIM
N_BRANCH = 3
CONV_W = 4
FFN_CONV_W = 3
D_FF = 2816
CHUNK = 64
EPS = 1e-6
N_MIX = 6144
SAMPLE_PAD = 8
SUB = 8
FFN_COLS = 2816
LOG2E = 1.4426950408889634
LN2 = 0.6931471805599453

F32 = jnp.float32
BF16 = jnp.bfloat16
VMEM_LIMIT = 56 * 1024 * 1024


def _bmm(a, b):
    return jnp.einsum('gtd,gde->gte', a.astype(BF16), b.astype(BF16), preferred_element_type=F32)


def _bmm_nt(a, b):
    return jnp.einsum('gtd,gsd->gts', a.astype(BF16), b.astype(BF16), preferred_element_type=F32)


def _bmm_tn(a, b):
    return jnp.einsum('gsd,gse->gde', a.astype(BF16), b.astype(BF16), preferred_element_type=F32)


def _split2(a):
    hi = a.astype(BF16)
    return hi, (a - hi.astype(F32)).astype(BF16)


def _bmm_split(a, b):
    a_hi, a_lo = a
    b_hi, b_lo = b
    n = a_hi.shape[1]
    p = jnp.einsum('gtk,gks->gts', jnp.concatenate([a_hi, a_lo], axis=1), b_hi, preferred_element_type=F32)
    return p[:, :n] + p[:, n:] + jnp.einsum('gtk,gks->gts', a_hi, b_lo, preferred_element_type=F32)


def _cumsum_rows(tri_bf, x):
    n = x.shape[1]
    x1 = x.astype(BF16)
    r1 = x - x1.astype(F32)
    x2 = r1.astype(BF16)
    x3 = (r1 - x2.astype(F32)).astype(BF16)
    p = jnp.dot(tri_bf, jnp.concatenate([x1, x2, x3], axis=1), preferred_element_type=F32)
    return p[:, 0:n] + p[:, n:2 * n] + p[:, 2 * n:3 * n]


def _rms(x, g):
    return x * lax.rsqrt(jnp.mean(x * x, -1, keepdims=True) + EPS) * g


def _params(n_axes):
    return pltpu.CompilerParams(dimension_semantics=("arbitrary",) * n_axes,
                                vmem_limit_bytes=VMEM_LIMIT)


def _mix_kernel(x_ref, ln_ref, wm_ref, ws_ref, gS_ref, gconv_ref, mC_ref, mn_ref, mm_ref, hS_ref,
                cw_ref, sp_ref, nrm_ref, lb_ref,
                om_ref, gcs_o, gS_o, mC_o, mn_o, mm_o, hS_o,
                xpad_ref, pm_ref, ps_ref, *, bb, c, lv):
    G = bb * N_HEADS
    gh = [(bi, h) for bi in range(bb) for h in range(N_HEADS)]
    rows = lambda bi: slice(bi * c, (bi + 1) * c)

    @pl.when(pl.program_id(1) == 0)
    def _():
        gS_o[...] = gS_ref[...]
        mC_o[...] = mC_ref[...]
        mn_o[...] = mn_ref[...]
        mm_o[...] = mm_ref[...]
        hS_o[...] = hS_ref[...]
        xpad_ref[:, 5:8, :] = gconv_ref[...]

    hb = _rms(x_ref[...].reshape(bb * c, D_MODEL), ln_ref[...]).astype(BF16)

    def proj(g0, g1):
        cs = slice(g0 * MIX_W, g1 * MIX_W)
        pm_ref[:, cs] = jnp.dot(hb, wm_ref[:, cs], preferred_element_type=F32)

    ps_ref[...] = jnp.dot(hb, ws_ref[...], preferred_element_type=F32)
    proj(0, 4)
    for bi in range(bb):
        gcs_o[bi] = pm_ref[bi * c + lv - (CONV_W - 1):bi * c + lv, 0:3 * MIX_W]

    r2 = lax.broadcasted_iota(jnp.int32, (c, c), 0)
    s2 = lax.broadcasted_iota(jnp.int32, (c, c), 1)
    causal = r2 >= s2
    strict = r2 > s2
    tri_bf = causal.astype(BF16)
    eye = (r2 == s2).astype(F32)
    rowc = lax.broadcasted_iota(jnp.int32, (c, 1), 0)
    live = rowc < lv
    lane = lax.broadcasted_iota(jnp.int32, (c, 128), 1)
    sub = min(SUB, c)
    nblk = c // sub
    n_sq = int(math.log2(c)) - 1

    def heads(col0):
        return jnp.stack([pm_ref[rows(bi), col0 + h * HEAD_DIM:col0 + (h + 1) * HEAD_DIM] for bi, h in gh])

    def head_rows(ref, r):
        return jnp.stack([ref[r:r + 1, h * HEAD_DIM:(h + 1) * HEAD_DIM] for _, h in gh])

    def put(col0, val):
        for g, (bi, h) in enumerate(gh):
            om_ref[bi, :, col0 + h * HEAD_DIM:col0 + (h + 1) * HEAD_DIM] = val[g].astype(om_ref.dtype)

    alog = sp_ref[0:1, :]
    bias = sp_ref[1:2, :]
    gates, gates2, cums2, gates2_t, cums2_t, convs = [], [], [], [], [], []
    for bi in range(bb):
        ps = ps_ref[rows(bi), :]
        z = ps + bias
        gt = jnp.where(lane < 4, jax.nn.sigmoid(ps),
                       jnp.where(lane < 8, -jnp.exp(alog) * jax.nn.softplus(z),
                                 jnp.where(lane < 12, z, jax.nn.log_sigmoid(z))))
        gt2 = gt * LOG2E
        cm2 = _cumsum_rows(tri_bf, gt) * LOG2E
        gates.append(gt)
        gates2.append(gt2)
        cums2.append(cm2)
        gates2_t.append(gt2.T)
        cums2_t.append(cm2.T)
        xpad_ref[bi, 8:8 + c, :] = pm_ref[rows(bi), 0:3 * MIX_W]
        acc = cw_ref[0:1, :] * xpad_ref[bi, 5:5 + c, :]
        for j in range(1, CONV_W):
            acc = acc + cw_ref[j:j + 1, :] * xpad_ref[bi, 5 + j:5 + j + c, :]
        convs.append(jax.nn.silu(acc))
        xpad_ref[bi, 5:8, :] = xpad_ref[bi, 5 + c:8 + c, :]

    col = lambda arrs, j: jnp.stack([arrs[bi][:, j + h:j + h + 1] for bi, h in gh])
    row = lambda arrs, j: jnp.stack([arrs[bi][j + h:j + h + 1, :] for bi, h in gh])
    part = lambda p: jnp.stack([convs[bi][:, p * MIX_W + h * HEAD_DIM:p * MIX_W + (h + 1) * HEAD_DIM]
                                for bi, h in gh])


    def gdn():
        q = part(0)
        k = part(1)
        v = part(2)
        q = q * lax.rsqrt(jnp.sum(q * q, -1, keepdims=True) + EPS) * HEAD_DIM ** -0.5
        k = k * lax.rsqrt(jnp.sum(k * k, -1, keepdims=True) + EPS)
        beta = col(gates, 0)
        Gc = col(cums2, 4)
        Gr = row(cums2_t, 4)
        decay = jnp.exp2(jnp.where(causal, Gc - Gr, -jnp.inf))
        S = gS_o[...].reshape(G, HEAD_DIM, HEAD_DIM)
        qk2 = jnp.concatenate([q, k], axis=1)
        P = _bmm_nt(qk2, k)
        R = _bmm(qk2, S)
        QK, KK = P[:, :c], P[:, c:]
        QS, KS = R[:, :c], R[:, c:]
        eG = jnp.exp2(Gc)
        rhs = beta * (v - eG * KS)
        Mp = -jnp.where(strict, beta * decay * KK, 0.0)
        Tinv = eye + Mp
        Ms = _split2(Mp)
        yield
        for _ in range(n_sq):
            Mp = _bmm_split(Ms, Ms)
            Ms = _split2(Mp)
            Tinv = Tinv + _bmm_split(_split2(Tinv), Ms)
            yield
        u = _bmm_split(_split2(Tinv), _split2(rhs))
        yield
        o = eG * QS + _bmm(QK * decay, u)
        Gl = Gc[:, lv - 1:lv, :]
        wl = jnp.where(live, jnp.exp2(Gl - Gc), 0.0)
        gS_o[...] = (jnp.exp2(Gl) * S + _bmm_tn(k * wl, u)).reshape(bb, N_HEADS, HEAD_DIM, HEAD_DIM)
        put(0, _rms(o, head_rows(nrm_ref, 0)) * jax.nn.silu(heads(3 * MIX_W)))

    def mlstm():
        yield
        q = heads(4 * MIX_W)
        k = heads(5 * MIX_W) * HEAD_DIM ** -0.5
        ig_c = col(gates2, 8)
        ig_r = row(gates2_t, 8)
        Fc = col(cums2, 12)
        Fr = row(cums2_t, 12)
        m0 = jnp.stack([mm_o[bi, h:h + 1, 0:1] for bi, h in gh])
        n_row = jnp.stack([mn_o[bi, h:h + 1, :] for bi, h in gh])
        C = mC_o[...].reshape(G, HEAD_DIM, HEAD_DIM)
        logD = jnp.where(causal, Fc - Fr + ig_r, -jnp.inf)
        m02 = m0 * LOG2E
        b = Fc + m02
        mt = jnp.maximum(b, jnp.max(logD, -1, keepdims=True))
        s = _bmm_nt(q, k) * jnp.exp2(logD - mt)
        inter = jnp.exp2(b - mt)
        yield
        v = heads(6 * MIX_W)
        num = _bmm(s, v) + inter * _bmm(q, C)
        den = jnp.sum(s, -1, keepdims=True) + inter * jnp.sum(q * n_row, -1, keepdims=True)
        hm = num / jnp.maximum(jnp.abs(den), jnp.exp2(-mt))
        yield
        mt_l = mt[:, lv - 1:lv, :]
        Fl = Fc[:, lv - 1:lv, :]
        wl = jnp.where(live, jnp.exp2(Fl - Fc + ig_c - mt_l), 0.0)
        d0 = jnp.exp2(Fl + m02 - mt_l)
        kw = k * wl
        mC_o[...] = (d0 * C + _bmm_tn(kw, v)).reshape(bb, N_HEADS, HEAD_DIM, HEAD_DIM)
        n_new = d0 * n_row + jnp.sum(kw, axis=1, keepdims=True)
        m_b = jnp.broadcast_to(mt_l * LN2, (G, 1, HEAD_DIM))
        for g, (bi, h) in enumerate(gh):
            mn_o[bi, h:h + 1, :] = n_new[g]
            mm_o[bi, h:h + 1, :] = m_b[g]
        yield
        put(MIX_W, _rms(hm, head_rows(nrm_ref, 1)) * jax.nn.sigmoid(heads(7 * MIX_W)))

    def hgrn():
        hq = jax.nn.silu(heads(8 * MIX_W))
        lbh = lb_ref[...]
        Gs_l, kk_l = [], []
        for bi in range(bb):
            fg = lbh + (1.0 - lbh) * jax.nn.sigmoid(pm_ref[rows(bi), 9 * MIX_W:10 * MIX_W])
            kk_l.append(1.0 - fg)
            Gs_l.append(_cumsum_rows(tri_bf, jnp.log2(fg)))
        kk = jnp.stack([kk_l[bi][:, h * HEAD_DIM:(h + 1) * HEAD_DIM] for bi, h in gh])
        Gm = jnp.stack([Gs_l[bi][:, h * HEAD_DIM:(h + 1) * HEAD_DIM] for bi, h in gh])
        S = hS_o[...].reshape(G, HEAD_DIM, HEAD_DIM)
        yield
        tiles = lambda a: a.reshape(G * nblk, sub, HEAD_DIM)
        hq_t, kk_t, Gm_t = tiles(hq), tiles(kk), tiles(Gm)
        A = jnp.zeros((G, c, c), F32)
        for dlt in range(sub):
            ks = kk_t if dlt == 0 else pltpu.roll(kk_t, dlt, 1)
            Gs = Gm_t if dlt == 0 else pltpu.roll(Gm_t, dlt, 1)
            d = jnp.sum(hq_t * ks * jnp.exp2(jnp.minimum(Gm_t - Gs, 0.0)), -1, keepdims=True)
            A = jnp.where((s2 == r2 - dlt) & (jnp.bitwise_and(r2, sub - 1) >= dlt), d.reshape(G, c, 1), A)
            if dlt % 2 == 1:
                yield
        if nblk > 1:
            Gref = jnp.concatenate(
                [jnp.broadcast_to(Gm[:, I * sub:I * sub + 1, :], (G, sub, HEAD_DIM)) for I in range(nblk)], axis=1)
            qt = hq * jnp.exp2(Gm - Gref)
            pieces = [jnp.zeros((G, sub, c), F32)]
            for I in range(1, nblk):
                kt = kk * jnp.exp2(jnp.minimum(Gm[:, I * sub:I * sub + 1, :] - Gm, 0.0))
                pieces.append(_bmm_nt(qt[:, I * sub:(I + 1) * sub], kt))
                if I % 2 == 1:
                    yield
            A = jnp.where(s2 < jnp.bitwise_and(r2, -sub), jnp.concatenate(pieces, axis=1), A)
        hi = heads(10 * MIX_W)
        o = _bmm(hq * jnp.exp2(Gm), S) + _bmm(A, hi)
        Gl = Gm[:, lv - 1:lv, :]
        kw = jnp.where(live, kk * jnp.exp2(Gl - Gm), 0.0)
        dcol = jnp.exp2(jnp.swapaxes(jnp.broadcast_to(Gl, (G, 8, HEAD_DIM)), 1, 2)[:, :, 0:1])
        hS_o[...] = (dcol * S + _bmm_tn(kw, hi)).reshape(bb, N_HEADS, HEAD_DIM, HEAD_DIM)
        yield
        put(2 * MIX_W, _rms(o, head_rows(nrm_ref, 2)) * jax.nn.silu(heads(11 * MIX_W)))

    def proj_rest():
        proj(8, 12)
        yield
        proj(4, 8)

    streams = [gdn(), proj_rest(), hgrn(), mlstm()]
    while streams:
        for st in list(streams):
            if next(st, "done") == "done":
                streams.remove(st)


def _mixers(x3, ln, w_main, w_small, gS, gconv, mC, mn, mm, hS, cw, sp, nrm, lb, l, *, bb, c, lv):
    Bt, Lp, _ = x3.shape
    kern = functools.partial(_mix_kernel, bb=bb, c=c, lv=lv)
    st4 = pl.BlockSpec((bb, N_HEADS, HEAD_DIM, HEAD_DIM), lambda b, j: (b, 0, 0, 0))
    st3 = pl.BlockSpec((bb, N_HEADS, HEAD_DIM), lambda b, j: (b, 0, 0))
    in4 = pl.BlockSpec((None, bb, N_HEADS, HEAD_DIM, HEAD_DIM), lambda b, j: (l, b, 0, 0, 0))
    in3 = pl.BlockSpec((None, bb, N_HEADS, HEAD_DIM), lambda b, j: (l, b, 0, 0))
    full = lambda a: pl.BlockSpec(a.shape, lambda b, j: (0,) * a.ndim)
    once = dict(pipeline_mode=pl.Buffered(1))
    return pl.pallas_call(
        kern,
        grid=(Bt // bb, Lp // c),
        in_specs=[pl.BlockSpec((bb, c, D_MODEL), lambda b, j: (b, j, 0)),
                  full(ln),
                  pl.BlockSpec((None, D_MODEL, N_MIX), lambda b, j: (l, 0, 0), **once),
                  pl.BlockSpec((None, D_MODEL, 128), lambda b, j: (l, 0, 0), **once),
                  in4,
                  pl.BlockSpec((None, bb, CONV_W - 1, 3 * MIX_W), lambda b, j: (l, b, 0, 0)),
                  in4, in3, in3, in4,
                  full(cw), full(sp), full(nrm), full(lb)],
        out_specs=[pl.BlockSpec((bb, c, 3 * MIX_W), lambda b, j: (b, j, 0)),
                   pl.BlockSpec((bb, CONV_W - 1, 3 * MIX_W), lambda b, j: (b, 0, 0)),
                   st4, st4, st3, st3, st4],
        out_shape=[jax.ShapeDtypeStruct((Bt, Lp, 3 * MIX_W), BF16),
                   jax.ShapeDtypeStruct(gconv.shape[1:], F32),
                   jax.ShapeDtypeStruct(gS.shape[1:], F32),
                   jax.ShapeDtypeStruct(mC.shape[1:], F32),
                   jax.ShapeDtypeStruct(mn.shape[1:], F32),
                   jax.ShapeDtypeStruct(mm.shape[1:], F32),
                   jax.ShapeDtypeStruct(hS.shape[1:], F32)],
        scratch_shapes=[pltpu.VMEM((bb, c + 8, 3 * MIX_W), F32),
                        pltpu.VMEM((bb * c, N_MIX), F32),
                        pltpu.VMEM((bb * c, 128), F32)],
        compiler_params=_params(2),
        name="mixers",
    )(x3, ln, w_main, w_small, gS, gconv, mC, mn, mm, hS, cw, sp, nrm, lb)


def _chan_kernel(om_ref, x_ref, st_ref, lnm_ref, wg_ref, wbr_ref, wout_ref, lnf_ref, wup_ref, cw_ref, cb_ref,
                 wd_ref, lno_ref, out_ref, ust_ref, *carry, tm, nseq, tiles_per_seq, st_end, final):
    rows = tm // nseq
    x = x_ref[...]
    hm = _rms(x, lnm_ref[...]).astype(BF16)
    acc = None
    for n in range(N_BRANCH):
        br = jnp.dot(om_ref[:, n * MIX_W:(n + 1) * MIX_W], wbr_ref[n], preferred_element_type=F32)
        pg = jnp.dot(hm, wg_ref[:, n * D_MODEL:(n + 1) * D_MODEL], preferred_element_type=F32)
        t = jax.nn.sigmoid(pg) * br
        acc = t if acc is None else acc + t
    x1 = x + jnp.dot(acc.astype(BF16), wout_ref[...], preferred_element_type=F32)
    hf = _rms(x1, lnf_ref[...]).astype(BF16)

    if tiles_per_seq > 1:
        prev_ref, = carry

        @pl.when(pl.program_id(0) % tiles_per_seq == 0)
        def _():
            prev_ref[:, 0:2, :] = st_ref[...]
    else:
        prev_ref = st_ref

    row8 = lax.broadcasted_iota(jnp.int32, (1, 8, 1), 1)

    def up(col0):
        return jnp.dot(hf, wup_ref[:, col0:col0 + FFN_COLS], preferred_element_type=F32)

    def conv(col0, u_raw):
        cs = slice(col0, col0 + FFN_COLS)
        w0, w1, w2, cb = cw_ref[0:1, cs], cw_ref[1:2, cs], cw_ref[2:3, cs], cb_ref[:, cs]
        uc = u_raw.reshape(nseq, rows, FFN_COLS)
        r1 = pltpu.roll(uc, 1, 1)
        r2 = pltpu.roll(uc, 2, 1)
        p0 = prev_ref[:, 0:1, cs]
        p1 = prev_ref[:, 1:2, cs]
        s1 = jnp.where(row8 == 0, p1, r1[:, 0:8])
        s2 = jnp.where(row8 == 0, p0, jnp.where(row8 == 1, p1, r2[:, 0:8]))
        y = w0 * s2 + w1 * s1 + w2 * uc[:, 0:8] + cb
        if rows > 8:
            y = jnp.concatenate([y, w0 * r2[:, 8:] + w1 * r1[:, 8:] + w2 * uc[:, 8:] + cb], axis=1)
        ust_ref[:, :, cs] = uc[:, st_end - 2:st_end, :]
        if tiles_per_seq > 1:
            prev_ref[:, 0:2, cs] = uc[:, rows - 2:rows, :]
        return y.reshape(tm, FFN_COLS)

    n_chunks = D_FF // FFN_COLS
    out = x1
    nxt = up(0), up(D_FF)
    for jc in range(n_chunks):
        cur = nxt
        if jc + 1 < n_chunks:
            nxt = up((jc + 1) * FFN_COLS), up(D_FF + (jc + 1) * FFN_COLS)
        ua = conv(jc * FFN_COLS, cur[0])
        ub = conv(D_FF + jc * FFN_COLS, cur[1])
        act = (jax.nn.silu(ua) * ub).astype(BF16)
        out = out + jnp.dot(act, wd_ref[jc * FFN_COLS:(jc + 1) * FFN_COLS, :], preferred_element_type=F32)
    out_ref[...] = _rms(out, lno_ref[...]) if final else out


def _chan(om, x, st, W, ln_final, l, *, rows_per_seq, st_end, tm, final):
    T = x.shape[0]
    nseq = max(1, tm // rows_per_seq)
    tiles_per_seq = max(1, rows_per_seq // tm)
    kern = functools.partial(_chan_kernel, tm=tm, nseq=nseq, tiles_per_seq=tiles_per_seq,
                             st_end=st_end, final=final)
    once = dict(pipeline_mode=pl.Buffered(1))
    st_spec = pl.BlockSpec((nseq, FFN_CONV_W - 1, 2 * D_FF), lambda i: (i // tiles_per_seq, 0, 0))
    return pl.pallas_call(
        kern,
        grid=(T // tm,),
        in_specs=[pl.BlockSpec((tm, 3 * MIX_W), lambda i: (i, 0)),
                  pl.BlockSpec((tm, D_MODEL), lambda i: (i, 0)),
                  st_spec,
                  pl.BlockSpec((1, D_MODEL), lambda i: (0, 0)),
                  pl.BlockSpec((None, D_MODEL, N_BRANCH * D_MODEL),
                               lambda i: (l, 0, N_MIX // (N_BRANCH * D_MODEL)), **once),
                  pl.BlockSpec((None, N_BRANCH, MIX_W, D_MODEL), lambda i: (l, 0, 0, 0), **once),
                  pl.BlockSpec((None, D_MODEL, D_MODEL), lambda i: (l, 0, 0), **once),
                  pl.BlockSpec((1, D_MODEL), lambda i: (0, 0)),
                  pl.BlockSpec((None, D_MODEL, 2 * D_FF), lambda i: (l, 0, 0), **once),
                  pl.BlockSpec((FFN_CONV_W, 2 * D_FF), lambda i: (0, 0)),
                  pl.BlockSpec((1, 2 * D_FF), lambda i: (0, 0)),
                  pl.BlockSpec((None, D_FF, D_MODEL), lambda i: (l, 0, 0), **once),
                  pl.BlockSpec((1, D_MODEL), lambda i: (0, 0))],
        out_specs=[pl.BlockSpec((tm, D_MODEL), lambda i: (i, 0)), st_spec],
        out_shape=[jax.ShapeDtypeStruct((T, D_MODEL), F32),
                   jax.ShapeDtypeStruct(st.shape, F32)],
        scratch_shapes=[pltpu.VMEM((1, 8, 2 * D_FF), F32)] if tiles_per_seq > 1 else [],
        compiler_params=_params(1),
        name="chan",
    )(om, x, st, W["ln_mix"][l][None], W["w_main"], W["w_br"], W["w_out"], W["ln_ffn"][l][None], W["w_up"],
      W["fcw"][l], W["fcb"][l][None], W["w_down"], ln_final[None])


def _prep_weights(ln_mix, w_in, gdn_conv_w, gdn_A_log, gdn_dt_bias, gdn_norm, m_ibias, m_fbias,
                  m_norm, lb_all, hgrn_norm, w_br, w_out, ln_ffn, w_up, ffn_conv_w, ffn_conv_b, w_down):
    depth = w_in.shape[0]
    w_main = jnp.concatenate([w_in[:, :, 0:2048], w_in[:, :, 2056:4104], w_in[:, :, 4112:]], axis=2).astype(BF16)
    w_small = jnp.concatenate([w_in[:, :, 2048:2056], w_in[:, :, 4104:4112],
                               jnp.zeros((depth, D_MODEL, 128 - 4 * N_HEADS), F32)], axis=2).astype(BF16)
    z4 = jnp.zeros((depth, N_HEADS), F32)
    pad = jnp.zeros((depth, 128 - 4 * N_HEADS), F32)
    sp = jnp.zeros((depth, 8, 128), F32)
    sp = sp.at[:, 0].set(jnp.concatenate([z4, gdn_A_log, z4, z4, pad], axis=1))
    sp = sp.at[:, 1].set(jnp.concatenate([z4, gdn_dt_bias, m_ibias, m_fbias, pad], axis=1))
    return dict(
        ln_mix=ln_mix, w_main=w_main, w_small=w_small, cw=gdn_conv_w, sp=sp,
        nrm=jnp.stack([gdn_norm, m_norm, hgrn_norm], axis=1), lb=lb_all,
        w_br=w_br.astype(BF16), w_out=w_out.astype(BF16), ln_ffn=ln_ffn,
        w_up=w_up.astype(BF16), fcw=ffn_conv_w, fcb=ffn_conv_b, w_down=w_down.astype(BF16))


def _trunk(x3, states, W, ln_final, *, lv, c, bb, sample):
    Bt, Lp, _ = x3.shape
    T = Bt * Lp
    x = x3.reshape(T, D_MODEL)
    gdn_S, gdn_conv, m_C, m_n, m_m, h_S, ffn_conv = states
    depth = W["w_main"].shape[0]
    mm_in = jnp.broadcast_to(m_m[..., None], m_m.shape + (HEAD_DIM,))
    new = [[] for _ in range(7)]
    for l in range(depth):
        om, gcs, gS, mC, mn, mm, hS = _mixers(
            x.reshape(Bt, Lp, D_MODEL), W["ln_mix"][l][None], W["w_main"], W["w_small"],
            gdn_S, gdn_conv, m_C, m_n, mm_in, h_S,
            W["cw"][l], W["sp"][l], W["nrm"][l], W["lb"][l][None], l, bb=bb, c=c, lv=lv)
        tm = 256
        x, ust = _chan(om.reshape(T, 3 * MIX_W), x, ffn_conv[l], W, ln_final, l, rows_per_seq=Lp,
                       st_end=lv if sample else tm, tm=tm, final=l == depth - 1)
        outs = (gS, gcs, mC, mn, mm[..., 0], hS, ust)
        for acc, o in zip(new, outs):
            acc.append(o)
    return x.reshape(Bt, Lp, D_MODEL), tuple(jnp.stack(s, axis=0) for s in new)


def kernel(x_prompt, x_sample, state_gdn_S, state_gdn_conv, state_mlstm_C, state_mlstm_n, state_mlstm_m, state_hgrn_S, state_ffn_conv, ln_mix, w_in, gdn_conv_w, gdn_A_log, gdn_dt_bias, gdn_norm, m_ibias, m_fbias, m_norm, hgrn_lb, hgrn_norm, w_br, w_out, ln_ffn, w_up, ffn_conv_w, ffn_conv_b, w_down, ln_final):
    depth = w_in.shape[0]
    lb_all = jnp.cumsum(jax.nn.softmax(hgrn_lb.astype(F32), axis=0), axis=0)
    lb_all = lb_all - lb_all[0]
    layers = _prep_weights(ln_mix, w_in, gdn_conv_w, gdn_A_log, gdn_dt_bias, gdn_norm, m_ibias,
                           m_fbias, m_norm, lb_all, hgrn_norm, w_br, w_out, ln_ffn, w_up, ffn_conv_w,
                           ffn_conv_b, w_down)

    B, L, _ = x_prompt.shape
    H, Dh = N_HEADS, HEAD_DIM
    zeros = (jnp.zeros((depth, B, H, Dh, Dh), F32), jnp.zeros((depth, B, CONV_W - 1, 3 * MIX_W), F32),
             jnp.zeros((depth, B, H, Dh, Dh), F32), jnp.zeros((depth, B, H, Dh), F32),
             jnp.zeros((depth, B, H), F32), jnp.zeros((depth, B, H, Dh, Dh), F32),
             jnp.zeros((depth, B, FFN_CONV_W - 1, 2 * D_FF), F32))
    y_p, st_p = _trunk(x_prompt, zeros, layers, ln_final, lv=CHUNK, c=CHUNK, bb=4, sample=False)

    Bs, Ls, _ = x_sample.shape
    xs = jnp.pad(x_sample, ((0, 0), (0, SAMPLE_PAD - Ls), (0, 0)))
    st_in = (state_gdn_S, state_gdn_conv, state_mlstm_C, state_mlstm_n, state_mlstm_m, state_hgrn_S,
             state_ffn_conv)
    y_s, st_s = _trunk(xs, st_in, layers, ln_final, lv=Ls, c=SAMPLE_PAD, bb=8, sample=True)
    return (y_p, y_s[:, :Ls]) + st_p + st_s
```

```python
import functools
import math

import jax
import jax.numpy as jnp
from jax import lax
from jax.experimental import pallas as pl
from jax.experimental.pallas import tpu as pltpu

D_MODEL = 1024
N_HEADS = 4
HEAD_DIM = 128
MIX_W = N_HEADS * HEAD_DIM
N_BRANCH = 3
CONV_W = 4
FFN_CONV_W = 3
D_FF = 2816
CHUNK = 64
EPS = 1e-6
N_MIX = 6144
SAMPLE_PAD = 8
SUB = 8
FFN_COLS = 2816
LOG2E = 1.4426950408889634
LN2 = 0.6931471805599453

F32 = jnp.float32
BF16 = jnp.bfloat16
VMEM_LIMIT = 56 * 1024 * 1024


def _bmm(a, b):
    return jnp.einsum('gtd,gde->gte', a.astype(BF16), b.astype(BF16), preferred_element_type=F32)


def _bmm_nt(a, b):
    return jnp.einsum('gtd,gsd->gts', a.astype(BF16), b.astype(BF16), preferred_element_type=F32)


def _bmm_tn(a, b):
    return jnp.einsum('gsd,gse->gde', a.astype(BF16), b.astype(BF16), preferred_element_type=F32)


def _split2(a):
    hi = a.astype(BF16)
    return hi, (a - hi.astype(F32)).astype(BF16)


def _bmm_split(a, b):
    a_hi, a_lo = a
    b_hi, b_lo = b
    n = a_hi.shape[1]
    p = jnp.einsum('gtk,gks->gts', jnp.concatenate([a_hi, a_lo], axis=1), b_hi, preferred_element_type=F32)
    return p[:, :n] + p[:, n:] + jnp.einsum('gtk,gks->gts', a_hi, b_lo, preferred_element_type=F32)


def _cumsum_rows(tri_bf, x):
    n = x.shape[1]
    x1 = x.astype(BF16)
    r1 = x - x1.astype(F32)
    x2 = r1.astype(BF16)
    x3 = (r1 - x2.astype(F32)).astype(BF16)
    p = jnp.dot(tri_bf, jnp.concatenate([x1, x2, x3], axis=1), preferred_element_type=F32)
    return p[:, 0:n] + p[:, n:2 * n] + p[:, 2 * n:3 * n]


def _rms(x, g):
    return x * lax.rsqrt(jnp.mean(x * x, -1, keepdims=True) + EPS) * g


def _params(n_axes):
    return pltpu.CompilerParams(dimension_semantics=("arbitrary",) * n_axes,
                                vmem_limit_bytes=VMEM_LIMIT)


def _mix_kernel(x_ref, ln_ref, wm_ref, ws_ref, gS_ref, gconv_ref, mC_ref, mn_ref, mm_ref, hS_ref,
                cw_ref, sp_ref, nrm_ref, lb_ref, *rest, bb, c, lv, layer, n_earlier):
    earlier = [rest[5 * d:5 * d + 5] for d in range(n_earlier)]
    om_ref, gcs_o, *st_o = rest[5 * n_earlier:5 * n_earlier + 7]
    xpad_ref, pm_ref, ps_ref = rest[5 * n_earlier + 7:]
    gS_o, mC_o, mn_o, mm_o, hS_o = [r.at[layer] for r in st_o] if n_earlier else st_o
    G = bb * N_HEADS
    gh = [(bi, h) for bi in range(bb) for h in range(N_HEADS)]
    rows = lambda bi: slice(bi * c, (bi + 1) * c)

    @pl.when(pl.program_id(1) == 0)
    def _():
        gS_o[...] = gS_ref[...]
        mC_o[...] = mC_ref[...]
        mn_o[...] = mn_ref[...]
        mm_o[...] = mm_ref[...]
        hS_o[...] = hS_ref[...]
        xpad_ref[:, 5:8, :] = gconv_ref[...]
        for d, refs in enumerate(earlier):
            for dst, src in zip(st_o, refs):
                dst[d] = src[...]

    hb = _rms(x_ref[...].reshape(bb * c, D_MODEL), ln_ref[...]).astype(BF16)

    def proj(g0, g1):
        cs = slice(g0 * MIX_W, g1 * MIX_W)
        pm_ref[:, cs] = jnp.dot(hb, wm_ref[:, cs], preferred_element_type=F32)

    ps_ref[...] = jnp.dot(hb, ws_ref[...], preferred_element_type=F32)
    proj(0, 4)
    for bi in range(bb):
        gcs_o[bi] = pm_ref[bi * c + lv - (CONV_W - 1):bi * c + lv, 0:3 * MIX_W]

    r2 = lax.broadcasted_iota(jnp.int32, (c, c), 0)
    s2 = lax.broadcasted_iota(jnp.int32, (c, c), 1)
    causal = r2 >= s2
    strict = r2 > s2
    tri_bf = causal.astype(BF16)
    eye = (r2 == s2).astype(F32)
    rowc = lax.broadcasted_iota(jnp.int32, (c, 1), 0)
    live = rowc < lv
    lane = lax.broadcasted_iota(jnp.int32, (c, 128), 1)
    sub = min(SUB, c)
    nblk = c // sub
    n_sq = int(math.log2(c)) - 1

    def heads(col0):
        return jnp.stack([pm_ref[rows(bi), col0 + h * HEAD_DIM:col0 + (h + 1) * HEAD_DIM] for bi, h in gh])

    def head_rows(ref, r):
        return jnp.stack([ref[r:r + 1, h * HEAD_DIM:(h + 1) * HEAD_DIM] for _, h in gh])

    def put(col0, val):
        for g, (bi, h) in enumerate(gh):
            om_ref[bi, :, col0 + h * HEAD_DIM:col0 + (h + 1) * HEAD_DIM] = val[g].astype(om_ref.dtype)

    alog = sp_ref[0:1, :]
    bias = sp_ref[1:2, :]
    gates, gates2, cums2, gates2_t, cums2_t, convs = [], [], [], [], [], []
    for bi in range(bb):
        ps = ps_ref[rows(bi), :]
        z = ps + bias
        gt = jnp.where(lane < 4, jax.nn.sigmoid(ps),
                       jnp.where(lane < 8, -jnp.exp(alog) * jax.nn.softplus(z),
                                 jnp.where(lane < 12, z, jax.nn.log_sigmoid(z))))
        gt2 = gt * LOG2E
        cm2 = _cumsum_rows(tri_bf, gt) * LOG2E
        gates.append(gt)
        gates2.append(gt2)
        cums2.append(cm2)
        gates2_t.append(gt2.T)
        cums2_t.append(cm2.T)
        xpad_ref[bi, 8:8 + c, :] = pm_ref[rows(bi), 0:3 * MIX_W]
        acc = cw_ref[0:1, :] * xpad_ref[bi, 5:5 + c, :]
        for j in range(1, CONV_W):
            acc = acc + cw_ref[j:j + 1, :] * xpad_ref[bi, 5 + j:5 + j + c, :]
        convs.append(jax.nn.silu(acc))
        xpad_ref[bi, 5:8, :] = xpad_ref[bi, 5 + c:8 + c, :]

    col = lambda arrs, j: jnp.stack([arrs[bi][:, j + h:j + h + 1] for bi, h in gh])
    row = lambda arrs, j: jnp.stack([arrs[bi][j + h:j + h + 1, :] for bi, h in gh])
    part = lambda p: jnp.stack([convs[bi][:, p * MIX_W + h * HEAD_DIM:p * MIX_W + (h + 1) * HEAD_DIM]
                                for bi, h in gh])


    def gdn():
        q = part(0)
        k = part(1)
        v = part(2)
        q = q * lax.rsqrt(jnp.sum(q * q, -1, keepdims=True) + EPS) * HEAD_DIM ** -0.5
        k = k * lax.rsqrt(jnp.sum(k * k, -1, keepdims=True) + EPS)
        beta = col(gates, 0)
        Gc = col(cums2, 4)
        Gr = row(cums2_t, 4)
        decay = jnp.exp2(jnp.where(causal, Gc - Gr, -jnp.inf))
        S = gS_o[...].reshape(G, HEAD_DIM, HEAD_DIM)
        qk2 = jnp.concatenate([q, k], axis=1)
        P = _bmm_nt(qk2, k)
        R = _bmm(qk2, S)
        QK, KK = P[:, :c], P[:, c:]
        QS, KS = R[:, :c], R[:, c:]
        eG = jnp.exp2(Gc)
        rhs = beta * (v - eG * KS)
        Mp = -jnp.where(strict, beta * decay * KK, 0.0)
        Tinv = eye + Mp
        Ms = _split2(Mp)
        yield
        for _ in range(n_sq):
            Mp = _bmm_split(Ms, Ms)
            Ms = _split2(Mp)
            Tinv = Tinv + _bmm_split(_split2(Tinv), Ms)
            yield
        u = _bmm_split(_split2(Tinv), _split2(rhs))
        yield
        o = eG * QS + _bmm(QK * decay, u)
        Gl = Gc[:, lv - 1:lv, :]
        wl = jnp.where(live, jnp.exp2(Gl - Gc), 0.0)
        gS_o[...] = (jnp.exp2(Gl) * S + _bmm_tn(k * wl, u)).reshape(bb, N_HEADS, HEAD_DIM, HEAD_DIM)
        put(0, _rms(o, head_rows(nrm_ref, 0)) * jax.nn.silu(heads(3 * MIX_W)))

    def mlstm():
        yield
        q = heads(4 * MIX_W)
        k = heads(5 * MIX_W) * HEAD_DIM ** -0.5
        ig_c = col(gates2, 8)
        ig_r = row(gates2_t, 8)
        Fc = col(cums2, 12)
        Fr = row(cums2_t, 12)
        m0 = jnp.stack([mm_o[bi, h:h + 1, 0:1] for bi, h in gh])
        n_row = jnp.stack([mn_o[bi, h:h + 1, :] for bi, h in gh])
        C = mC_o[...].reshape(G, HEAD_DIM, HEAD_DIM)
        logD = jnp.where(causal, Fc - Fr + ig_r, -jnp.inf)
        m02 = m0 * LOG2E
        b = Fc + m02
        mt = jnp.maximum(b, jnp.max(logD, -1, keepdims=True))
        s = _bmm_nt(q, k) * jnp.exp2(logD - mt)
        inter = jnp.exp2(b - mt)
        yield
        v = heads(6 * MIX_W)
        num = _bmm(s, v) + inter * _bmm(q, C)
        den = jnp.sum(s, -1, keepdims=True) + inter * jnp.sum(q * n_row, -1, keepdims=True)
        hm = num / jnp.maximum(jnp.abs(den), jnp.exp2(-mt))
        yield
        mt_l = mt[:, lv - 1:lv, :]
        Fl = Fc[:, lv - 1:lv, :]
        wl = jnp.where(live, jnp.exp2(Fl - Fc + ig_c - mt_l), 0.0)
        d0 = jnp.exp2(Fl + m02 - mt_l)
        kw = k * wl
        mC_o[...] = (d0 * C + _bmm_tn(kw, v)).reshape(bb, N_HEADS, HEAD_DIM, HEAD_DIM)
        n_new = d0 * n_row + jnp.sum(kw, axis=1, keepdims=True)
        m_b = jnp.broadcast_to(mt_l * LN2, (G, 1, HEAD_DIM))
        for g, (bi, h) in enumerate(gh):
            mn_o[bi, h:h + 1, :] = n_new[g]
            mm_o[bi, h:h + 1, :] = m_b[g]
        yield
        put(MIX_W, _rms(hm, head_rows(nrm_ref, 1)) * jax.nn.sigmoid(heads(7 * MIX_W)))

    def hgrn():
        hq = jax.nn.silu(heads(8 * MIX_W))
        lbh = lb_ref[...]
        Gs_l, kk_l = [], []
        for bi in range(bb):
            fg = lbh + (1.0 - lbh) * jax.nn.sigmoid(pm_ref[rows(bi), 9 * MIX_W:10 * MIX_W])
            kk_l.append(1.0 - fg)
            Gs_l.append(_cumsum_rows(tri_bf, jnp.log2(fg)))
        kk = jnp.stack([kk_l[bi][:, h * HEAD_DIM:(h + 1) * HEAD_DIM] for bi, h in gh])
        Gm = jnp.stack([Gs_l[bi][:, h * HEAD_DIM:(h + 1) * HEAD_DIM] for bi, h in gh])
        S = hS_o[...].reshape(G, HEAD_DIM, HEAD_DIM)
        yield
        tiles = lambda a: a.reshape(G * nblk, sub, HEAD_DIM)
        hq_t, kk_t, Gm_t = tiles(hq), tiles(kk), tiles(Gm)
        A = jnp.zeros((G, c, c), F32)
        for dlt in range(sub):
            ks = kk_t if dlt == 0 else pltpu.roll(kk_t, dlt, 1)
            Gs = Gm_t if dlt == 0 else pltpu.roll(Gm_t, dlt, 1)
            d = jnp.sum(hq_t * ks * jnp.exp2(jnp.minimum(Gm_t - Gs, 0.0)), -1, keepdims=True)
            A = jnp.where((s2 == r2 - dlt) & (jnp.bitwise_and(r2, sub - 1) >= dlt), d.reshape(G, c, 1), A)
            if dlt % 2 == 1:
                yield
        if nblk > 1:
            Gref = jnp.concatenate(
                [jnp.broadcast_to(Gm[:, I * sub:I * sub + 1, :], (G, sub, HEAD_DIM)) for I in range(nblk)], axis=1)
            qt = hq * jnp.exp2(Gm - Gref)
            pieces = [jnp.zeros((G, sub, c), F32)]
            for I in range(1, nblk):
                kt = kk * jnp.exp2(jnp.minimum(Gm[:, I * sub:I * sub + 1, :] - Gm, 0.0))
                pieces.append(_bmm_nt(qt[:, I * sub:(I + 1) * sub], kt))
                if I % 2 == 1:
                    yield
            A = jnp.where(s2 < jnp.bitwise_and(r2, -sub), jnp.concatenate(pieces, axis=1), A)
        hi = heads(10 * MIX_W)
        o = _bmm(hq * jnp.exp2(Gm), S) + _bmm(A, hi)
        Gl = Gm[:, lv - 1:lv, :]
        kw = jnp.where(live, kk * jnp.exp2(Gl - Gm), 0.0)
        dcol = jnp.exp2(jnp.swapaxes(jnp.broadcast_to(Gl, (G, 8, HEAD_DIM)), 1, 2)[:, :, 0:1])
        hS_o[...] = (dcol * S + _bmm_tn(kw, hi)).reshape(bb, N_HEADS, HEAD_DIM, HEAD_DIM)
        yield
        put(2 * MIX_W, _rms(o, head_rows(nrm_ref, 2)) * jax.nn.silu(heads(11 * MIX_W)))

    def proj_rest():
        proj(8, 12)
        yield
        proj(4, 8)

    streams = [gdn(), proj_rest(), hgrn(), mlstm()]
    while streams:
        for st in list(streams):
            if next(st, "done") == "done":
                streams.remove(st)


def _mixers(x3, ln, w_main, w_small, gS, gconv, mC, mn, mm, hS, cw, sp, nrm, lb, l, earlier, *, bb, c, lv):
    Bt, Lp, _ = x3.shape
    n_earlier = len(earlier)
    kern = functools.partial(_mix_kernel, bb=bb, c=c, lv=lv, layer=l, n_earlier=n_earlier)
    st4 = pl.BlockSpec((bb, N_HEADS, HEAD_DIM, HEAD_DIM), lambda b, j: (b, 0, 0, 0))
    st3 = pl.BlockSpec((bb, N_HEADS, HEAD_DIM), lambda b, j: (b, 0, 0))
    in4 = pl.BlockSpec((None, bb, N_HEADS, HEAD_DIM, HEAD_DIM), lambda b, j: (l, b, 0, 0, 0))
    in3 = pl.BlockSpec((None, bb, N_HEADS, HEAD_DIM), lambda b, j: (l, b, 0, 0))
    full = lambda a: pl.BlockSpec(a.shape, lambda b, j: (0,) * a.ndim)
    once = dict(pipeline_mode=pl.Buffered(1))
    if n_earlier:
        depth = n_earlier + 1
        out4 = pl.BlockSpec((depth, bb, N_HEADS, HEAD_DIM, HEAD_DIM), lambda b, j: (0, b, 0, 0, 0))
        out3 = pl.BlockSpec((depth, bb, N_HEADS, HEAD_DIM), lambda b, j: (0, b, 0, 0))
        lead = (depth,)
    else:
        out4, out3, lead = st4, st3, ()
    return pl.pallas_call(
        kern,
        grid=(Bt // bb, Lp // c),
        in_specs=[pl.BlockSpec((bb, c, D_MODEL), lambda b, j: (b, j, 0)),
                  full(ln),
                  pl.BlockSpec((None, D_MODEL, N_MIX), lambda b, j: (l, 0, 0), **once),
                  pl.BlockSpec((None, D_MODEL, 128), lambda b, j: (l, 0, 0), **once),
                  in4,
                  pl.BlockSpec((None, bb, CONV_W - 1, 3 * MIX_W), lambda b, j: (l, b, 0, 0)),
                  in4, in3, in3, in4,
                  full(cw), full(sp), full(nrm), full(lb)] + [st4, st4, st3, st3, st4] * n_earlier,
        out_specs=[pl.BlockSpec((bb, c, 3 * MIX_W), lambda b, j: (b, j, 0)),
                   pl.BlockSpec((bb, CONV_W - 1, 3 * MIX_W), lambda b, j: (b, 0, 0)),
                   out4, out4, out3, out3, out4],
        out_shape=[jax.ShapeDtypeStruct((Bt, Lp, 3 * MIX_W), BF16),
                   jax.ShapeDtypeStruct(gconv.shape[1:], F32),
                   jax.ShapeDtypeStruct(lead + gS.shape[1:], F32),
                   jax.ShapeDtypeStruct(lead + mC.shape[1:], F32),
                   jax.ShapeDtypeStruct(lead + mn.shape[1:], F32),
                   jax.ShapeDtypeStruct(lead + mm.shape[1:], F32),
                   jax.ShapeDtypeStruct(lead + hS.shape[1:], F32)],
        scratch_shapes=[pltpu.VMEM((bb, c + 8, 3 * MIX_W), F32),
                        pltpu.VMEM((bb * c, N_MIX), F32),
                        pltpu.VMEM((bb * c, 128), F32)],
        compiler_params=_params(2),
        name="mixers",
    )(x3, ln, w_main, w_small, gS, gconv, mC, mn, mm, hS, cw, sp, nrm, lb, *[a for e in earlier for a in e])


def _chan_kernel(om_ref, x_ref, st_ref, lnm_ref, wg_ref, wbr_ref, wout_ref, lnf_ref, wup_ref, cw_ref, cb_ref,
                 wd_ref, lno_ref, out_ref, ust_ref, *carry, tm, nseq, tiles_per_seq, st_end, final):
    rows = tm // nseq
    x = x_ref[...]
    hm = _rms(x, lnm_ref[...]).astype(BF16)
    acc = None
    for n in range(N_BRANCH):
        br = jnp.dot(om_ref[:, n * MIX_W:(n + 1) * MIX_W], wbr_ref[n], preferred_element_type=F32)
        pg = jnp.dot(hm, wg_ref[:, n * D_MODEL:(n + 1) * D_MODEL], preferred_element_type=F32)
        t = jax.nn.sigmoid(pg) * br
        acc = t if acc is None else acc + t
    x1 = x + jnp.dot(acc.astype(BF16), wout_ref[...], preferred_element_type=F32)
    hf = _rms(x1, lnf_ref[...]).astype(BF16)

    if tiles_per_seq > 1:
        prev_ref, = carry

        @pl.when(pl.program_id(0) % tiles_per_seq == 0)
        def _():
            prev_ref[:, 0:2, :] = st_ref[...]
    else:
        prev_ref = st_ref

    row8 = lax.broadcasted_iota(jnp.int32, (1, 8, 1), 1)

    def up(col0):
        return jnp.dot(hf, wup_ref[:, col0:col0 + FFN_COLS], preferred_element_type=F32)

    def conv(col0, u_raw):
        cs = slice(col0, col0 + FFN_COLS)
        w0, w1, w2, cb = cw_ref[0:1, cs], cw_ref[1:2, cs], cw_ref[2:3, cs], cb_ref[:, cs]
        uc = u_raw.reshape(nseq, rows, FFN_COLS)
        r1 = pltpu.roll(uc, 1, 1)
        r2 = pltpu.roll(uc, 2, 1)
        p0 = prev_ref[:, 0:1, cs]
        p1 = prev_ref[:, 1:2, cs]
        s1 = jnp.where(row8 == 0, p1, r1[:, 0:8])
        s2 = jnp.where(row8 == 0, p0, jnp.where(row8 == 1, p1, r2[:, 0:8]))
        y = w0 * s2 + w1 * s1 + w2 * uc[:, 0:8] + cb
        if rows > 8:
            y = jnp.concatenate([y, w0 * r2[:, 8:] + w1 * r1[:, 8:] + w2 * uc[:, 8:] + cb], axis=1)
        ust_ref[:, :, cs] = uc[:, st_end - 2:st_end, :]
        if tiles_per_seq > 1:
            prev_ref[:, 0:2, cs] = uc[:, rows - 2:rows, :]
        return y.reshape(tm, FFN_COLS)

    n_chunks = D_FF // FFN_COLS
    out = x1
    nxt = up(0), up(D_FF)
    for jc in range(n_chunks):
        cur = nxt
        if jc + 1 < n_chunks:
            nxt = up((jc + 1) * FFN_COLS), up(D_FF + (jc + 1) * FFN_COLS)
        ua = conv(jc * FFN_COLS, cur[0])
        ub = conv(D_FF + jc * FFN_COLS, cur[1])
        act = (jax.nn.silu(ua) * ub).astype(BF16)
        out = out + jnp.dot(act, wd_ref[jc * FFN_COLS:(jc + 1) * FFN_COLS, :], preferred_element_type=F32)
    out_ref[...] = _rms(out, lno_ref[...]) if final else out


def _chan(om, x, st, W, ln_final, l, *, rows_per_seq, st_end, tm, final):
    T = x.shape[0]
    nseq = max(1, tm // rows_per_seq)
    tiles_per_seq = max(1, rows_per_seq // tm)
    kern = functools.partial(_chan_kernel, tm=tm, nseq=nseq, tiles_per_seq=tiles_per_seq,
                             st_end=st_end, final=final)
    once = dict(pipeline_mode=pl.Buffered(1))
    st_spec = pl.BlockSpec((nseq, FFN_CONV_W - 1, 2 * D_FF), lambda i: (i // tiles_per_seq, 0, 0))
    return pl.pallas_call(
        kern,
        grid=(T // tm,),
        in_specs=[pl.BlockSpec((tm, 3 * MIX_W), lambda i: (i, 0)),
                  pl.BlockSpec((tm, D_MODEL), lambda i: (i, 0)),
                  st_spec,
                  pl.BlockSpec((1, D_MODEL), lambda i: (0, 0)),
                  pl.BlockSpec((None, D_MODEL, N_BRANCH * D_MODEL),
                               lambda i: (l, 0, N_MIX // (N_BRANCH * D_MODEL)), **once),
                  pl.BlockSpec((None, N_BRANCH, MIX_W, D_MODEL), lambda i: (l, 0, 0, 0), **once),
                  pl.BlockSpec((None, D_MODEL, D_MODEL), lambda i: (l, 0, 0), **once),
                  pl.BlockSpec((1, D_MODEL), lambda i: (0, 0)),
                  pl.BlockSpec((None, D_MODEL, 2 * D_FF), lambda i: (l, 0, 0), **once),
                  pl.BlockSpec((FFN_CONV_W, 2 * D_FF), lambda i: (0, 0)),
                  pl.BlockSpec((1, 2 * D_FF), lambda i: (0, 0)),
                  pl.BlockSpec((None, D_FF, D_MODEL), lambda i: (l, 0, 0), **once),
                  pl.BlockSpec((1, D_MODEL), lambda i: (0, 0))],
        out_specs=[pl.BlockSpec((tm, D_MODEL), lambda i: (i, 0)), st_spec],
        out_shape=[jax.ShapeDtypeStruct((T, D_MODEL), F32),
                   jax.ShapeDtypeStruct(st.shape, F32)],
        scratch_shapes=[pltpu.VMEM((1, 8, 2 * D_FF), F32)] if tiles_per_seq > 1 else [],
        compiler_params=_params(1),
        name="chan",
    )(om, x, st, W["ln_mix"][l][None], W["w_main"], W["w_br"], W["w_out"], W["ln_ffn"][l][None], W["w_up"],
      W["fcw"][l], W["fcb"][l][None], W["w_down"], ln_final[None])


def _prep_weights(ln_mix, w_in, gdn_conv_w, gdn_A_log, gdn_dt_bias, gdn_norm, m_ibias, m_fbias,
                  m_norm, lb_all, hgrn_norm, w_br, w_out, ln_ffn, w_up, ffn_conv_w, ffn_conv_b, w_down):
    depth = w_in.shape[0]
    w_main = jnp.concatenate([w_in[:, :, 0:2048], w_in[:, :, 2056:4104], w_in[:, :, 4112:]], axis=2).astype(BF16)
    w_small = jnp.concatenate([w_in[:, :, 2048:2056], w_in[:, :, 4104:4112],
                               jnp.zeros((depth, D_MODEL, 128 - 4 * N_HEADS), F32)], axis=2).astype(BF16)
    z4 = jnp.zeros((depth, N_HEADS), F32)
    pad = jnp.zeros((depth, 128 - 4 * N_HEADS), F32)
    sp = jnp.zeros((depth, 8, 128), F32)
    sp = sp.at[:, 0].set(jnp.concatenate([z4, gdn_A_log, z4, z4, pad], axis=1))
    sp = sp.at[:, 1].set(jnp.concatenate([z4, gdn_dt_bias, m_ibias, m_fbias, pad], axis=1))
    return dict(
        ln_mix=ln_mix, w_main=w_main, w_small=w_small, cw=gdn_conv_w, sp=sp,
        nrm=jnp.stack([gdn_norm, m_norm, hgrn_norm], axis=1), lb=lb_all,
        w_br=w_br.astype(BF16), w_out=w_out.astype(BF16), ln_ffn=ln_ffn,
        w_up=w_up.astype(BF16), fcw=ffn_conv_w, fcb=ffn_conv_b, w_down=w_down.astype(BF16))


def _trunk(x3, states, W, ln_final, *, lv, c, bb, bb_gather, sample):
    Bt, Lp, _ = x3.shape
    T = Bt * Lp
    x = x3.reshape(T, D_MODEL)
    gdn_S, gdn_conv, m_C, m_n, m_m, h_S, ffn_conv = states
    depth = W["w_main"].shape[0]
    mm_in = jnp.broadcast_to(m_m[..., None], m_m.shape + (HEAD_DIM,))
    gconv_new, fconv_new, per_layer = [], [], []
    for l in range(depth):
        last = l == depth - 1
        gather = last and bb_gather is not None and depth > 1
        om, gcs, *st = _mixers(
            x.reshape(Bt, Lp, D_MODEL), W["ln_mix"][l][None], W["w_main"], W["w_small"],
            gdn_S, gdn_conv, m_C, m_n, mm_in, h_S,
            W["cw"][l], W["sp"][l], W["nrm"][l], W["lb"][l][None], l, per_layer if gather else (),
            bb=bb_gather if gather else bb, c=c, lv=lv)
        per_layer.append(st)
        tm = 256
        x, ust = _chan(om.reshape(T, 3 * MIX_W), x, ffn_conv[l], W, ln_final, l, rows_per_seq=Lp,
                       st_end=lv if sample else tm, tm=tm, final=last)
        gconv_new.append(gcs)
        fconv_new.append(ust)
    gS, mC, mn, mm, hS = st if gather else [jnp.stack(s) for s in zip(*per_layer)]
    return x.reshape(Bt, Lp, D_MODEL), (gS, jnp.stack(gconv_new), mC, mn, mm[..., 0], hS, jnp.stack(fconv_new))


def kernel(x_prompt, x_sample, state_gdn_S, state_gdn_conv, state_mlstm_C, state_mlstm_n, state_mlstm_m, state_hgrn_S, state_ffn_conv, ln_mix, w_in, gdn_conv_w, gdn_A_log, gdn_dt_bias, gdn_norm, m_ibias, m_fbias, m_norm, hgrn_lb, hgrn_norm, w_br, w_out, ln_ffn, w_up, ffn_conv_w, ffn_conv_b, w_down, ln_final):
    depth = w_in.shape[0]
    lb_all = jnp.cumsum(jax.nn.softmax(hgrn_lb.astype(F32), axis=0), axis=0)
    lb_all = lb_all - lb_all[0]
    layers = _prep_weights(ln_mix, w_in, gdn_conv_w, gdn_A_log, gdn_dt_bias, gdn_norm, m_ibias,
                           m_fbias, m_norm, lb_all, hgrn_norm, w_br, w_out, ln_ffn, w_up, ffn_conv_w,
                           ffn_conv_b, w_down)

    B, L, _ = x_prompt.shape
    H, Dh = N_HEADS, HEAD_DIM
    zeros = (jnp.zeros((depth, B, H, Dh, Dh), F32), jnp.zeros((depth, B, CONV_W - 1, 3 * MIX_W), F32),
             jnp.zeros((depth, B, H, Dh, Dh), F32), jnp.zeros((depth, B, H, Dh), F32),
             jnp.zeros((depth, B, H), F32), jnp.zeros((depth, B, H, Dh, Dh), F32),
             jnp.zeros((depth, B, FFN_CONV_W - 1, 2 * D_FF), F32))
    y_p, st_p = _trunk(x_prompt, zeros, layers, ln_final, lv=CHUNK, c=CHUNK, bb=4, bb_gather=None, sample=False)

    Bs, Ls, _ = x_sample.shape
    xs = jnp.pad(x_sample, ((0, 0), (0, SAMPLE_PAD - Ls), (0, 0)))
    st_in = (state_gdn_S, state_gdn_conv, state_mlstm_C, state_mlstm_n, state_mlstm_m, state_hgrn_S,
             state_ffn_conv)
    y_s, st_s = _trunk(xs, st_in, layers, ln_final, lv=Ls, c=SAMPLE_PAD, bb=8, bb_gather=4, sample=True)
    return (y_p, y_s[:, :Ls]) + st_p + st_s
```

```python
import functools
import math

import jax
import jax.numpy as jnp
from jax import lax
from jax.experimental import pallas as pl
from jax.experimental.pallas import tpu as pltpu

D_MODEL = 1024
N_HEADS = 4
HEAD_DIM = 128
MIX_W = N_HEADS * HEAD_DIM
N_BRANCH = 3
CONV_W = 4
FFN_CONV_W = 3
D_FF = 2816
CHUNK = 64
EPS = 1e-6
N_MIX = 6144
SAMPLE_PAD = 8
SUB = 8
FFN_COLS = 2816
LOG2E = 1.4426950408889634
LN2 = 0.6931471805599453

F32 = jnp.float32
BF16 = jnp.bfloat16
VMEM_LIMIT = 56 * 1024 * 1024


def _bmm(a, b):
    return jnp.einsum('gtd,gde->gte', a.astype(BF16), b.astype(BF16), preferred_element_type=F32)


def _bmm_nt(a, b):
    return jnp.einsum('gtd,gsd->gts', a.astype(BF16), b.astype(BF16), preferred_element_type=F32)


def _bmm_tn(a, b):
    return jnp.einsum('gsd,gse->gde', a.astype(BF16), b.astype(BF16), preferred_element_type=F32)


def _split2(a):
    hi = a.astype(BF16)
    return hi, (a - hi.astype(F32)).astype(BF16)


def _bmm_split(a, b):
    a_hi, a_lo = a
    b_hi, b_lo = b
    n = a_hi.shape[1]
    p = jnp.einsum('gtk,gks->gts', jnp.concatenate([a_hi, a_lo], axis=1), b_hi, preferred_element_type=F32)
    if b_lo is None:
        return p[:, :n] + p[:, n:]
    return p[:, :n] + p[:, n:] + jnp.einsum('gtk,gks->gts', a_hi, b_lo, preferred_element_type=F32)


def _cumsum_rows(tri_bf, x):
    n = x.shape[1]
    x1 = x.astype(BF16)
    r1 = x - x1.astype(F32)
    x2 = r1.astype(BF16)
    x3 = (r1 - x2.astype(F32)).astype(BF16)
    p = jnp.dot(tri_bf, jnp.concatenate([x1, x2, x3], axis=1), preferred_element_type=F32)
    return p[:, 0:n] + p[:, n:2 * n] + p[:, 2 * n:3 * n]


def _rms(x, g):
    return x * lax.rsqrt(jnp.mean(x * x, -1, keepdims=True) + EPS) * g


def _params(n_axes):
    return pltpu.CompilerParams(dimension_semantics=("arbitrary",) * n_axes,
                                vmem_limit_bytes=VMEM_LIMIT)


def _mix_kernel(x_ref, ln_ref, wm_ref, ws_ref, gS_ref, gconv_ref, mC_ref, mn_ref, mm_ref, hS_ref,
                cw_ref, sp_ref, nrm_ref, lb_ref, *rest, bb, c, lv, layer, n_earlier):
    earlier = [rest[5 * d:5 * d + 5] for d in range(n_earlier)]
    om_ref, gcs_o, *st_o = rest[5 * n_earlier:5 * n_earlier + 7]
    xpad_ref, pm_ref, ps_ref = rest[5 * n_earlier + 7:]
    gS_o, mC_o, mn_o, mm_o, hS_o = [r.at[layer] for r in st_o] if n_earlier else st_o
    G = bb * N_HEADS
    gh = [(bi, h) for bi in range(bb) for h in range(N_HEADS)]
    rows = lambda bi: slice(bi * c, (bi + 1) * c)

    @pl.when(pl.program_id(1) == 0)
    def _():
        gS_o[...] = gS_ref[...]
        mC_o[...] = mC_ref[...]
        mn_o[...] = mn_ref[...]
        mm_o[...] = mm_ref[...]
        hS_o[...] = hS_ref[...]
        xpad_ref[:, 5:8, :] = gconv_ref[...]
        for d, refs in enumerate(earlier):
            for dst, src in zip(st_o, refs):
                dst[d] = src[...]

    hb = _rms(x_ref[...].reshape(bb * c, D_MODEL), ln_ref[...]).astype(BF16)

    def proj(g0, g1):
        cs = slice(g0 * MIX_W, g1 * MIX_W)
        pm_ref[:, cs] = jnp.dot(hb, wm_ref[:, cs], preferred_element_type=F32)

    ps_ref[...] = jnp.dot(hb, ws_ref[...], preferred_element_type=F32)
    proj(0, 4)
    for bi in range(bb):
        gcs_o[bi] = pm_ref[bi * c + lv - (CONV_W - 1):bi * c + lv, 0:3 * MIX_W]

    r2 = lax.broadcasted_iota(jnp.int32, (c, c), 0)
    s2 = lax.broadcasted_iota(jnp.int32, (c, c), 1)
    causal = r2 >= s2
    strict = r2 > s2
    tri_bf = causal.astype(BF16)
    eye = (r2 == s2).astype(F32)
    rowc = lax.broadcasted_iota(jnp.int32, (c, 1), 0)
    live = rowc < lv
    lane = lax.broadcasted_iota(jnp.int32, (c, 128), 1)
    sub = min(SUB, c)
    nblk = c // sub
    n_sq = int(math.log2(c)) - 1

    def heads(col0):
        return jnp.stack([pm_ref[rows(bi), col0 + h * HEAD_DIM:col0 + (h + 1) * HEAD_DIM] for bi, h in gh])

    def head_rows(ref, r):
        return jnp.stack([ref[r:r + 1, h * HEAD_DIM:(h + 1) * HEAD_DIM] for _, h in gh])

    def put(col0, val):
        for g, (bi, h) in enumerate(gh):
            om_ref[bi, :, col0 + h * HEAD_DIM:col0 + (h + 1) * HEAD_DIM] = val[g].astype(om_ref.dtype)

    alog = sp_ref[0:1, :]
    bias = sp_ref[1:2, :]
    gates, gates2, cums2, gates2_t, cums2_t, convs = [], [], [], [], [], []
    for bi in range(bb):
        ps = ps_ref[rows(bi), :]
        z = ps + bias
        gt = jnp.where(lane < 4, jax.nn.sigmoid(ps),
                       jnp.where(lane < 8, -jnp.exp(alog) * jax.nn.softplus(z),
                                 jnp.where(lane < 12, z, jax.nn.log_sigmoid(z))))
        gt2 = gt * LOG2E
        cm2 = _cumsum_rows(tri_bf, gt) * LOG2E
        gates.append(gt)
        gates2.append(gt2)
        cums2.append(cm2)
        gates2_t.append(gt2.T)
        cums2_t.append(cm2.T)
        xpad_ref[bi, 8:8 + c, :] = pm_ref[rows(bi), 0:3 * MIX_W]
        acc = cw_ref[0:1, :] * xpad_ref[bi, 5:5 + c, :]
        for j in range(1, CONV_W):
            acc = acc + cw_ref[j:j + 1, :] * xpad_ref[bi, 5 + j:5 + j + c, :]
        convs.append(jax.nn.silu(acc))
        xpad_ref[bi, 5:8, :] = xpad_ref[bi, 5 + c:8 + c, :]

    col = lambda arrs, j: jnp.stack([arrs[bi][:, j + h:j + h + 1] for bi, h in gh])
    row = lambda arrs, j: jnp.stack([arrs[bi][j + h:j + h + 1, :] for bi, h in gh])
    part = lambda p: jnp.stack([convs[bi][:, p * MIX_W + h * HEAD_DIM:p * MIX_W + (h + 1) * HEAD_DIM]
                                for bi, h in gh])


    def gdn():
        q = part(0)
        k = part(1)
        v = part(2)
        q = q * lax.rsqrt(jnp.sum(q * q, -1, keepdims=True) + EPS) * HEAD_DIM ** -0.5
        k = k * lax.rsqrt(jnp.sum(k * k, -1, keepdims=True) + EPS)
        beta = col(gates, 0)
        Gc = col(cums2, 4)
        Gr = row(cums2_t, 4)
        decay = jnp.exp2(jnp.where(causal, Gc - Gr, -jnp.inf))
        S = gS_o[...].reshape(G, HEAD_DIM, HEAD_DIM)
        qk2 = jnp.concatenate([q, k], axis=1)
        P = _bmm_nt(qk2, k)
        R = _bmm(qk2, S)
        QK, KK = P[:, :c], P[:, c:]
        QS, KS = R[:, :c], R[:, c:]
        eG = jnp.exp2(Gc)
        rhs = beta * (v - eG * KS)
        Mp = -jnp.where(strict, beta * decay * KK, 0.0)
        Tinv = eye + Mp
        Ms = _split2(Mp)
        yield
        for _ in range(n_sq):
            Mp = _bmm_split(Ms, (Ms[0], None))
            Ms = _split2(Mp)
            Tinv = Tinv + _bmm_split(_split2(Tinv), (Ms[0], None))
            yield
        u = _bmm_split(_split2(Tinv), _split2(rhs))
        yield
        o = eG * QS + _bmm(QK * decay, u)
        Gl = Gc[:, lv - 1:lv, :]
        wl = jnp.where(live, jnp.exp2(Gl - Gc), 0.0)
        gS_o[...] = (jnp.exp2(Gl) * S + _bmm_tn(k * wl, u)).reshape(bb, N_HEADS, HEAD_DIM, HEAD_DIM)
        put(0, _rms(o, head_rows(nrm_ref, 0)) * jax.nn.silu(heads(3 * MIX_W)))

    def mlstm():
        yield
        q = heads(4 * MIX_W)
        k = heads(5 * MIX_W) * HEAD_DIM ** -0.5
        ig_c = col(gates2, 8)
        ig_r = row(gates2_t, 8)
        Fc = col(cums2, 12)
        Fr = row(cums2_t, 12)
        m0 = jnp.stack([mm_o[bi, h:h + 1, 0:1] for bi, h in gh])
        n_row = jnp.stack([mn_o[bi, h:h + 1, :] for bi, h in gh])
        C = mC_o[...].reshape(G, HEAD_DIM, HEAD_DIM)
        logD = jnp.where(causal, Fc - Fr + ig_r, -jnp.inf)
        m02 = m0 * LOG2E
        b = Fc + m02
        mt = jnp.maximum(b, jnp.max(logD, -1, keepdims=True))
        s = _bmm_nt(q, k) * jnp.exp2(logD - mt)
        inter = jnp.exp2(b - mt)
        yield
        v = heads(6 * MIX_W)
        num = _bmm(s, v) + inter * _bmm(q, C)
        den = jnp.sum(s, -1, keepdims=True) + inter * jnp.sum(q * n_row, -1, keepdims=True)
        hm = num / jnp.maximum(jnp.abs(den), jnp.exp2(-mt))
        yield
        mt_l = mt[:, lv - 1:lv, :]
        Fl = Fc[:, lv - 1:lv, :]
        wl = jnp.where(live, jnp.exp2(Fl - Fc + ig_c - mt_l), 0.0)
        d0 = jnp.exp2(Fl + m02 - mt_l)
        kw = k * wl
        mC_o[...] = (d0 * C + _bmm_tn(kw, v)).reshape(bb, N_HEADS, HEAD_DIM, HEAD_DIM)
        n_new = d0 * n_row + jnp.sum(kw, axis=1, keepdims=True)
        m_b = jnp.broadcast_to(mt_l * LN2, (G, 1, HEAD_DIM))
        for g, (bi, h) in enumerate(gh):
            mn_o[bi, h:h + 1, :] = n_new[g]
            mm_o[bi, h:h + 1, :] = m_b[g]
        yield
        put(MIX_W, _rms(hm, head_rows(nrm_ref, 1)) * jax.nn.sigmoid(heads(7 * MIX_W)))

    def hgrn():
        hq = jax.nn.silu(heads(8 * MIX_W))
        lbh = lb_ref[...]
        Gs_l, kk_l = [], []
        for bi in range(bb):
            fg = lbh + (1.0 - lbh) * jax.nn.sigmoid(pm_ref[rows(bi), 9 * MIX_W:10 * MIX_W])
            kk_l.append(1.0 - fg)
            Gs_l.append(_cumsum_rows(tri_bf, jnp.log2(fg)))
        kk = jnp.stack([kk_l[bi][:, h * HEAD_DIM:(h + 1) * HEAD_DIM] for bi, h in gh])
        Gm = jnp.stack([Gs_l[bi][:, h * HEAD_DIM:(h + 1) * HEAD_DIM] for bi, h in gh])
        S = hS_o[...].reshape(G, HEAD_DIM, HEAD_DIM)
        yield
        tiles = lambda a: a.reshape(G * nblk, sub, HEAD_DIM)
        hq_t, kk_t, Gm_t = tiles(hq), tiles(kk), tiles(Gm)
        A = jnp.zeros((G, c, c), F32)
        for dlt in range(sub):
            ks = kk_t if dlt == 0 else pltpu.roll(kk_t, dlt, 1)
            Gs = Gm_t if dlt == 0 else pltpu.roll(Gm_t, dlt, 1)
            d = jnp.sum(hq_t * ks * jnp.exp2(jnp.minimum(Gm_t - Gs, 0.0)), -1, keepdims=True)
            A = jnp.where((s2 == r2 - dlt) & (jnp.bitwise_and(r2, sub - 1) >= dlt), d.reshape(G, c, 1), A)
            if dlt % 2 == 1:
                yield
        if nblk > 1:
            Gref = jnp.concatenate(
                [jnp.broadcast_to(Gm[:, I * sub:I * sub + 1, :], (G, sub, HEAD_DIM)) for I in range(nblk)], axis=1)
            qt = hq * jnp.exp2(Gm - Gref)
            pieces = [jnp.zeros((G, sub, c), F32)]
            for I in range(1, nblk):
                kt = kk * jnp.exp2(jnp.minimum(Gm[:, I * sub:I * sub + 1, :] - Gm, 0.0))
                pieces.append(_bmm_nt(qt[:, I * sub:(I + 1) * sub], kt))
                if I % 2 == 1:
                    yield
            A = jnp.where(s2 < jnp.bitwise_and(r2, -sub), jnp.concatenate(pieces, axis=1), A)
        hi = heads(10 * MIX_W)
        o = _bmm(hq * jnp.exp2(Gm), S) + _bmm(A, hi)
        Gl = Gm[:, lv - 1:lv, :]
        kw = jnp.where(live, kk * jnp.exp2(Gl - Gm), 0.0)
        dcol = jnp.exp2(jnp.swapaxes(jnp.broadcast_to(Gl, (G, 8, HEAD_DIM)), 1, 2)[:, :, 0:1])
        hS_o[...] = (dcol * S + _bmm_tn(kw, hi)).reshape(bb, N_HEADS, HEAD_DIM, HEAD_DIM)
        yield
        put(2 * MIX_W, _rms(o, head_rows(nrm_ref, 2)) * jax.nn.silu(heads(11 * MIX_W)))

    def proj_rest():
        proj(8, 12)
        yield
        proj(4, 8)

    streams = [gdn(), proj_rest(), hgrn(), mlstm()]
    while streams:
        for st in list(streams):
            if next(st, "done") == "done":
                streams.remove(st)


def _mixers(x3, ln, w_main, w_small, gS, gconv, mC, mn, mm, hS, cw, sp, nrm, lb, l, earlier, *, bb, c, lv):
    Bt, Lp, _ = x3.shape
    n_earlier = len(earlier)
    kern = functools.partial(_mix_kernel, bb=bb, c=c, lv=lv, layer=l, n_earlier=n_earlier)
    st4 = pl.BlockSpec((bb, N_HEADS, HEAD_DIM, HEAD_DIM), lambda b, j: (b, 0, 0, 0))
    st3 = pl.BlockSpec((bb, N_HEADS, HEAD_DIM), lambda b, j: (b, 0, 0))
    in4 = pl.BlockSpec((None, bb, N_HEADS, HEAD_DIM, HEAD_DIM), lambda b, j: (l, b, 0, 0, 0))
    in3 = pl.BlockSpec((None, bb, N_HEADS, HEAD_DIM), lambda b, j: (l, b, 0, 0))
    full = lambda a: pl.BlockSpec(a.shape, lambda b, j: (0,) * a.ndim)
    once = dict(pipeline_mode=pl.Buffered(1))
    if n_earlier:
        depth = n_earlier + 1
        out4 = pl.BlockSpec((depth, bb, N_HEADS, HEAD_DIM, HEAD_DIM), lambda b, j: (0, b, 0, 0, 0))
        out3 = pl.BlockSpec((depth, bb, N_HEADS, HEAD_DIM), lambda b, j: (0, b, 0, 0))
        lead = (depth,)
    else:
        out4, out3, lead = st4, st3, ()
    return pl.pallas_call(
        kern,
        grid=(Bt // bb, Lp // c),
        in_specs=[pl.BlockSpec((bb, c, D_MODEL), lambda b, j: (b, j, 0)),
                  full(ln),
                  pl.BlockSpec((None, D_MODEL, N_MIX), lambda b, j: (l, 0, 0), **once),
                  pl.BlockSpec((None, D_MODEL, 128), lambda b, j: (l, 0, 0), **once),
                  in4,
                  pl.BlockSpec((None, bb, CONV_W - 1, 3 * MIX_W), lambda b, j: (l, b, 0, 0)),
                  in4, in3, in3, in4,
                  full(cw), full(sp), full(nrm), full(lb)] + [st4, st4, st3, st3, st4] * n_earlier,
        out_specs=[pl.BlockSpec((bb, c, 3 * MIX_W), lambda b, j: (b, j, 0)),
                   pl.BlockSpec((bb, CONV_W - 1, 3 * MIX_W), lambda b, j: (b, 0, 0)),
                   out4, out4, out3, out3, out4],
        out_shape=[jax.ShapeDtypeStruct((Bt, Lp, 3 * MIX_W), BF16),
                   jax.ShapeDtypeStruct(gconv.shape[1:], F32),
                   jax.ShapeDtypeStruct(lead + gS.shape[1:], F32),
                   jax.ShapeDtypeStruct(lead + mC.shape[1:], F32),
                   jax.ShapeDtypeStruct(lead + mn.shape[1:], F32),
                   jax.ShapeDtypeStruct(lead + mm.shape[1:], F32),
                   jax.ShapeDtypeStruct(lead + hS.shape[1:], F32)],
        scratch_shapes=[pltpu.VMEM((bb, c + 8, 3 * MIX_W), F32),
                        pltpu.VMEM((bb * c, N_MIX), F32),
                        pltpu.VMEM((bb * c, 128), F32)],
        compiler_params=_params(2),
        name="mixers",
    )(x3, ln, w_main, w_small, gS, gconv, mC, mn, mm, hS, cw, sp, nrm, lb, *[a for e in earlier for a in e])


def _chan_kernel(om_ref, x_ref, st_ref, lnm_ref, wg_ref, wbr_ref, wout_ref, lnf_ref, wup_ref, cw_ref, cb_ref,
                 wd_ref, lno_ref, out_ref, ust_ref, *carry, tm, nseq, tiles_per_seq, st_end, final):
    rows = tm // nseq
    x = x_ref[...]
    hm = _rms(x, lnm_ref[...]).astype(BF16)
    acc = None
    for n in range(N_BRANCH):
        br = jnp.dot(om_ref[:, n * MIX_W:(n + 1) * MIX_W], wbr_ref[n], preferred_element_type=F32)
        pg = jnp.dot(hm, wg_ref[:, n * D_MODEL:(n + 1) * D_MODEL], preferred_element_type=F32)
        t = jax.nn.sigmoid(pg) * br
        acc = t if acc is None else acc + t
    x1 = x + jnp.dot(acc.astype(BF16), wout_ref[...], preferred_element_type=F32)
    hf = _rms(x1, lnf_ref[...]).astype(BF16)

    if tiles_per_seq > 1:
        prev_ref, = carry

        @pl.when(pl.program_id(0) % tiles_per_seq == 0)
        def _():
            prev_ref[:, 0:2, :] = st_ref[...]
    else:
        prev_ref = st_ref

    row8 = lax.broadcasted_iota(jnp.int32, (1, 8, 1), 1)

    def up(col0):
        return jnp.dot(hf, wup_ref[:, col0:col0 + FFN_COLS], preferred_element_type=F32)

    def conv(col0, u_raw):
        cs = slice(col0, col0 + FFN_COLS)
        w0, w1, w2, cb = cw_ref[0:1, cs], cw_ref[1:2, cs], cw_ref[2:3, cs], cb_ref[:, cs]
        uc = u_raw.reshape(nseq, rows, FFN_COLS)
        r1 = pltpu.roll(uc, 1, 1)
        r2 = pltpu.roll(uc, 2, 1)
        p0 = prev_ref[:, 0:1, cs]
        p1 = prev_ref[:, 1:2, cs]
        s1 = jnp.where(row8 == 0, p1, r1[:, 0:8])
        s2 = jnp.where(row8 == 0, p0, jnp.where(row8 == 1, p1, r2[:, 0:8]))
        y = w0 * s2 + w1 * s1 + w2 * uc[:, 0:8] + cb
        if rows > 8:
            y = jnp.concatenate([y, w0 * r2[:, 8:] + w1 * r1[:, 8:] + w2 * uc[:, 8:] + cb], axis=1)
        ust_ref[:, :, cs] = uc[:, st_end - 2:st_end, :]
        if tiles_per_seq > 1:
            prev_ref[:, 0:2, cs] = uc[:, rows - 2:rows, :]
        return y.reshape(tm, FFN_COLS)

    n_chunks = D_FF // FFN_COLS
    out = x1
    nxt = up(0), up(D_FF)
    for jc in range(n_chunks):
        cur = nxt
        if jc + 1 < n_chunks:
            nxt = up((jc + 1) * FFN_COLS), up(D_FF + (jc + 1) * FFN_COLS)
        ua = conv(jc * FFN_COLS, cur[0])
        ub = conv(D_FF + jc * FFN_COLS, cur[1])
        act = (jax.nn.silu(ua) * ub).astype(BF16)
        out = out + jnp.dot(act, wd_ref[jc * FFN_COLS:(jc + 1) * FFN_COLS, :], preferred_element_type=F32)
    out_ref[...] = _rms(out, lno_ref[...]) if final else out


def _chan(om, x, st, W, ln_final, l, *, rows_per_seq, st_end, tm, final):
    T = x.shape[0]
    nseq = max(1, tm // rows_per_seq)
    tiles_per_seq = max(1, rows_per_seq // tm)
    kern = functools.partial(_chan_kernel, tm=tm, nseq=nseq, tiles_per_seq=tiles_per_seq,
                             st_end=st_end, final=final)
    once = dict(pipeline_mode=pl.Buffered(1))
    st_spec = pl.BlockSpec((nseq, FFN_CONV_W - 1, 2 * D_FF), lambda i: (i // tiles_per_seq, 0, 0))
    return pl.pallas_call(
        kern,
        grid=(T // tm,),
        in_specs=[pl.BlockSpec((tm, 3 * MIX_W), lambda i: (i, 0)),
                  pl.BlockSpec((tm, D_MODEL), lambda i: (i, 0)),
                  st_spec,
                  pl.BlockSpec((1, D_MODEL), lambda i: (0, 0)),
                  pl.BlockSpec((None, D_MODEL, N_BRANCH * D_MODEL),
                               lambda i: (l, 0, N_MIX // (N_BRANCH * D_MODEL)), **once),
                  pl.BlockSpec((None, N_BRANCH, MIX_W, D_MODEL), lambda i: (l, 0, 0, 0), **once),
                  pl.BlockSpec((None, D_MODEL, D_MODEL), lambda i: (l, 0, 0), **once),
                  pl.BlockSpec((1, D_MODEL), lambda i: (0, 0)),
                  pl.BlockSpec((None, D_MODEL, 2 * D_FF), lambda i: (l, 0, 0), **once),
                  pl.BlockSpec((FFN_CONV_W, 2 * D_FF), lambda i: (0, 0)),
                  pl.BlockSpec((1, 2 * D_FF), lambda i: (0, 0)),
                  pl.BlockSpec((None, D_FF, D_MODEL), lambda i: (l, 0, 0), **once),
                  pl.BlockSpec((1, D_MODEL), lambda i: (0, 0))],
        out_specs=[pl.BlockSpec((tm, D_MODEL), lambda i: (i, 0)), st_spec],
        out_shape=[jax.ShapeDtypeStruct((T, D_MODEL), F32),
                   jax.ShapeDtypeStruct(st.shape, F32)],
        scratch_shapes=[pltpu.VMEM((1, 8, 2 * D_FF), F32)] if tiles_per_seq > 1 else [],
        compiler_params=_params(1),
        name="chan",
    )(om, x, st, W["ln_mix"][l][None], W["w_main"], W["w_br"], W["w_out"], W["ln_ffn"][l][None], W["w_up"],
      W["fcw"][l], W["fcb"][l][None], W["w_down"], ln_final[None])


def _prep_weights(ln_mix, w_in, gdn_conv_w, gdn_A_log, gdn_dt_bias, gdn_norm, m_ibias, m_fbias,
                  m_norm, lb_all, hgrn_norm, w_br, w_out, ln_ffn, w_up, ffn_conv_w, ffn_conv_b, w_down):
    depth = w_in.shape[0]
    w_main = jnp.concatenate([w_in[:, :, 0:2048], w_in[:, :, 2056:4104], w_in[:, :, 4112:]], axis=2).astype(BF16)
    w_small = jnp.concatenate([w_in[:, :, 2048:2056], w_in[:, :, 4104:4112],
                               jnp.zeros((depth, D_MODEL, 128 - 4 * N_HEADS), F32)], axis=2).astype(BF16)
    z4 = jnp.zeros((depth, N_HEADS), F32)
    pad = jnp.zeros((depth, 128 - 4 * N_HEADS), F32)
    sp = jnp.zeros((depth, 8, 128), F32)
    sp = sp.at[:, 0].set(jnp.concatenate([z4, gdn_A_log, z4, z4, pad], axis=1))
    sp = sp.at[:, 1].set(jnp.concatenate([z4, gdn_dt_bias, m_ibias, m_fbias, pad], axis=1))
    return dict(
        ln_mix=ln_mix, w_main=w_main, w_small=w_small, cw=gdn_conv_w, sp=sp,
        nrm=jnp.stack([gdn_norm, m_norm, hgrn_norm], axis=1), lb=lb_all,
        w_br=w_br.astype(BF16), w_out=w_out.astype(BF16), ln_ffn=ln_ffn,
        w_up=w_up.astype(BF16), fcw=ffn_conv_w, fcb=ffn_conv_b, w_down=w_down.astype(BF16))


def _trunk(x3, states, W, ln_final, *, lv, c, bb, bb_gather, sample):
    Bt, Lp, _ = x3.shape
    T = Bt * Lp
    x = x3.reshape(T, D_MODEL)
    gdn_S, gdn_conv, m_C, m_n, m_m, h_S, ffn_conv = states
    depth = W["w_main"].shape[0]
    mm_in = jnp.broadcast_to(m_m[..., None], m_m.shape + (HEAD_DIM,))
    gconv_new, fconv_new, per_layer = [], [], []
    for l in range(depth):
        last = l == depth - 1
        gather = last and bb_gather is not None and depth > 1
        om, gcs, *st = _mixers(
            x.reshape(Bt, Lp, D_MODEL), W["ln_mix"][l][None], W["w_main"], W["w_small"],
            gdn_S, gdn_conv, m_C, m_n, mm_in, h_S,
            W["cw"][l], W["sp"][l], W["nrm"][l], W["lb"][l][None], l, per_layer if gather else (),
            bb=bb_gather if gather else bb, c=c, lv=lv)
        per_layer.append(st)
        tm = 256
        x, ust = _chan(om.reshape(T, 3 * MIX_W), x, ffn_conv[l], W, ln_final, l, rows_per_seq=Lp,
                       st_end=lv if sample else tm, tm=tm, final=last)
        gconv_new.append(gcs)
        fconv_new.append(ust)
    gS, mC, mn, mm, hS = st if gather else [jnp.stack(s) for s in zip(*per_layer)]
    return x.reshape(Bt, Lp, D_MODEL), (gS, jnp.stack(gconv_new), mC, mn, mm[..., 0], hS, jnp.stack(fconv_new))


def kernel(x_prompt, x_sample, state_gdn_S, state_gdn_conv, state_mlstm_C, state_mlstm_n, state_mlstm_m, state_hgrn_S, state_ffn_conv, ln_mix, w_in, gdn_conv_w, gdn_A_log, gdn_dt_bias, gdn_norm, m_ibias, m_fbias, m_norm, hgrn_lb, hgrn_norm, w_br, w_out, ln_ffn, w_up, ffn_conv_w, ffn_conv_b, w_down, ln_final):
    depth = w_in.shape[0]
    lb_all = jnp.cumsum(jax.nn.softmax(hgrn_lb.astype(F32), axis=0), axis=0)
    lb_all = lb_all - lb_all[0]
    layers = _prep_weights(ln_mix, w_in, gdn_conv_w, gdn_A_log, gdn_dt_bias, gdn_norm, m_ibias,
                           m_fbias, m_norm, lb_all, hgrn_norm, w_br, w_out, ln_ffn, w_up, ffn_conv_w,
                           ffn_conv_b, w_down)

    B, L, _ = x_prompt.shape
    H, Dh = N_HEADS, HEAD_DIM
    zeros = (jnp.zeros((depth, B, H, Dh, Dh), F32), jnp.zeros((depth, B, CONV_W - 1, 3 * MIX_W), F32),
             jnp.zeros((depth, B, H, Dh, Dh), F32), jnp.zeros((depth, B, H, Dh), F32),
             jnp.zeros((depth, B, H), F32), jnp.zeros((depth, B, H, Dh, Dh), F32),
             jnp.zeros((depth, B, FFN_CONV_W - 1, 2 * D_FF), F32))
    y_p, st_p = _trunk(x_prompt, zeros, layers, ln_final, lv=CHUNK, c=CHUNK, bb=4, bb_gather=None, sample=False)

    Bs, Ls, _ = x_sample.shape
    xs = jnp.pad(x_sample, ((0, 0), (0, SAMPLE_PAD - Ls), (0, 0)))
    st_in = (state_gdn_S, state_gdn_conv, state_mlstm_C, state_mlstm_n, state_mlstm_m, state_hgrn_S,
             state_ffn_conv)
    y_s, st_s = _trunk(xs, st_in, layers, ln_final, lv=Ls, c=SAMPLE_PAD, bb=8, bb_gather=4, sample=True)
    return (y_p, y_s[:, :Ls]) + st_p + st_s
```

```python
import functools
import math

import jax
import jax.numpy as jnp
from jax import lax
from jax.experimental import pallas as pl
from jax.experimental.pallas import tpu as pltpu

D_MODEL = 1024
N_HEADS = 4
HEAD_DIM = 128
MIX_W = N_HEADS * HEAD_DIM
N_BRANCH = 3
CONV_W = 4
FFN_CONV_W = 3
D_FF = 2816
CHUNK = 64
EPS = 1e-6
N_MIX = 6144
SAMPLE_PAD = 8
SUB = 8
FFN_COLS = 2816
LOG2E = 1.4426950408889634
LN2 = 0.6931471805599453

F32 = jnp.float32
BF16 = jnp.bfloat16
VMEM_LIMIT = 56 * 1024 * 1024


def _bmm(a, b):
    return jnp.einsum('gtd,gde->gte', a.astype(BF16), b.astype(BF16), preferred_element_type=F32)


def _bmm_nt(a, b):
    return jnp.einsum('gtd,gsd->gts', a.astype(BF16), b.astype(BF16), preferred_element_type=F32)


def _bmm_tn(a, b):
    return jnp.einsum('gsd,gse->gde', a.astype(BF16), b.astype(BF16), preferred_element_type=F32)


def _split2(a):
    hi = a.astype(BF16)
    return hi, (a - hi.astype(F32)).astype(BF16)


def _bmm_split(a, b):
    a_hi, a_lo = a
    b_hi, b_lo = b
    n = a_hi.shape[1]
    p = jnp.einsum('gtk,gks->gts', jnp.concatenate([a_hi, a_lo], axis=1), b_hi, preferred_element_type=F32)
    if b_lo is None:
        return p[:, :n] + p[:, n:]
    return p[:, :n] + p[:, n:] + jnp.einsum('gtk,gks->gts', a_hi, b_lo, preferred_element_type=F32)


def _cumsum_rows(tri_bf, x):
    n = x.shape[1]
    x1 = x.astype(BF16)
    r1 = x - x1.astype(F32)
    x2 = r1.astype(BF16)
    x3 = (r1 - x2.astype(F32)).astype(BF16)
    p = jnp.dot(tri_bf, jnp.concatenate([x1, x2, x3], axis=1), preferred_element_type=F32)
    return p[:, 0:n] + p[:, n:2 * n] + p[:, 2 * n:3 * n]


def _rms(x, g):
    return x * lax.rsqrt(jnp.mean(x * x, -1, keepdims=True) + EPS) * g


def _params(n_axes):
    return pltpu.CompilerParams(dimension_semantics=("arbitrary",) * n_axes,
                                vmem_limit_bytes=VMEM_LIMIT)


def _mix_kernel(x_ref, ln_ref, wm_ref, ws_ref, gS_ref, gconv_ref, mC_ref, mn_ref, mm_ref, hS_ref,
                cw_ref, sp_ref, nrm_ref, lb_ref, *rest, bb, c, lv, layer, n_earlier):
    earlier = [rest[5 * d:5 * d + 5] for d in range(n_earlier)]
    om_ref, gcs_o, *st_o = rest[5 * n_earlier:5 * n_earlier + 7]
    xpad_ref, pm_ref, ps_ref = rest[5 * n_earlier + 7:]
    gS_o, mC_o, mn_o, mm_o, hS_o = [r.at[layer] for r in st_o] if n_earlier else st_o
    G = bb * N_HEADS
    gh = [(bi, h) for bi in range(bb) for h in range(N_HEADS)]
    rows = lambda bi: slice(bi * c, (bi + 1) * c)

    @pl.when(pl.program_id(1) == 0)
    def _():
        gS_o[...] = gS_ref[...]
        mC_o[...] = mC_ref[...]
        mn_o[...] = mn_ref[...]
        mm_o[...] = mm_ref[...]
        hS_o[...] = hS_ref[...]
        xpad_ref[:, 5:8, :] = gconv_ref[...]
        for d, refs in enumerate(earlier):
            for dst, src in zip(st_o, refs):
                dst[d] = src[...]

    hb = _rms(x_ref[...].reshape(bb * c, D_MODEL), ln_ref[...]).astype(BF16)

    def proj(g0, g1):
        cs = slice(g0 * MIX_W, g1 * MIX_W)
        pm_ref[:, cs] = jnp.dot(hb, wm_ref[:, cs], preferred_element_type=F32)

    ps_ref[...] = jnp.dot(hb, ws_ref[...], preferred_element_type=F32)
    proj(0, 4)
    for bi in range(bb):
        gcs_o[bi] = pm_ref[bi * c + lv - (CONV_W - 1):bi * c + lv, 0:3 * MIX_W]

    r2 = lax.broadcasted_iota(jnp.int32, (c, c), 0)
    s2 = lax.broadcasted_iota(jnp.int32, (c, c), 1)
    causal = r2 >= s2
    strict = r2 > s2
    tri_bf = causal.astype(BF16)
    eye = (r2 == s2).astype(F32)
    rowc = lax.broadcasted_iota(jnp.int32, (c, 1), 0)
    live = rowc < lv
    lane = lax.broadcasted_iota(jnp.int32, (c, 128), 1)
    sub = min(SUB, c)
    nblk = c // sub
    n_sq = int(math.log2(c)) - 1

    def heads(col0):
        return jnp.stack([pm_ref[rows(bi), col0 + h * HEAD_DIM:col0 + (h + 1) * HEAD_DIM] for bi, h in gh])

    def head_rows(ref, r):
        return jnp.stack([ref[r:r + 1, h * HEAD_DIM:(h + 1) * HEAD_DIM] for _, h in gh])

    def put(col0, val):
        for g, (bi, h) in enumerate(gh):
            om_ref[bi, :, col0 + h * HEAD_DIM:col0 + (h + 1) * HEAD_DIM] = val[g].astype(om_ref.dtype)

    alog = sp_ref[0:1, :]
    bias = sp_ref[1:2, :]
    gates, gates2, cums2, gates2_t, cums2_t, convs = [], [], [], [], [], []
    for bi in range(bb):
        ps = ps_ref[rows(bi), :]
        z = ps + bias
        gt = jnp.where(lane < 4, jax.nn.sigmoid(ps),
                       jnp.where(lane < 8, -jnp.exp(alog) * jax.nn.softplus(z),
                                 jnp.where(lane < 12, z, jax.nn.log_sigmoid(z))))
        gt2 = gt * LOG2E
        cm2 = _cumsum_rows(tri_bf, gt) * LOG2E
        gates.append(gt)
        gates2.append(gt2)
        cums2.append(cm2)
        gates2_t.append(gt2.T)
        cums2_t.append(cm2.T)
        xpad_ref[bi, 8:8 + c, :] = pm_ref[rows(bi), 0:3 * MIX_W]
        acc = cw_ref[0:1, :] * xpad_ref[bi, 5:5 + c, :]
        for j in range(1, CONV_W):
            acc = acc + cw_ref[j:j + 1, :] * xpad_ref[bi, 5 + j:5 + j + c, :]
        convs.append(jax.nn.silu(acc))
        xpad_ref[bi, 5:8, :] = xpad_ref[bi, 5 + c:8 + c, :]

    col = lambda arrs, j: jnp.stack([arrs[bi][:, j + h:j + h + 1] for bi, h in gh])
    row = lambda arrs, j: jnp.stack([arrs[bi][j + h:j + h + 1, :] for bi, h in gh])
    part = lambda p: jnp.stack([convs[bi][:, p * MIX_W + h * HEAD_DIM:p * MIX_W + (h + 1) * HEAD_DIM]
                                for bi, h in gh])


    def gdn():
        q = part(0)
        k = part(1)
        v = part(2)
        q = q * (lax.rsqrt(jnp.sum(q * q, -1, keepdims=True) + EPS) * HEAD_DIM ** -0.5)
        k = k * lax.rsqrt(jnp.sum(k * k, -1, keepdims=True) + EPS)
        beta = col(gates, 0)
        Gc = col(cums2, 4)
        Gr = row(cums2_t, 4)
        decay = jnp.exp2(jnp.where(causal, Gc - Gr, -jnp.inf))
        S = gS_o[...].reshape(G, HEAD_DIM, HEAD_DIM)
        qk2 = jnp.concatenate([q, k], axis=1)
        P = _bmm_nt(qk2, k)
        R = _bmm(qk2, S)
        QK, KK = P[:, :c], P[:, c:]
        QS, KS = R[:, :c], R[:, c:]
        eG = jnp.exp2(Gc)
        rhs = beta * (v - eG * KS)
        Mp = -jnp.where(strict, beta * decay * KK, 0.0)
        Tinv = eye + Mp
        Ms = _split2(Mp)
        yield
        for _ in range(n_sq):
            Mp = _bmm_split(Ms, (Ms[0], None))
            Ms = _split2(Mp)
            Tinv = Tinv + _bmm_split(_split2(Tinv), (Ms[0], None))
            yield
        u = _bmm_split(_split2(Tinv), _split2(rhs))
        yield
        o = eG * QS + _bmm(QK * decay, u)
        Gl = Gc[:, lv - 1:lv, :]
        wl = jnp.where(live, jnp.exp2(Gl - Gc), 0.0)
        gS_o[...] = (jnp.exp2(Gl) * S + _bmm_tn(k * wl, u)).reshape(bb, N_HEADS, HEAD_DIM, HEAD_DIM)
        put(0, _rms(o, head_rows(nrm_ref, 0)) * jax.nn.silu(heads(3 * MIX_W)))

    def mlstm():
        yield
        q = heads(4 * MIX_W)
        k = heads(5 * MIX_W) * HEAD_DIM ** -0.5
        ig_c = col(gates2, 8)
        ig_r = row(gates2_t, 8)
        Fc = col(cums2, 12)
        Fr = row(cums2_t, 12)
        m0 = jnp.stack([mm_o[bi, h:h + 1, 0:1] for bi, h in gh])
        n_row = jnp.stack([mn_o[bi, h:h + 1, :] for bi, h in gh])
        C = mC_o[...].reshape(G, HEAD_DIM, HEAD_DIM)
        logD = jnp.where(causal, Fc - Fr + ig_r, -jnp.inf)
        m02 = m0 * LOG2E
        b = Fc + m02
        mt = jnp.maximum(b, jnp.max(logD, -1, keepdims=True))
        s = _bmm_nt(q, k) * jnp.exp2(logD - mt)
        inter = jnp.exp2(b - mt)
        yield
        v = heads(6 * MIX_W)
        num = _bmm(s, v) + inter * _bmm(q, C)
        den = jnp.sum(s, -1, keepdims=True) + inter * jnp.sum(q * n_row, -1, keepdims=True)
        hm = num / jnp.maximum(jnp.abs(den), jnp.exp2(-mt))
        yield
        mt_l = mt[:, lv - 1:lv, :]
        Fl = Fc[:, lv - 1:lv, :]
        wl = jnp.where(live, jnp.exp2(Fl - Fc + ig_c - mt_l), 0.0)
        d0 = jnp.exp2(Fl + m02 - mt_l)
        kw = k * wl
        mC_o[...] = (d0 * C + _bmm_tn(kw, v)).reshape(bb, N_HEADS, HEAD_DIM, HEAD_DIM)
        n_new = d0 * n_row + jnp.sum(kw, axis=1, keepdims=True)
        m_b = jnp.broadcast_to(mt_l * LN2, (G, 1, HEAD_DIM))
        for g, (bi, h) in enumerate(gh):
            mn_o[bi, h:h + 1, :] = n_new[g]
            mm_o[bi, h:h + 1, :] = m_b[g]
        yield
        put(MIX_W, _rms(hm, head_rows(nrm_ref, 1)) * jax.nn.sigmoid(heads(7 * MIX_W)))

    def hgrn():
        hq = jax.nn.silu(heads(8 * MIX_W))
        lbh = lb_ref[...]
        Gs_l, kk_l = [], []
        for bi in range(bb):
            fg = lbh + (1.0 - lbh) * jax.nn.sigmoid(pm_ref[rows(bi), 9 * MIX_W:10 * MIX_W])
            kk_l.append(1.0 - fg)
            Gs_l.append(_cumsum_rows(tri_bf, jnp.log2(fg)))
        kk = jnp.stack([kk_l[bi][:, h * HEAD_DIM:(h + 1) * HEAD_DIM] for bi, h in gh])
        Gm = jnp.stack([Gs_l[bi][:, h * HEAD_DIM:(h + 1) * HEAD_DIM] for bi, h in gh])
        S = hS_o[...].reshape(G, HEAD_DIM, HEAD_DIM)
        yield
        tiles = lambda a: a.reshape(G * nblk, sub, HEAD_DIM)
        hq_t, kk_t, Gm_t = tiles(hq), tiles(kk), tiles(Gm)
        A = jnp.zeros((G, c, c), F32)
        for dlt in range(sub):
            ks = kk_t if dlt == 0 else pltpu.roll(kk_t, dlt, 1)
            Gs = Gm_t if dlt == 0 else pltpu.roll(Gm_t, dlt, 1)
            d = jnp.sum(hq_t * ks * jnp.exp2(Gm_t - Gs), -1, keepdims=True)
            A = jnp.where((s2 == r2 - dlt) & (jnp.bitwise_and(r2, sub - 1) >= dlt), d.reshape(G, c, 1), A)
            if dlt % 2 == 1:
                yield
        if nblk > 1:
            Gref = jnp.concatenate(
                [jnp.broadcast_to(Gm[:, I * sub:I * sub + 1, :], (G, sub, HEAD_DIM)) for I in range(nblk)], axis=1)
            qt = hq * jnp.exp2(Gm - Gref)
            pieces = [jnp.zeros((G, sub, c), F32)]
            for I in range(1, nblk):
                kt = kk[:, :I * sub] * jnp.exp2(Gm[:, I * sub:I * sub + 1, :] - Gm[:, :I * sub])
                kt = jnp.concatenate([kt, jnp.zeros((G, c - I * sub, HEAD_DIM), F32)], axis=1)
                pieces.append(_bmm_nt(qt[:, I * sub:(I + 1) * sub], kt))
                if I % 2 == 1:
                    yield
            A = jnp.where(s2 < jnp.bitwise_and(r2, -sub), jnp.concatenate(pieces, axis=1), A)
        hi = heads(10 * MIX_W)
        o = _bmm(hq * jnp.exp2(Gm), S) + _bmm(A, hi)
        Gl = Gm[:, lv - 1:lv, :]
        kw = jnp.where(live, kk * jnp.exp2(Gl - Gm), 0.0)
        dcol = jnp.exp2(jnp.swapaxes(jnp.broadcast_to(Gl, (G, 8, HEAD_DIM)), 1, 2)[:, :, 0:1])
        hS_o[...] = (dcol * S + _bmm_tn(kw, hi)).reshape(bb, N_HEADS, HEAD_DIM, HEAD_DIM)
        yield
        put(2 * MIX_W, _rms(o, head_rows(nrm_ref, 2)) * jax.nn.silu(heads(11 * MIX_W)))

    def proj_rest():
        proj(8, 12)
        yield
        proj(4, 8)

    streams = [gdn(), proj_rest(), hgrn(), mlstm()]
    while streams:
        for st in list(streams):
            if next(st, "done") == "done":
                streams.remove(st)


def _mixers(x3, ln, w_main, w_small, gS, gconv, mC, mn, mm, hS, cw, sp, nrm, lb, l, earlier, *, bb, c, lv):
    Bt, Lp, _ = x3.shape
    n_earlier = len(earlier)
    kern = functools.partial(_mix_kernel, bb=bb, c=c, lv=lv, layer=l, n_earlier=n_earlier)
    st4 = pl.BlockSpec((bb, N_HEADS, HEAD_DIM, HEAD_DIM), lambda b, j: (b, 0, 0, 0))
    st3 = pl.BlockSpec((bb, N_HEADS, HEAD_DIM), lambda b, j: (b, 0, 0))
    in4 = pl.BlockSpec((None, bb, N_HEADS, HEAD_DIM, HEAD_DIM), lambda b, j: (l, b, 0, 0, 0))
    in3 = pl.BlockSpec((None, bb, N_HEADS, HEAD_DIM), lambda b, j: (l, b, 0, 0))
    full = lambda a: pl.BlockSpec(a.shape, lambda b, j: (0,) * a.ndim)
    once = dict(pipeline_mode=pl.Buffered(1))
    if n_earlier:
        depth = n_earlier + 1
        out4 = pl.BlockSpec((depth, bb, N_HEADS, HEAD_DIM, HEAD_DIM), lambda b, j: (0, b, 0, 0, 0))
        out3 = pl.BlockSpec((depth, bb, N_HEADS, HEAD_DIM), lambda b, j: (0, b, 0, 0))
        lead = (depth,)
    else:
        out4, out3, lead = st4, st3, ()
    return pl.pallas_call(
        kern,
        grid=(Bt // bb, Lp // c),
        in_specs=[pl.BlockSpec((bb, c, D_MODEL), lambda b, j: (b, j, 0)),
                  full(ln),
                  pl.BlockSpec((None, D_MODEL, N_MIX), lambda b, j: (l, 0, 0), **once),
                  pl.BlockSpec((None, D_MODEL, 128), lambda b, j: (l, 0, 0), **once),
                  in4,
                  pl.BlockSpec((None, bb, CONV_W - 1, 3 * MIX_W), lambda b, j: (l, b, 0, 0)),
                  in4, in3, in3, in4,
                  full(cw), full(sp), full(nrm), full(lb)] + [st4, st4, st3, st3, st4] * n_earlier,
        out_specs=[pl.BlockSpec((bb, c, 3 * MIX_W), lambda b, j: (b, j, 0)),
                   pl.BlockSpec((bb, CONV_W - 1, 3 * MIX_W), lambda b, j: (b, 0, 0)),
                   out4, out4, out3, out3, out4],
        out_shape=[jax.ShapeDtypeStruct((Bt, Lp, 3 * MIX_W), BF16),
                   jax.ShapeDtypeStruct(gconv.shape[1:], F32),
                   jax.ShapeDtypeStruct(lead + gS.shape[1:], F32),
                   jax.ShapeDtypeStruct(lead + mC.shape[1:], F32),
                   jax.ShapeDtypeStruct(lead + mn.shape[1:], F32),
                   jax.ShapeDtypeStruct(lead + mm.shape[1:], F32),
                   jax.ShapeDtypeStruct(lead + hS.shape[1:], F32)],
        scratch_shapes=[pltpu.VMEM((bb, c + 8, 3 * MIX_W), F32),
                        pltpu.VMEM((bb * c, N_MIX), F32),
                        pltpu.VMEM((bb * c, 128), F32)],
        compiler_params=_params(2),
        name="mixers",
    )(x3, ln, w_main, w_small, gS, gconv, mC, mn, mm, hS, cw, sp, nrm, lb, *[a for e in earlier for a in e])


def _chan_kernel(om_ref, x_ref, st_ref, lnm_ref, wg_ref, wbr_ref, wout_ref, lnf_ref, wup_ref, cw_ref, cb_ref,
                 wd_ref, lno_ref, out_ref, ust_ref, *carry, tm, nseq, tiles_per_seq, st_end, final):
    rows = tm // nseq
    x = x_ref[...]
    hm = _rms(x, lnm_ref[...]).astype(BF16)
    acc = None
    for n in range(N_BRANCH):
        br = jnp.dot(om_ref[:, n * MIX_W:(n + 1) * MIX_W], wbr_ref[n], preferred_element_type=F32)
        pg = jnp.dot(hm, wg_ref[:, n * D_MODEL:(n + 1) * D_MODEL], preferred_element_type=F32)
        t = jax.nn.sigmoid(pg) * br
        acc = t if acc is None else acc + t
    x1 = x + jnp.dot(acc.astype(BF16), wout_ref[...], preferred_element_type=F32)
    hf = _rms(x1, lnf_ref[...]).astype(BF16)

    if tiles_per_seq > 1:
        prev_ref, = carry

        @pl.when(pl.program_id(0) % tiles_per_seq == 0)
        def _():
            prev_ref[:, 0:2, :] = st_ref[...]
    else:
        prev_ref = st_ref

    row8 = lax.broadcasted_iota(jnp.int32, (1, 8, 1), 1)

    def up(col0):
        return jnp.dot(hf, wup_ref[:, col0:col0 + FFN_COLS], preferred_element_type=F32)

    def conv(col0, u_raw):
        cs = slice(col0, col0 + FFN_COLS)
        w0, w1, w2, cb = cw_ref[0:1, cs], cw_ref[1:2, cs], cw_ref[2:3, cs], cb_ref[:, cs]
        uc = u_raw.reshape(nseq, rows, FFN_COLS)
        r1 = pltpu.roll(uc, 1, 1)
        r2 = pltpu.roll(uc, 2, 1)
        p0 = prev_ref[:, 0:1, cs]
        p1 = prev_ref[:, 1:2, cs]
        s1 = jnp.where(row8 == 0, p1, r1[:, 0:8])
        s2 = jnp.where(row8 == 0, p0, jnp.where(row8 == 1, p1, r2[:, 0:8]))
        y = w0 * s2 + w1 * s1 + w2 * uc[:, 0:8] + cb
        if rows > 8:
            y = jnp.concatenate([y, w0 * r2[:, 8:] + w1 * r1[:, 8:] + w2 * uc[:, 8:] + cb], axis=1)
        ust_ref[:, :, cs] = uc[:, st_end - 2:st_end, :]
        if tiles_per_seq > 1:
            prev_ref[:, 0:2, cs] = uc[:, rows - 2:rows, :]
        return y.reshape(tm, FFN_COLS)

    n_chunks = D_FF // FFN_COLS
    out = x1
    nxt = up(0), up(D_FF)
    for jc in range(n_chunks):
        cur = nxt
        if jc + 1 < n_chunks:
            nxt = up((jc + 1) * FFN_COLS), up(D_FF + (jc + 1) * FFN_COLS)
        ua = conv(jc * FFN_COLS, cur[0])
        ub = conv(D_FF + jc * FFN_COLS, cur[1])
        act = (jax.nn.silu(ua) * ub).astype(BF16)
        out = out + jnp.dot(act, wd_ref[jc * FFN_COLS:(jc + 1) * FFN_COLS, :], preferred_element_type=F32)
    out_ref[...] = _rms(out, lno_ref[...]) if final else out


def _chan(om, x, st, W, ln_final, l, *, rows_per_seq, st_end, tm, final):
    T = x.shape[0]
    nseq = max(1, tm // rows_per_seq)
    tiles_per_seq = max(1, rows_per_seq // tm)
    kern = functools.partial(_chan_kernel, tm=tm, nseq=nseq, tiles_per_seq=tiles_per_seq,
                             st_end=st_end, final=final)
    once = dict(pipeline_mode=pl.Buffered(1))
    st_spec = pl.BlockSpec((nseq, FFN_CONV_W - 1, 2 * D_FF), lambda i: (i // tiles_per_seq, 0, 0))
    return pl.pallas_call(
        kern,
        grid=(T // tm,),
        in_specs=[pl.BlockSpec((tm, 3 * MIX_W), lambda i: (i, 0)),
                  pl.BlockSpec((tm, D_MODEL), lambda i: (i, 0)),
                  st_spec,
                  pl.BlockSpec((1, D_MODEL), lambda i: (0, 0)),
                  pl.BlockSpec((None, D_MODEL, N_BRANCH * D_MODEL),
                               lambda i: (l, 0, N_MIX // (N_BRANCH * D_MODEL)), **once),
                  pl.BlockSpec((None, N_BRANCH, MIX_W, D_MODEL), lambda i: (l, 0, 0, 0), **once),
                  pl.BlockSpec((None, D_MODEL, D_MODEL), lambda i: (l, 0, 0), **once),
                  pl.BlockSpec((1, D_MODEL), lambda i: (0, 0)),
                  pl.BlockSpec((None, D_MODEL, 2 * D_FF), lambda i: (l, 0, 0), **once),
                  pl.BlockSpec((FFN_CONV_W, 2 * D_FF), lambda i: (0, 0)),
                  pl.BlockSpec((1, 2 * D_FF), lambda i: (0, 0)),
                  pl.BlockSpec((None, D_FF, D_MODEL), lambda i: (l, 0, 0), **once),
                  pl.BlockSpec((1, D_MODEL), lambda i: (0, 0))],
        out_specs=[pl.BlockSpec((tm, D_MODEL), lambda i: (i, 0)), st_spec],
        out_shape=[jax.ShapeDtypeStruct((T, D_MODEL), F32),
                   jax.ShapeDtypeStruct(st.shape, F32)],
        scratch_shapes=[pltpu.VMEM((1, 8, 2 * D_FF), F32)] if tiles_per_seq > 1 else [],
        compiler_params=_params(1),
        name="chan",
    )(om, x, st, W["ln_mix"][l][None], W["w_main"], W["w_br"], W["w_out"], W["ln_ffn"][l][None], W["w_up"],
      W["fcw"][l], W["fcb"][l][None], W["w_down"], ln_final[None])


def _prep_weights(ln_mix, w_in, gdn_conv_w, gdn_A_log, gdn_dt_bias, gdn_norm, m_ibias, m_fbias,
                  m_norm, lb_all, hgrn_norm, w_br, w_out, ln_ffn, w_up, ffn_conv_w, ffn_conv_b, w_down):
    depth = w_in.shape[0]
    w_main = jnp.concatenate([w_in[:, :, 0:2048], w_in[:, :, 2056:4104], w_in[:, :, 4112:]], axis=2).astype(BF16)
    w_small = jnp.concatenate([w_in[:, :, 2048:2056], w_in[:, :, 4104:4112],
                               jnp.zeros((depth, D_MODEL, 128 - 4 * N_HEADS), F32)], axis=2).astype(BF16)
    z4 = jnp.zeros((depth, N_HEADS), F32)
    pad = jnp.zeros((depth, 128 - 4 * N_HEADS), F32)
    sp = jnp.zeros((depth, 8, 128), F32)
    sp = sp.at[:, 0].set(jnp.concatenate([z4, gdn_A_log, z4, z4, pad], axis=1))
    sp = sp.at[:, 1].set(jnp.concatenate([z4, gdn_dt_bias, m_ibias, m_fbias, pad], axis=1))
    return dict(
        ln_mix=ln_mix, w_main=w_main, w_small=w_small, cw=gdn_conv_w, sp=sp,
        nrm=jnp.stack([gdn_norm, m_norm, hgrn_norm], axis=1), lb=lb_all,
        w_br=w_br.astype(BF16), w_out=w_out.astype(BF16), ln_ffn=ln_ffn,
        w_up=w_up.astype(BF16), fcw=ffn_conv_w, fcb=ffn_conv_b, w_down=w_down.astype(BF16))


def _trunk(x3, states, W, ln_final, *, lv, c, bb, bb_gather, sample):
    Bt, Lp, _ = x3.shape
    T = Bt * Lp
    x = x3.reshape(T, D_MODEL)
    gdn_S, gdn_conv, m_C, m_n, m_m, h_S, ffn_conv = states
    depth = W["w_main"].shape[0]
    mm_in = jnp.broadcast_to(m_m[..., None], m_m.shape + (HEAD_DIM,))
    gconv_new, fconv_new, per_layer = [], [], []
    for l in range(depth):
        last = l == depth - 1
        gather = last and bb_gather is not None and depth > 1
        om, gcs, *st = _mixers(
            x.reshape(Bt, Lp, D_MODEL), W["ln_mix"][l][None], W["w_main"], W["w_small"],
            gdn_S, gdn_conv, m_C, m_n, mm_in, h_S,
            W["cw"][l], W["sp"][l], W["nrm"][l], W["lb"][l][None], l, per_layer if gather else (),
            bb=bb_gather if gather else bb, c=c, lv=lv)
        per_layer.append(st)
        tm = 256
        x, ust = _chan(om.reshape(T, 3 * MIX_W), x, ffn_conv[l], W, ln_final, l, rows_per_seq=Lp,
                       st_end=lv if sample else tm, tm=tm, final=last)
        gconv_new.append(gcs)
        fconv_new.append(ust)
    gS, mC, mn, mm, hS = st if gather else [jnp.stack(s) for s in zip(*per_layer)]
    return x.reshape(Bt, Lp, D_MODEL), (gS, jnp.stack(gconv_new), mC, mn, mm[..., 0], hS, jnp.stack(fconv_new))


def kernel(x_prompt, x_sample, state_gdn_S, state_gdn_conv, state_mlstm_C, state_mlstm_n, state_mlstm_m, state_hgrn_S, state_ffn_conv, ln_mix, w_in, gdn_conv_w, gdn_A_log, gdn_dt_bias, gdn_norm, m_ibias, m_fbias, m_norm, hgrn_lb, hgrn_norm, w_br, w_out, ln_ffn, w_up, ffn_conv_w, ffn_conv_b, w_down, ln_final):
    depth = w_in.shape[0]
    lb_all = jnp.cumsum(jax.nn.softmax(hgrn_lb.astype(F32), axis=0), axis=0)
    lb_all = lb_all - lb_all[0]
    layers = _prep_weights(ln_mix, w_in, gdn_conv_w, gdn_A_log, gdn_dt_bias, gdn_norm, m_ibias,
                           m_fbias, m_norm, lb_all, hgrn_norm, w_br, w_out, ln_ffn, w_up, ffn_conv_w,
                           ffn_conv_b, w_down)

    B, L, _ = x_prompt.shape
    H, Dh = N_HEADS, HEAD_DIM
    zeros = (jnp.zeros((depth, B, H, Dh, Dh), F32), jnp.zeros((depth, B, CONV_W - 1, 3 * MIX_W), F32),
             jnp.zeros((depth, B, H, Dh, Dh), F32), jnp.zeros((depth, B, H, Dh), F32),
             jnp.zeros((depth, B, H), F32), jnp.zeros((depth, B, H, Dh, Dh), F32),
             jnp.zeros((depth, B, FFN_CONV_W - 1, 2 * D_FF), F32))
    y_p, st_p = _trunk(x_prompt, zeros, layers, ln_final, lv=CHUNK, c=CHUNK, bb=4, bb_gather=None, sample=False)

    Bs, Ls, _ = x_sample.shape
    xs = jnp.pad(x_sample, ((0, 0), (0, SAMPLE_PAD - Ls), (0, 0)))
    st_in = (state_gdn_S, state_gdn_conv, state_mlstm_C, state_mlstm_n, state_mlstm_m, state_hgrn_S,
             state_ffn_conv)
    y_s, st_s = _trunk(xs, st_in, layers, ln_final, lv=Ls, c=SAMPLE_PAD, bb=8, bb_gather=4, sample=True)
    return (y_p, y_s[:, :Ls]) + st_p + st_s
```

```python
import functools
import math

import jax
import jax.numpy as jnp
from jax import lax
from jax.experimental import pallas as pl
from jax.experimental.pallas import tpu as pltpu

D_MODEL = 1024
N_HEADS = 4
HEAD_DIM = 128
MIX_W = N_HEADS * HEAD_DIM
N_BRANCH = 3
CONV_W = 4
FFN_CONV_W = 3
D_FF = 2816
CHUNK = 64
EPS = 1e-6
N_MIX = 6144
SAMPLE_PAD = 8
SUB = 8
FFN_COLS = D_FF
LOG2E = 1.4426950408889634
LN2 = 0.6931471805599453

F32 = jnp.float32
BF16 = jnp.bfloat16
VMEM_LIMIT = 56 * 1024 * 1024


def _bmm(a, b):
    return jnp.einsum('gtd,gde->gte', a.astype(BF16), b.astype(BF16), preferred_element_type=F32)


def _bmm_nt(a, b):
    return jnp.einsum('gtd,gsd->gts', a.astype(BF16), b.astype(BF16), preferred_element_type=F32)


def _bmm_tn(a, b):
    return jnp.einsum('gsd,gse->gde', a.astype(BF16), b.astype(BF16), preferred_element_type=F32)


def _split2(a):
    hi = a.astype(BF16)
    return hi, (a - hi.astype(F32)).astype(BF16)


def _bmm_split(a, b):
    a_hi, a_lo = a
    b_hi, b_lo = b
    n = a_hi.shape[1]
    p = jnp.einsum('gtk,gks->gts', jnp.concatenate([a_hi, a_lo], axis=1), b_hi, preferred_element_type=F32)
    if b_lo is None:
        return p[:, :n] + p[:, n:]
    return p[:, :n] + p[:, n:] + jnp.einsum('gtk,gks->gts', a_hi, b_lo, preferred_element_type=F32)


def _cumsum_rows(tri_bf, x):
    n = x.shape[1]
    x1 = x.astype(BF16)
    r1 = x - x1.astype(F32)
    x2 = r1.astype(BF16)
    x3 = (r1 - x2.astype(F32)).astype(BF16)
    p = jnp.dot(tri_bf, jnp.concatenate([x1, x2, x3], axis=1), preferred_element_type=F32)
    return p[:, 0:n] + p[:, n:2 * n] + p[:, 2 * n:3 * n]


def _rms(x, g):
    return x * lax.rsqrt(jnp.mean(x * x, -1, keepdims=True) + EPS) * g


def _params(n_axes):
    return pltpu.CompilerParams(dimension_semantics=("arbitrary",) * n_axes,
                                vmem_limit_bytes=VMEM_LIMIT)


def _mix_kernel(x_ref, ln_ref, wm_ref, ws_ref, gS_ref, gconv_ref, mC_ref, mn_ref, mm_ref, hS_ref,
                cw_ref, sp_ref, nrm_ref, lb_ref, *rest, bb, c, lv, layer, n_earlier):
    earlier = [rest[5 * d:5 * d + 5] for d in range(n_earlier)]
    om_ref, gcs_o, *st_o = rest[5 * n_earlier:5 * n_earlier + 7]
    xpad_ref, pm_ref, ps_ref = rest[5 * n_earlier + 7:]
    gS_o, mC_o, mn_o, mm_o, hS_o = [r.at[layer] for r in st_o] if n_earlier else st_o
    G = bb * N_HEADS
    gh = [(bi, h) for bi in range(bb) for h in range(N_HEADS)]
    rows = lambda bi: slice(bi * c, (bi + 1) * c)

    @pl.when(pl.program_id(1) == 0)
    def _():
        gS_o[...] = gS_ref[...]
        mC_o[...] = mC_ref[...]
        mn_o[...] = mn_ref[...]
        mm_o[...] = mm_ref[...]
        hS_o[...] = hS_ref[...]
        xpad_ref[:, 5:8, :] = gconv_ref[...]
        for d, refs in enumerate(earlier):
            for dst, src in zip(st_o, refs):
                dst[d] = src[...]

    hb = _rms(x_ref[...].reshape(bb * c, D_MODEL), ln_ref[...]).astype(BF16)

    def proj(g0, g1):
        cs = slice(g0 * MIX_W, g1 * MIX_W)
        pm_ref[:, cs] = jnp.dot(hb, wm_ref[:, cs], preferred_element_type=F32)

    ps_ref[...] = jnp.dot(hb, ws_ref[...], preferred_element_type=F32)
    proj(0, 4)
    for bi in range(bb):
        gcs_o[bi] = pm_ref[bi * c + lv - (CONV_W - 1):bi * c + lv, 0:3 * MIX_W]

    r2 = lax.broadcasted_iota(jnp.int32, (c, c), 0)
    s2 = lax.broadcasted_iota(jnp.int32, (c, c), 1)
    causal = r2 >= s2
    strict = r2 > s2
    tri_bf = causal.astype(BF16)
    eye = (r2 == s2).astype(F32)
    rowc = lax.broadcasted_iota(jnp.int32, (c, 1), 0)
    live = rowc < lv
    lane = lax.broadcasted_iota(jnp.int32, (c, 128), 1)
    sub = min(SUB, c)
    nblk = c // sub
    n_sq = int(math.log2(c)) - 1

    def heads(col0):
        return jnp.stack([pm_ref[rows(bi), col0 + h * HEAD_DIM:col0 + (h + 1) * HEAD_DIM] for bi, h in gh])

    def head_rows(ref, r):
        return jnp.stack([ref[r:r + 1, h * HEAD_DIM:(h + 1) * HEAD_DIM] for _, h in gh])

    def put(col0, val):
        for g, (bi, h) in enumerate(gh):
            om_ref[bi, :, col0 + h * HEAD_DIM:col0 + (h + 1) * HEAD_DIM] = val[g].astype(om_ref.dtype)

    alog = sp_ref[0:1, :]
    bias = sp_ref[1:2, :]
    gates, gates2, cums2, gates2_t, cums2_t, convs = [], [], [], [], [], []
    for bi in range(bb):
        ps = ps_ref[rows(bi), :]
        z = ps + bias
        gt = jnp.where(lane < 4, jax.nn.sigmoid(ps),
                       jnp.where(lane < 8, -jnp.exp(alog) * jax.nn.softplus(z),
                                 jnp.where(lane < 12, z, jax.nn.log_sigmoid(z))))
        gt2 = gt * LOG2E
        cm2 = _cumsum_rows(tri_bf, gt) * LOG2E
        gates.append(gt)
        gates2.append(gt2)
        cums2.append(cm2)
        gates2_t.append(gt2.T)
        cums2_t.append(cm2.T)
        xpad_ref[bi, 8:8 + c, :] = pm_ref[rows(bi), 0:3 * MIX_W]
        acc = cw_ref[0:1, :] * xpad_ref[bi, 5:5 + c, :]
        for j in range(1, CONV_W):
            acc = acc + cw_ref[j:j + 1, :] * xpad_ref[bi, 5 + j:5 + j + c, :]
        convs.append(jax.nn.silu(acc))
        xpad_ref[bi, 5:8, :] = xpad_ref[bi, 5 + c:8 + c, :]

    col = lambda arrs, j: jnp.stack([arrs[bi][:, j + h:j + h + 1] for bi, h in gh])
    row = lambda arrs, j: jnp.stack([arrs[bi][j + h:j + h + 1, :] for bi, h in gh])
    part = lambda p: jnp.stack([convs[bi][:, p * MIX_W + h * HEAD_DIM:p * MIX_W + (h + 1) * HEAD_DIM]
                                for bi, h in gh])


    def gdn():
        q = part(0)
        k = part(1)
        v = part(2)
        q = q * (lax.rsqrt(jnp.sum(q * q, -1, keepdims=True) + EPS) * HEAD_DIM ** -0.5)
        k = k * lax.rsqrt(jnp.sum(k * k, -1, keepdims=True) + EPS)
        beta = col(gates, 0)
        Gc = col(cums2, 4)
        Gr = row(cums2_t, 4)
        decay = jnp.exp2(jnp.where(causal, Gc - Gr, -jnp.inf))
        S = gS_o[...].reshape(G, HEAD_DIM, HEAD_DIM)
        qk2 = jnp.concatenate([q, k], axis=1)
        P = _bmm_nt(qk2, k)
        R = _bmm(qk2, S)
        QK, KK = P[:, :c], P[:, c:]
        QS, KS = R[:, :c], R[:, c:]
        eG = jnp.exp2(Gc)
        rhs = beta * (v - eG * KS)
        Mp = -jnp.where(strict, beta * decay * KK, 0.0)
        Tinv = eye + Mp
        Ms = _split2(Mp)
        yield
        for _ in range(n_sq):
            Mp = _bmm_split(Ms, (Ms[0], None))
            Ms = _split2(Mp)
            Tinv = Tinv + _bmm_split(_split2(Tinv), (Ms[0], None))
            yield
        u = _bmm_split(_split2(Tinv), _split2(rhs))
        yield
        o = eG * QS + _bmm(QK * decay, u)
        Gl = Gc[:, lv - 1:lv, :]
        wl = jnp.where(live, jnp.exp2(Gl - Gc), 0.0)
        gS_o[...] = (jnp.exp2(Gl) * S + _bmm_tn(k * wl, u)).reshape(bb, N_HEADS, HEAD_DIM, HEAD_DIM)
        put(0, _rms(o, head_rows(nrm_ref, 0)) * jax.nn.silu(heads(3 * MIX_W)))

    def mlstm():
        yield
        q = heads(4 * MIX_W)
        k = heads(5 * MIX_W) * HEAD_DIM ** -0.5
        ig_c = col(gates2, 8)
        ig_r = row(gates2_t, 8)
        Fc = col(cums2, 12)
        Fr = row(cums2_t, 12)
        m0 = jnp.stack([mm_o[bi, h:h + 1, 0:1] for bi, h in gh])
        n_row = jnp.stack([mn_o[bi, h:h + 1, :] for bi, h in gh])
        C = mC_o[...].reshape(G, HEAD_DIM, HEAD_DIM)
        logD = jnp.where(causal, Fc - Fr + ig_r, -jnp.inf)
        m02 = m0 * LOG2E
        b = Fc + m02
        mt = jnp.maximum(b, jnp.max(logD, -1, keepdims=True))
        s = _bmm_nt(q, k) * jnp.exp2(logD - mt)
        inter = jnp.exp2(b - mt)
        yield
        v = heads(6 * MIX_W)
        num = _bmm(s, v) + inter * _bmm(q, C)
        den = jnp.sum(s, -1, keepdims=True) + inter * jnp.sum(q * n_row, -1, keepdims=True)
        hm = num / jnp.maximum(jnp.abs(den), jnp.exp2(-mt))
        yield
        mt_l = mt[:, lv - 1:lv, :]
        Fl = Fc[:, lv - 1:lv, :]
        wl = jnp.where(live, jnp.exp2(Fl - Fc + ig_c - mt_l), 0.0)
        d0 = jnp.exp2(Fl + m02 - mt_l)
        kw = k * wl
        mC_o[...] = (d0 * C + _bmm_tn(kw, v)).reshape(bb, N_HEADS, HEAD_DIM, HEAD_DIM)
        n_new = d0 * n_row + jnp.sum(kw, axis=1, keepdims=True)
        m_b = jnp.broadcast_to(mt_l * LN2, (G, 1, HEAD_DIM))
        for g, (bi, h) in enumerate(gh):
            mn_o[bi, h:h + 1, :] = n_new[g]
            mm_o[bi, h:h + 1, :] = m_b[g]
        yield
        put(MIX_W, _rms(hm, head_rows(nrm_ref, 1)) * jax.nn.sigmoid(heads(7 * MIX_W)))

    def hgrn():
        hq = jax.nn.silu(heads(8 * MIX_W))
        lbh = lb_ref[...]
        Gs_l, kk_l = [], []
        for bi in range(bb):
            fg = lbh + (1.0 - lbh) * jax.nn.sigmoid(pm_ref[rows(bi), 9 * MIX_W:10 * MIX_W])
            kk_l.append(1.0 - fg)
            Gs_l.append(_cumsum_rows(tri_bf, jnp.log2(fg)))
        kk = jnp.stack([kk_l[bi][:, h * HEAD_DIM:(h + 1) * HEAD_DIM] for bi, h in gh])
        Gm = jnp.stack([Gs_l[bi][:, h * HEAD_DIM:(h + 1) * HEAD_DIM] for bi, h in gh])
        S = hS_o[...].reshape(G, HEAD_DIM, HEAD_DIM)
        yield
        tiles = lambda a: a.reshape(G * nblk, sub, HEAD_DIM)
        hq_t, kk_t, Gm_t = tiles(hq), tiles(kk), tiles(Gm)
        A = jnp.zeros((G, c, c), F32)
        for dlt in range(sub):
            ks = kk_t if dlt == 0 else pltpu.roll(kk_t, dlt, 1)
            Gs = Gm_t if dlt == 0 else pltpu.roll(Gm_t, dlt, 1)
            d = jnp.sum(hq_t * ks * jnp.exp2(Gm_t - Gs), -1, keepdims=True)
            A = jnp.where((s2 == r2 - dlt) & (jnp.bitwise_and(r2, sub - 1) >= dlt), d.reshape(G, c, 1), A)
            if dlt % 2 == 1:
                yield
        if nblk > 1:
            Gref = jnp.concatenate(
                [jnp.broadcast_to(Gm[:, I * sub:I * sub + 1, :], (G, sub, HEAD_DIM)) for I in range(nblk)], axis=1)
            qt = hq * jnp.exp2(Gm - Gref)
            pieces = [jnp.zeros((G, sub, c), F32)]
            for I in range(1, nblk):
                kt = kk[:, :I * sub] * jnp.exp2(Gm[:, I * sub:I * sub + 1, :] - Gm[:, :I * sub])
                kt = jnp.concatenate([kt, jnp.zeros((G, c - I * sub, HEAD_DIM), F32)], axis=1)
                pieces.append(_bmm_nt(qt[:, I * sub:(I + 1) * sub], kt))
                if I % 2 == 1:
                    yield
            A = jnp.where(s2 < jnp.bitwise_and(r2, -sub), jnp.concatenate(pieces, axis=1), A)
        hi = heads(10 * MIX_W)
        o = _bmm(hq * jnp.exp2(Gm), S) + _bmm(A, hi)
        Gl = Gm[:, lv - 1:lv, :]
        kw = jnp.where(live, kk * jnp.exp2(Gl - Gm), 0.0)
        dcol = jnp.exp2(jnp.swapaxes(jnp.broadcast_to(Gl, (G, 8, HEAD_DIM)), 1, 2)[:, :, 0:1])
        hS_o[...] = (dcol * S + _bmm_tn(kw, hi)).reshape(bb, N_HEADS, HEAD_DIM, HEAD_DIM)
        yield
        put(2 * MIX_W, _rms(o, head_rows(nrm_ref, 2)) * jax.nn.silu(heads(11 * MIX_W)))

    def proj_rest():
        proj(8, 12)
        yield
        proj(4, 8)

    streams = [gdn(), proj_rest(), hgrn(), mlstm()]
    while streams:
        for st in list(streams):
            if next(st, "done") == "done":
                streams.remove(st)


def _mixers(x3, ln, w_main, w_small, gS, gconv, mC, mn, mm, hS, cw, sp, nrm, lb, l, earlier, *, bb, c, lv):
    Bt, Lp, _ = x3.shape
    n_earlier = len(earlier)
    kern = functools.partial(_mix_kernel, bb=bb, c=c, lv=lv, layer=l, n_earlier=n_earlier)
    st4 = pl.BlockSpec((bb, N_HEADS, HEAD_DIM, HEAD_DIM), lambda b, j: (b, 0, 0, 0))
    st3 = pl.BlockSpec((bb, N_HEADS, HEAD_DIM), lambda b, j: (b, 0, 0))
    in4 = pl.BlockSpec((None, bb, N_HEADS, HEAD_DIM, HEAD_DIM), lambda b, j: (l, b, 0, 0, 0))
    in3 = pl.BlockSpec((None, bb, N_HEADS, HEAD_DIM), lambda b, j: (l, b, 0, 0))
    full = lambda a: pl.BlockSpec(a.shape, lambda b, j: (0,) * a.ndim)
    once = dict(pipeline_mode=pl.Buffered(1))
    if n_earlier:
        depth = n_earlier + 1
        out4 = pl.BlockSpec((depth, bb, N_HEADS, HEAD_DIM, HEAD_DIM), lambda b, j: (0, b, 0, 0, 0))
        out3 = pl.BlockSpec((depth, bb, N_HEADS, HEAD_DIM), lambda b, j: (0, b, 0, 0))
        lead = (depth,)
    else:
        out4, out3, lead = st4, st3, ()
    return pl.pallas_call(
        kern,
        grid=(Bt // bb, Lp // c),
        in_specs=[pl.BlockSpec((bb, c, D_MODEL), lambda b, j: (b, j, 0)),
                  full(ln),
                  pl.BlockSpec((None, D_MODEL, N_MIX), lambda b, j: (l, 0, 0), **once),
                  pl.BlockSpec((None, D_MODEL, 128), lambda b, j: (l, 0, 0), **once),
                  in4,
                  pl.BlockSpec((None, bb, CONV_W - 1, 3 * MIX_W), lambda b, j: (l, b, 0, 0)),
                  in4, in3, in3, in4,
                  full(cw), full(sp), full(nrm), full(lb)] + [st4, st4, st3, st3, st4] * n_earlier,
        out_specs=[pl.BlockSpec((bb, c, 3 * MIX_W), lambda b, j: (b, j, 0)),
                   pl.BlockSpec((bb, CONV_W - 1, 3 * MIX_W), lambda b, j: (b, 0, 0)),
                   out4, out4, out3, out3, out4],
        out_shape=[jax.ShapeDtypeStruct((Bt, Lp, 3 * MIX_W), BF16),
                   jax.ShapeDtypeStruct(gconv.shape[1:], F32),
                   jax.ShapeDtypeStruct(lead + gS.shape[1:], F32),
                   jax.ShapeDtypeStruct(lead + mC.shape[1:], F32),
                   jax.ShapeDtypeStruct(lead + mn.shape[1:], F32),
                   jax.ShapeDtypeStruct(lead + mm.shape[1:], F32),
                   jax.ShapeDtypeStruct(lead + hS.shape[1:], F32)],
        scratch_shapes=[pltpu.VMEM((bb, c + 8, 3 * MIX_W), F32),
                        pltpu.VMEM((bb * c, N_MIX), F32),
                        pltpu.VMEM((bb * c, 128), F32)],
        compiler_params=_params(2),
        name="mixers",
    )(x3, ln, w_main, w_small, gS, gconv, mC, mn, mm, hS, cw, sp, nrm, lb, *[a for e in earlier for a in e])


def _chan_kernel(om_ref, x_ref, st_ref, lnm_ref, wg_ref, wbr_ref, wout_ref, lnf_ref, wup_ref, cw_ref, cb_ref,
                 wd_ref, lno_ref, out_ref, ust_ref, *carry, tm, nseq, tiles_per_seq, st_end, final):
    rows = tm // nseq
    x = x_ref[...]
    hm = _rms(x, lnm_ref[...]).astype(BF16)
    acc = None
    for n in range(N_BRANCH):
        br = jnp.dot(om_ref[:, n * MIX_W:(n + 1) * MIX_W], wbr_ref[n], preferred_element_type=F32)
        pg = jnp.dot(hm, wg_ref[:, n * D_MODEL:(n + 1) * D_MODEL], preferred_element_type=F32)
        t = jax.nn.sigmoid(pg) * br
        acc = t if acc is None else acc + t
    x1 = x + jnp.dot(acc.astype(BF16), wout_ref[...], preferred_element_type=F32)
    hf = _rms(x1, lnf_ref[...]).astype(BF16)

    if tiles_per_seq > 1:
        prev_ref, = carry

        @pl.when(pl.program_id(0) % tiles_per_seq == 0)
        def _():
            prev_ref[:, 0:2, :] = st_ref[...]
    else:
        prev_ref = st_ref

    row8 = lax.broadcasted_iota(jnp.int32, (1, 8, 1), 1)

    def up(col0):
        return jnp.dot(hf, wup_ref[:, col0:col0 + FFN_COLS], preferred_element_type=F32)

    def conv(col0, u_raw):
        cs = slice(col0, col0 + FFN_COLS)
        w0, w1, w2, cb = cw_ref[0:1, cs], cw_ref[1:2, cs], cw_ref[2:3, cs], cb_ref[:, cs]
        uc = u_raw.reshape(nseq, rows, FFN_COLS)
        r1 = pltpu.roll(uc, 1, 1)
        r2 = pltpu.roll(uc, 2, 1)
        p0 = prev_ref[:, 0:1, cs]
        p1 = prev_ref[:, 1:2, cs]
        s1 = jnp.where(row8 == 0, p1, r1[:, 0:8])
        s2 = jnp.where(row8 == 0, p0, jnp.where(row8 == 1, p1, r2[:, 0:8]))
        y = w0 * s2 + w1 * s1 + w2 * uc[:, 0:8] + cb
        if rows > 8:
            y = jnp.concatenate([y, w0 * r2[:, 8:] + w1 * r1[:, 8:] + w2 * uc[:, 8:] + cb], axis=1)
        ust_ref[:, :, cs] = uc[:, st_end - 2:st_end, :]
        if tiles_per_seq > 1:
            prev_ref[:, 0:2, cs] = uc[:, rows - 2:rows, :]
        return y.reshape(tm, FFN_COLS)

    n_chunks = D_FF // FFN_COLS
    out = x1
    nxt = up(0), up(D_FF)
    for jc in range(n_chunks):
        cur = nxt
        if jc + 1 < n_chunks:
            nxt = up((jc + 1) * FFN_COLS), up(D_FF + (jc + 1) * FFN_COLS)
        ua = conv(jc * FFN_COLS, cur[0])
        ub = conv(D_FF + jc * FFN_COLS, cur[1])
        act = (jax.nn.silu(ua) * ub).astype(BF16)
        out = out + jnp.dot(act, wd_ref[jc * FFN_COLS:(jc + 1) * FFN_COLS, :], preferred_element_type=F32)
    out_ref[...] = _rms(out, lno_ref[...]) if final else out


def _chan(om, x, st, W, ln_final, l, *, rows_per_seq, st_end, tm, final):
    T = x.shape[0]
    nseq = max(1, tm // rows_per_seq)
    tiles_per_seq = max(1, rows_per_seq // tm)
    kern = functools.partial(_chan_kernel, tm=tm, nseq=nseq, tiles_per_seq=tiles_per_seq,
                             st_end=st_end, final=final)
    once = dict(pipeline_mode=pl.Buffered(1))
    st_spec = pl.BlockSpec((nseq, FFN_CONV_W - 1, 2 * D_FF), lambda i: (i // tiles_per_seq, 0, 0))
    return pl.pallas_call(
        kern,
        grid=(T // tm,),
        in_specs=[pl.BlockSpec((tm, 3 * MIX_W), lambda i: (i, 0)),
                  pl.BlockSpec((tm, D_MODEL), lambda i: (i, 0)),
                  st_spec,
                  pl.BlockSpec((1, D_MODEL), lambda i: (0, 0)),
                  pl.BlockSpec((None, D_MODEL, N_BRANCH * D_MODEL),
                               lambda i: (l, 0, N_MIX // (N_BRANCH * D_MODEL)), **once),
                  pl.BlockSpec((None, N_BRANCH, MIX_W, D_MODEL), lambda i: (l, 0, 0, 0), **once),
                  pl.BlockSpec((None, D_MODEL, D_MODEL), lambda i: (l, 0, 0), **once),
                  pl.BlockSpec((1, D_MODEL), lambda i: (0, 0)),
                  pl.BlockSpec((None, D_MODEL, 2 * D_FF), lambda i: (l, 0, 0), **once),
                  pl.BlockSpec((FFN_CONV_W, 2 * D_FF), lambda i: (0, 0)),
                  pl.BlockSpec((1, 2 * D_FF), lambda i: (0, 0)),
                  pl.BlockSpec((None, D_FF, D_MODEL), lambda i: (l, 0, 0), **once),
                  pl.BlockSpec((1, D_MODEL), lambda i: (0, 0))],
        out_specs=[pl.BlockSpec((tm, D_MODEL), lambda i: (i, 0)), st_spec],
        out_shape=[jax.ShapeDtypeStruct((T, D_MODEL), F32),
                   jax.ShapeDtypeStruct(st.shape, F32)],
        scratch_shapes=[pltpu.VMEM((1, 8, 2 * D_FF), F32)] if tiles_per_seq > 1 else [],
        compiler_params=_params(1),
        name="chan",
    )(om, x, st, W["ln_mix"][l][None], W["w_main"], W["w_br"], W["w_out"], W["ln_ffn"][l][None], W["w_up"],
      W["fcw"][l], W["fcb"][l][None], W["w_down"], ln_final[None])


def _prep_weights(ln_mix, w_in, gdn_conv_w, gdn_A_log, gdn_dt_bias, gdn_norm, m_ibias, m_fbias,
                  m_norm, lb_all, hgrn_norm, w_br, w_out, ln_ffn, w_up, ffn_conv_w, ffn_conv_b, w_down):
    depth = w_in.shape[0]
    w_main = jnp.concatenate([w_in[:, :, 0:2048], w_in[:, :, 2056:4104], w_in[:, :, 4112:]], axis=2).astype(BF16)
    w_small = jnp.concatenate([w_in[:, :, 2048:2056], w_in[:, :, 4104:4112],
                               jnp.zeros((depth, D_MODEL, 128 - 4 * N_HEADS), F32)], axis=2).astype(BF16)
    z4 = jnp.zeros((depth, N_HEADS), F32)
    pad = jnp.zeros((depth, 128 - 4 * N_HEADS), F32)
    sp = jnp.zeros((depth, 8, 128), F32)
    sp = sp.at[:, 0].set(jnp.concatenate([z4, gdn_A_log, z4, z4, pad], axis=1))
    sp = sp.at[:, 1].set(jnp.concatenate([z4, gdn_dt_bias, m_ibias, m_fbias, pad], axis=1))
    return dict(
        ln_mix=ln_mix, w_main=w_main, w_small=w_small, cw=gdn_conv_w, sp=sp,
        nrm=jnp.stack([gdn_norm, m_norm, hgrn_norm], axis=1), lb=lb_all,
        w_br=w_br.astype(BF16), w_out=w_out.astype(BF16), ln_ffn=ln_ffn,
        w_up=w_up.astype(BF16), fcw=ffn_conv_w, fcb=ffn_conv_b, w_down=w_down.astype(BF16))


def _trunk(x3, states, W, ln_final, *, lv, c, bb, bb_gather, sample):
    Bt, Lp, _ = x3.shape
    T = Bt * Lp
    x = x3.reshape(T, D_MODEL)
    gdn_S, gdn_conv, m_C, m_n, m_m, h_S, ffn_conv = states
    depth = W["w_main"].shape[0]
    mm_in = jnp.broadcast_to(m_m[..., None], m_m.shape + (HEAD_DIM,))
    gconv_new, fconv_new, per_layer = [], [], []
    for l in range(depth):
        last = l == depth - 1
        gather = last and bb_gather is not None and depth > 1
        om, gcs, *st = _mixers(
            x.reshape(Bt, Lp, D_MODEL), W["ln_mix"][l][None], W["w_main"], W["w_small"],
            gdn_S, gdn_conv, m_C, m_n, mm_in, h_S,
            W["cw"][l], W["sp"][l], W["nrm"][l], W["lb"][l][None], l, per_layer if gather else (),
            bb=bb_gather if gather else bb, c=c, lv=lv)
        per_layer.append(st)
        tm = 256
        x, ust = _chan(om.reshape(T, 3 * MIX_W), x, ffn_conv[l], W, ln_final, l, rows_per_seq=Lp,
                       st_end=lv if sample else tm, tm=tm, final=last)
        gconv_new.append(gcs)
        fconv_new.append(ust)
    gS, mC, mn, mm, hS = st if gather else [jnp.stack(s) for s in zip(*per_layer)]
    return x.reshape(Bt, Lp, D_MODEL), (gS, jnp.stack(gconv_new), mC, mn, mm[..., 0], hS, jnp.stack(fconv_new))


def kernel(x_prompt, x_sample, state_gdn_S, state_gdn_conv, state_mlstm_C, state_mlstm_n, state_mlstm_m, state_hgrn_S, state_ffn_conv, ln_mix, w_in, gdn_conv_w, gdn_A_log, gdn_dt_bias, gdn_norm, m_ibias, m_fbias, m_norm, hgrn_lb, hgrn_norm, w_br, w_out, ln_ffn, w_up, ffn_conv_w, ffn_conv_b, w_down, ln_final):
    depth = w_in.shape[0]
    lb_all = jnp.cumsum(jax.nn.softmax(hgrn_lb.astype(F32), axis=0), axis=0)
    lb_all = lb_all - lb_all[0]
    layers = _prep_weights(ln_mix, w_in, gdn_conv_w, gdn_A_log, gdn_dt_bias, gdn_norm, m_ibias,
                           m_fbias, m_norm, lb_all, hgrn_norm, w_br, w_out, ln_ffn, w_up, ffn_conv_w,
                           ffn_conv_b, w_down)

    B, L, _ = x_prompt.shape
    H, Dh = N_HEADS, HEAD_DIM
    zeros = (jnp.zeros((depth, B, H, Dh, Dh), F32), jnp.zeros((depth, B, CONV_W - 1, 3 * MIX_W), F32),
             jnp.zeros((depth, B, H, Dh, Dh), F32), jnp.zeros((depth, B, H, Dh), F32),
             jnp.zeros((depth, B, H), F32), jnp.zeros((depth, B, H, Dh, Dh), F32),
             jnp.zeros((depth, B, FFN_CONV_W - 1, 2 * D_FF), F32))
    y_p, st_p = _trunk(x_prompt, zeros, layers, ln_final, lv=CHUNK, c=CHUNK, bb=4, bb_gather=None, sample=False)

    Bs, Ls, _ = x_sample.shape
    xs = jnp.pad(x_sample, ((0, 0), (0, SAMPLE_PAD - Ls), (0, 0)))
    st_in = (state_gdn_S, state_gdn_conv, state_mlstm_C, state_mlstm_n, state_mlstm_m, state_hgrn_S,
             state_ffn_conv)
    y_s, st_s = _trunk(xs, st_in, layers, ln_final, lv=Ls, c=SAMPLE_PAD, bb=8, bb_gather=4, sample=True)
    return (y_p, y_s[:, :Ls]) + st_p + st_s
```

```python
import functools
import math

import jax
import jax.numpy as jnp
from jax import lax
from jax.experimental import pallas as pl
from jax.experimental.pallas import tpu as pltpu

D_MODEL = 1024
N_HEADS = 4
HEAD_DIM = 128
MIX_W = N_HEADS * HEAD_DIM
N_BRANCH = 3
CONV_W = 4
FFN_CONV_W = 3
D_FF = 2816
CHUNK = 64
EPS = 1e-6
N_MIX = 6144
SAMPLE_PAD = 8
SUB = 8
FFN_COLS = D_FF
LOG2E = 1.4426950408889634
LN2 = 0.6931471805599453

F32 = jnp.float32
BF16 = jnp.bfloat16
VMEM_LIMIT = 56 * 1024 * 1024


def _bmm(a, b):
    return jnp.einsum('gtd,gde->gte', a.astype(BF16), b.astype(BF16), preferred_element_type=F32)


def _bmm_nt(a, b):
    return jnp.einsum('gtd,gsd->gts', a.astype(BF16), b.astype(BF16), preferred_element_type=F32)


def _bmm_tn(a, b):
    return jnp.einsum('gsd,gse->gde', a.astype(BF16), b.astype(BF16), preferred_element_type=F32)


def _split2(a):
    hi = a.astype(BF16)
    return hi, (a - hi.astype(F32)).astype(BF16)


def _bmm_split(a, b):
    a_hi, a_lo = a
    b_hi, b_lo = b
    n = a_hi.shape[1]
    p = jnp.einsum('gtk,gks->gts', jnp.concatenate([a_hi, a_lo], axis=1), b_hi, preferred_element_type=F32)
    if b_lo is None:
        return p[:, :n] + p[:, n:]
    return p[:, :n] + p[:, n:] + jnp.einsum('gtk,gks->gts', a_hi, b_lo, preferred_element_type=F32)


def _cumsum_rows(tri_bf, x):
    n = x.shape[1]
    x1 = x.astype(BF16)
    r1 = x - x1.astype(F32)
    x2 = r1.astype(BF16)
    x3 = (r1 - x2.astype(F32)).astype(BF16)
    p = jnp.dot(tri_bf, jnp.concatenate([x1, x2, x3], axis=1), preferred_element_type=F32)
    return p[:, 0:n] + p[:, n:2 * n] + p[:, 2 * n:3 * n]


def _rms(x, g):
    return x * lax.rsqrt(jnp.mean(x * x, -1, keepdims=True) + EPS) * g


def _params(n_axes):
    return pltpu.CompilerParams(dimension_semantics=("arbitrary",) * n_axes,
                                vmem_limit_bytes=VMEM_LIMIT)


def _mix_kernel(x_ref, ln_ref, wm_ref, ws_ref, gS_ref, gconv_ref, mC_ref, mn_ref, mm_ref, hS_ref,
                cw_ref, sp_ref, nrm_ref, lb_ref, *rest, bb, c, lv, layer, n_earlier):
    earlier = [rest[5 * d:5 * d + 5] for d in range(n_earlier)]
    om_ref, gcs_o, *st_o = rest[5 * n_earlier:5 * n_earlier + 7]
    xprev_ref, pm_ref, ps_ref = rest[5 * n_earlier + 7:]
    gS_o, mC_o, mn_o, mm_o, hS_o = [r.at[layer] for r in st_o] if n_earlier else st_o
    G = bb * N_HEADS
    gh = [(bi, h) for bi in range(bb) for h in range(N_HEADS)]
    rows = lambda bi: slice(bi * c, (bi + 1) * c)

    @pl.when(pl.program_id(1) == 0)
    def _():
        gS_o[...] = gS_ref[...]
        mC_o[...] = mC_ref[...]
        mn_o[...] = mn_ref[...]
        mm_o[...] = mm_ref[...]
        hS_o[...] = hS_ref[...]
        xprev_ref[:, 0:CONV_W - 1, :] = gconv_ref[...]
        for d, refs in enumerate(earlier):
            for dst, src in zip(st_o, refs):
                dst[d] = src[...]

    hb = _rms(x_ref[...].reshape(bb * c, D_MODEL), ln_ref[...]).astype(BF16)

    def proj(g0, g1):
        cs = slice(g0 * MIX_W, g1 * MIX_W)
        pm_ref[:, cs] = jnp.dot(hb, wm_ref[:, cs], preferred_element_type=F32)

    ps_ref[...] = jnp.dot(hb, ws_ref[...], preferred_element_type=F32)
    proj(0, 4)
    for bi in range(bb):
        gcs_o[bi] = pm_ref[bi * c + lv - (CONV_W - 1):bi * c + lv, 0:3 * MIX_W]

    r2 = lax.broadcasted_iota(jnp.int32, (c, c), 0)
    s2 = lax.broadcasted_iota(jnp.int32, (c, c), 1)
    causal = r2 >= s2
    strict = r2 > s2
    tri_bf = causal.astype(BF16)
    eye = (r2 == s2).astype(F32)
    rowc = lax.broadcasted_iota(jnp.int32, (c, 1), 0)
    row8c = lax.broadcasted_iota(jnp.int32, (8, 1), 0)
    live = rowc < lv
    lane = lax.broadcasted_iota(jnp.int32, (c, 128), 1)
    sub = min(SUB, c)
    nblk = c // sub
    n_sq = int(math.log2(c)) - 1

    def heads(col0):
        return jnp.stack([pm_ref[rows(bi), col0 + h * HEAD_DIM:col0 + (h + 1) * HEAD_DIM] for bi, h in gh])

    def head_rows(ref, r):
        return jnp.stack([ref[r:r + 1, h * HEAD_DIM:(h + 1) * HEAD_DIM] for _, h in gh])

    def put(col0, val):
        for g, (bi, h) in enumerate(gh):
            om_ref[bi, :, col0 + h * HEAD_DIM:col0 + (h + 1) * HEAD_DIM] = val[g].astype(om_ref.dtype)

    alog = sp_ref[0:1, :]
    bias = sp_ref[1:2, :]
    gates, gates2, cums2, gates2_t, cums2_t, convs = [], [], [], [], [], []
    for bi in range(bb):
        ps = ps_ref[rows(bi), :]
        z = ps + bias
        gt = jnp.where(lane < 4, jax.nn.sigmoid(ps),
                       jnp.where(lane < 8, -jnp.exp(alog) * jax.nn.softplus(z),
                                 jnp.where(lane < 12, z, jax.nn.log_sigmoid(z))))
        gt2 = gt * LOG2E
        cm2 = _cumsum_rows(tri_bf, gt) * LOG2E
        gates.append(gt)
        gates2.append(gt2)
        cums2.append(cm2)
        gates2_t.append(gt2.T)
        cums2_t.append(cm2.T)
        xq = pm_ref[rows(bi), 0:3 * MIX_W]
        prev = [xprev_ref[bi, i:i + 1, :] for i in range(CONV_W - 1)]
        taps = [xq[0:8]]
        rolled = [xq]
        for k in range(1, CONV_W):
            rk = pltpu.roll(xq, k, 0)
            rolled.append(rk)
            t = rk[0:8]
            for i in range(k):
                t = jnp.where(row8c == i, prev[CONV_W - 1 - k + i], t)
            taps.append(t)
        acc = cw_ref[0:1, :] * taps[CONV_W - 1]
        for j in range(1, CONV_W):
            acc = acc + cw_ref[j:j + 1, :] * taps[CONV_W - 1 - j]
        if c > 8:
            rest = cw_ref[0:1, :] * rolled[CONV_W - 1][8:]
            for j in range(1, CONV_W):
                rest = rest + cw_ref[j:j + 1, :] * rolled[CONV_W - 1 - j][8:]
            acc = jnp.concatenate([acc, rest], axis=0)
        convs.append(jax.nn.silu(acc))
        xprev_ref[bi, 0:CONV_W - 1, :] = xq[c - (CONV_W - 1):c, :]

    col = lambda arrs, j: jnp.stack([arrs[bi][:, j + h:j + h + 1] for bi, h in gh])
    row = lambda arrs, j: jnp.stack([arrs[bi][j + h:j + h + 1, :] for bi, h in gh])
    part = lambda p: jnp.stack([convs[bi][:, p * MIX_W + h * HEAD_DIM:p * MIX_W + (h + 1) * HEAD_DIM]
                                for bi, h in gh])


    def gdn():
        q = part(0)
        k = part(1)
        v = part(2)
        q = q * (lax.rsqrt(jnp.sum(q * q, -1, keepdims=True) + EPS) * HEAD_DIM ** -0.5)
        k = k * lax.rsqrt(jnp.sum(k * k, -1, keepdims=True) + EPS)
        beta = col(gates, 0)
        Gc = col(cums2, 4)
        Gr = row(cums2_t, 4)
        decay = jnp.exp2(jnp.where(causal, Gc - Gr, -jnp.inf))
        S = gS_o[...].reshape(G, HEAD_DIM, HEAD_DIM)
        qk2 = jnp.concatenate([q, k], axis=1)
        P = _bmm_nt(qk2, k)
        R = _bmm(qk2, S)
        QK, KK = P[:, :c], P[:, c:]
        QS, KS = R[:, :c], R[:, c:]
        eG = jnp.exp2(Gc)
        rhs = beta * (v - eG * KS)
        Mp = -jnp.where(strict, beta * decay * KK, 0.0)
        Tinv = eye + Mp
        Ms = _split2(Mp)
        yield
        for _ in range(n_sq):
            Mp = _bmm_split(Ms, (Ms[0], None))
            Ms = _split2(Mp)
            Tinv = Tinv + _bmm_split(_split2(Tinv), (Ms[0], None))
            yield
        u = _bmm_split(_split2(Tinv), _split2(rhs))
        yield
        o = eG * QS + _bmm(QK * decay, u)
        Gl = Gc[:, lv - 1:lv, :]
        wl = jnp.where(live, jnp.exp2(Gl - Gc), 0.0)
        gS_o[...] = (jnp.exp2(Gl) * S + _bmm_tn(k * wl, u)).reshape(bb, N_HEADS, HEAD_DIM, HEAD_DIM)
        put(0, _rms(o, head_rows(nrm_ref, 0)) * jax.nn.silu(heads(3 * MIX_W)))

    def mlstm():
        yield
        q = heads(4 * MIX_W)
        k = heads(5 * MIX_W) * HEAD_DIM ** -0.5
        ig_c = col(gates2, 8)
        ig_r = row(gates2_t, 8)
        Fc = col(cums2, 12)
        Fr = row(cums2_t, 12)
        m0 = jnp.stack([mm_o[bi, h:h + 1, 0:1] for bi, h in gh])
        n_row = jnp.stack([mn_o[bi, h:h + 1, :] for bi, h in gh])
        C = mC_o[...].reshape(G, HEAD_DIM, HEAD_DIM)
        logD = jnp.where(causal, Fc - Fr + ig_r, -jnp.inf)
        m02 = m0 * LOG2E
        b = Fc + m02
        mt = jnp.maximum(b, jnp.max(logD, -1, keepdims=True))
        s = _bmm_nt(q, k) * jnp.exp2(logD - mt)
        inter = jnp.exp2(b - mt)
        yield
        v = heads(6 * MIX_W)
        num = _bmm(s, v) + inter * _bmm(q, C)
        den = jnp.sum(s, -1, keepdims=True) + inter * jnp.sum(q * n_row, -1, keepdims=True)
        hm = num / jnp.maximum(jnp.abs(den), jnp.exp2(-mt))
        yield
        mt_l = mt[:, lv - 1:lv, :]
        Fl = Fc[:, lv - 1:lv, :]
        wl = jnp.where(live, jnp.exp2(Fl - Fc + ig_c - mt_l), 0.0)
        d0 = jnp.exp2(Fl + m02 - mt_l)
        kw = k * wl
        mC_o[...] = (d0 * C + _bmm_tn(kw, v)).reshape(bb, N_HEADS, HEAD_DIM, HEAD_DIM)
        n_new = d0 * n_row + jnp.sum(kw, axis=1, keepdims=True)
        m_b = jnp.broadcast_to(mt_l * LN2, (G, 1, HEAD_DIM))
        for g, (bi, h) in enumerate(gh):
            mn_o[bi, h:h + 1, :] = n_new[g]
            mm_o[bi, h:h + 1, :] = m_b[g]
        yield
        put(MIX_W, _rms(hm, head_rows(nrm_ref, 1)) * jax.nn.sigmoid(heads(7 * MIX_W)))

    def hgrn():
        hq = jax.nn.silu(heads(8 * MIX_W))
        lbh = lb_ref[...]
        Gs_l, kk_l = [], []
        for bi in range(bb):
            fg = lbh + (1.0 - lbh) * jax.nn.sigmoid(pm_ref[rows(bi), 9 * MIX_W:10 * MIX_W])
            kk_l.append(1.0 - fg)
            Gs_l.append(_cumsum_rows(tri_bf, jnp.log2(fg)))
        kk = jnp.stack([kk_l[bi][:, h * HEAD_DIM:(h + 1) * HEAD_DIM] for bi, h in gh])
        Gm = jnp.stack([Gs_l[bi][:, h * HEAD_DIM:(h + 1) * HEAD_DIM] for bi, h in gh])
        S = hS_o[...].reshape(G, HEAD_DIM, HEAD_DIM)
        yield
        tiles = lambda a: a.reshape(G * nblk, sub, HEAD_DIM)
        hq_t, kk_t, Gm_t = tiles(hq), tiles(kk), tiles(Gm)
        A = jnp.zeros((G, c, c), F32)
        for dlt in range(sub):
            ks = kk_t if dlt == 0 else pltpu.roll(kk_t, dlt, 1)
            Gs = Gm_t if dlt == 0 else pltpu.roll(Gm_t, dlt, 1)
            d = jnp.sum(hq_t * ks * jnp.exp2(Gm_t - Gs), -1, keepdims=True)
            A = jnp.where((s2 == r2 - dlt) & (jnp.bitwise_and(r2, sub - 1) >= dlt), d.reshape(G, c, 1), A)
            if dlt % 2 == 1:
                yield
        if nblk > 1:
            Gref = jnp.concatenate(
                [jnp.broadcast_to(Gm[:, I * sub:I * sub + 1, :], (G, sub, HEAD_DIM)) for I in range(nblk)], axis=1)
            qt = hq * jnp.exp2(Gm - Gref)
            pieces = [jnp.zeros((G, sub, c), F32)]
            for I in range(1, nblk):
                kt = kk[:, :I * sub] * jnp.exp2(Gm[:, I * sub:I * sub + 1, :] - Gm[:, :I * sub])
                kt = jnp.concatenate([kt, jnp.zeros((G, c - I * sub, HEAD_DIM), F32)], axis=1)
                pieces.append(_bmm_nt(qt[:, I * sub:(I + 1) * sub], kt))
                if I % 2 == 1:
                    yield
            A = jnp.where(s2 < jnp.bitwise_and(r2, -sub), jnp.concatenate(pieces, axis=1), A)
        hi = heads(10 * MIX_W)
        o = _bmm(hq * jnp.exp2(Gm), S) + _bmm(A, hi)
        Gl = Gm[:, lv - 1:lv, :]
        kw = jnp.where(live, kk * jnp.exp2(Gl - Gm), 0.0)
        dcol = jnp.exp2(jnp.swapaxes(jnp.broadcast_to(Gl, (G, 8, HEAD_DIM)), 1, 2)[:, :, 0:1])
        hS_o[...] = (dcol * S + _bmm_tn(kw, hi)).reshape(bb, N_HEADS, HEAD_DIM, HEAD_DIM)
        yield
        put(2 * MIX_W, _rms(o, head_rows(nrm_ref, 2)) * jax.nn.silu(heads(11 * MIX_W)))

    def proj_rest():
        proj(8, 12)
        yield
        proj(4, 8)

    streams = [gdn(), proj_rest(), hgrn(), mlstm()]
    while streams:
        for st in list(streams):
            if next(st, "done") == "done":
                streams.remove(st)


def _mixers(x3, ln, w_main, w_small, gS, gconv, mC, mn, mm, hS, cw, sp, nrm, lb, l, earlier, *, bb, c, lv):
    Bt, Lp, _ = x3.shape
    n_earlier = len(earlier)
    kern = functools.partial(_mix_kernel, bb=bb, c=c, lv=lv, layer=l, n_earlier=n_earlier)
    st4 = pl.BlockSpec((bb, N_HEADS, HEAD_DIM, HEAD_DIM), lambda b, j: (b, 0, 0, 0))
    st3 = pl.BlockSpec((bb, N_HEADS, HEAD_DIM), lambda b, j: (b, 0, 0))
    in4 = pl.BlockSpec((None, bb, N_HEADS, HEAD_DIM, HEAD_DIM), lambda b, j: (l, b, 0, 0, 0))
    in3 = pl.BlockSpec((None, bb, N_HEADS, HEAD_DIM), lambda b, j: (l, b, 0, 0))
    full = lambda a: pl.BlockSpec(a.shape, lambda b, j: (0,) * a.ndim)
    once = dict(pipeline_mode=pl.Buffered(1))
    if n_earlier:
        depth = n_earlier + 1
        out4 = pl.BlockSpec((depth, bb, N_HEADS, HEAD_DIM, HEAD_DIM), lambda b, j: (0, b, 0, 0, 0))
        out3 = pl.BlockSpec((depth, bb, N_HEADS, HEAD_DIM), lambda b, j: (0, b, 0, 0))
        lead = (depth,)
    else:
        out4, out3, lead = st4, st3, ()
    return pl.pallas_call(
        kern,
        grid=(Bt // bb, Lp // c),
        in_specs=[pl.BlockSpec((bb, c, D_MODEL), lambda b, j: (b, j, 0)),
                  full(ln),
                  pl.BlockSpec((None, D_MODEL, N_MIX), lambda b, j: (l, 0, 0), **once),
                  pl.BlockSpec((None, D_MODEL, 128), lambda b, j: (l, 0, 0), **once),
                  in4,
                  pl.BlockSpec((None, bb, CONV_W - 1, 3 * MIX_W), lambda b, j: (l, b, 0, 0)),
                  in4, in3, in3, in4,
                  full(cw), full(sp), full(nrm), full(lb)] + [st4, st4, st3, st3, st4] * n_earlier,
        out_specs=[pl.BlockSpec((bb, c, 3 * MIX_W), lambda b, j: (b, j, 0)),
                   pl.BlockSpec((bb, CONV_W - 1, 3 * MIX_W), lambda b, j: (b, 0, 0)),
                   out4, out4, out3, out3, out4],
        out_shape=[jax.ShapeDtypeStruct((Bt, Lp, 3 * MIX_W), BF16),
                   jax.ShapeDtypeStruct(gconv.shape[1:], F32),
                   jax.ShapeDtypeStruct(lead + gS.shape[1:], F32),
                   jax.ShapeDtypeStruct(lead + mC.shape[1:], F32),
                   jax.ShapeDtypeStruct(lead + mn.shape[1:], F32),
                   jax.ShapeDtypeStruct(lead + mm.shape[1:], F32),
                   jax.ShapeDtypeStruct(lead + hS.shape[1:], F32)],
        scratch_shapes=[pltpu.VMEM((bb, 8, 3 * MIX_W), F32),
                        pltpu.VMEM((bb * c, N_MIX), F32),
                        pltpu.VMEM((bb * c, 128), F32)],
        compiler_params=_params(2),
        name="mixers",
    )(x3, ln, w_main, w_small, gS, gconv, mC, mn, mm, hS, cw, sp, nrm, lb, *[a for e in earlier for a in e])


def _chan_kernel(om_ref, x_ref, st_ref, lnm_ref, wg_ref, wbr_ref, wout_ref, lnf_ref, wup_ref, cw_ref, cb_ref,
                 wd_ref, lno_ref, out_ref, ust_ref, *carry, tm, nseq, tiles_per_seq, st_end, final):
    rows = tm // nseq
    x = x_ref[...]
    hm = _rms(x, lnm_ref[...]).astype(BF16)
    acc = None
    for n in range(N_BRANCH):
        br = jnp.dot(om_ref[:, n * MIX_W:(n + 1) * MIX_W], wbr_ref[n], preferred_element_type=F32)
        pg = jnp.dot(hm, wg_ref[:, n * D_MODEL:(n + 1) * D_MODEL], preferred_element_type=F32)
        t = jax.nn.sigmoid(pg) * br
        acc = t if acc is None else acc + t
    x1 = x + jnp.dot(acc.astype(BF16), wout_ref[...], preferred_element_type=F32)
    hf = _rms(x1, lnf_ref[...]).astype(BF16)

    if tiles_per_seq > 1:
        prev_ref, = carry

        @pl.when(pl.program_id(0) % tiles_per_seq == 0)
        def _():
            prev_ref[:, 0:2, :] = st_ref[...]
    else:
        prev_ref = st_ref

    row8 = lax.broadcasted_iota(jnp.int32, (1, 8, 1), 1)

    def up(col0):
        return jnp.dot(hf, wup_ref[:, col0:col0 + FFN_COLS], preferred_element_type=F32)

    def conv(col0, u_raw):
        cs = slice(col0, col0 + FFN_COLS)
        w0, w1, w2, cb = cw_ref[0:1, cs], cw_ref[1:2, cs], cw_ref[2:3, cs], cb_ref[:, cs]
        uc = u_raw.reshape(nseq, rows, FFN_COLS)
        r1 = pltpu.roll(uc, 1, 1)
        r2 = pltpu.roll(uc, 2, 1)
        p0 = prev_ref[:, 0:1, cs]
        p1 = prev_ref[:, 1:2, cs]
        s1 = jnp.where(row8 == 0, p1, r1[:, 0:8])
        s2 = jnp.where(row8 == 0, p0, jnp.where(row8 == 1, p1, r2[:, 0:8]))
        y = w0 * s2 + w1 * s1 + w2 * uc[:, 0:8] + cb
        if rows > 8:
            y = jnp.concatenate([y, w0 * r2[:, 8:] + w1 * r1[:, 8:] + w2 * uc[:, 8:] + cb], axis=1)
        ust_ref[:, :, cs] = uc[:, st_end - 2:st_end, :]
        if tiles_per_seq > 1:
            prev_ref[:, 0:2, cs] = uc[:, rows - 2:rows, :]
        return y.reshape(tm, FFN_COLS)

    n_chunks = D_FF // FFN_COLS
    out = x1
    nxt = up(0), up(D_FF)
    for jc in range(n_chunks):
        cur = nxt
        if jc + 1 < n_chunks:
            nxt = up((jc + 1) * FFN_COLS), up(D_FF + (jc + 1) * FFN_COLS)
        ua = conv(jc * FFN_COLS, cur[0])
        ub = conv(D_FF + jc * FFN_COLS, cur[1])
        act = (jax.nn.silu(ua) * ub).astype(BF16)
        out = out + jnp.dot(act, wd_ref[jc * FFN_COLS:(jc + 1) * FFN_COLS, :], preferred_element_type=F32)
    out_ref[...] = _rms(out, lno_ref[...]) if final else out


def _chan(om, x, st, W, ln_final, l, *, rows_per_seq, st_end, tm, final):
    T = x.shape[0]
    nseq = max(1, tm // rows_per_seq)
    tiles_per_seq = max(1, rows_per_seq // tm)
    kern = functools.partial(_chan_kernel, tm=tm, nseq=nseq, tiles_per_seq=tiles_per_seq,
                             st_end=st_end, final=final)
    once = dict(pipeline_mode=pl.Buffered(1))
    st_spec = pl.BlockSpec((nseq, FFN_CONV_W - 1, 2 * D_FF), lambda i: (i // tiles_per_seq, 0, 0))
    return pl.pallas_call(
        kern,
        grid=(T // tm,),
        in_specs=[pl.BlockSpec((tm, 3 * MIX_W), lambda i: (i, 0)),
                  pl.BlockSpec((tm, D_MODEL), lambda i: (i, 0)),
                  st_spec,
                  pl.BlockSpec((1, D_MODEL), lambda i: (0, 0)),
                  pl.BlockSpec((None, D_MODEL, N_BRANCH * D_MODEL),
                               lambda i: (l, 0, N_MIX // (N_BRANCH * D_MODEL)), **once),
                  pl.BlockSpec((None, N_BRANCH, MIX_W, D_MODEL), lambda i: (l, 0, 0, 0), **once),
                  pl.BlockSpec((None, D_MODEL, D_MODEL), lambda i: (l, 0, 0), **once),
                  pl.BlockSpec((1, D_MODEL), lambda i: (0, 0)),
                  pl.BlockSpec((None, D_MODEL, 2 * D_FF), lambda i: (l, 0, 0), **once),
                  pl.BlockSpec((FFN_CONV_W, 2 * D_FF), lambda i: (0, 0)),
                  pl.BlockSpec((1, 2 * D_FF), lambda i: (0, 0)),
                  pl.BlockSpec((None, D_FF, D_MODEL), lambda i: (l, 0, 0), **once),
                  pl.BlockSpec((1, D_MODEL), lambda i: (0, 0))],
        out_specs=[pl.BlockSpec((tm, D_MODEL), lambda i: (i, 0)), st_spec],
        out_shape=[jax.ShapeDtypeStruct((T, D_MODEL), F32),
                   jax.ShapeDtypeStruct(st.shape, F32)],
        scratch_shapes=[pltpu.VMEM((1, 8, 2 * D_FF), F32)] if tiles_per_seq > 1 else [],
        compiler_params=_params(1),
        name="chan",
    )(om, x, st, W["ln_mix"][l][None], W["w_main"], W["w_br"], W["w_out"], W["ln_ffn"][l][None], W["w_up"],
      W["fcw"][l], W["fcb"][l][None], W["w_down"], ln_final[None])


def _prep_weights(ln_mix, w_in, gdn_conv_w, gdn_A_log, gdn_dt_bias, gdn_norm, m_ibias, m_fbias,
                  m_norm, lb_all, hgrn_norm, w_br, w_out, ln_ffn, w_up, ffn_conv_w, ffn_conv_b, w_down):
    depth = w_in.shape[0]
    w_main = jnp.concatenate([w_in[:, :, 0:2048], w_in[:, :, 2056:4104], w_in[:, :, 4112:]], axis=2).astype(BF16)
    w_small = jnp.concatenate([w_in[:, :, 2048:2056], w_in[:, :, 4104:4112],
                               jnp.zeros((depth, D_MODEL, 128 - 4 * N_HEADS), F32)], axis=2).astype(BF16)
    z4 = jnp.zeros((depth, N_HEADS), F32)
    pad = jnp.zeros((depth, 128 - 4 * N_HEADS), F32)
    sp = jnp.zeros((depth, 8, 128), F32)
    sp = sp.at[:, 0].set(jnp.concatenate([z4, gdn_A_log, z4, z4, pad], axis=1))
    sp = sp.at[:, 1].set(jnp.concatenate([z4, gdn_dt_bias, m_ibias, m_fbias, pad], axis=1))
    return dict(
        ln_mix=ln_mix, w_main=w_main, w_small=w_small, cw=gdn_conv_w, sp=sp,
        nrm=jnp.stack([gdn_norm, m_norm, hgrn_norm], axis=1), lb=lb_all,
        w_br=w_br.astype(BF16), w_out=w_out.astype(BF16), ln_ffn=ln_ffn,
        w_up=w_up.astype(BF16), fcw=ffn_conv_w, fcb=ffn_conv_b, w_down=w_down.astype(BF16))


def _trunk(x3, states, W, ln_final, *, lv, c, bb, bb_gather, sample):
    Bt, Lp, _ = x3.shape
    T = Bt * Lp
    x = x3.reshape(T, D_MODEL)
    gdn_S, gdn_conv, m_C, m_n, m_m, h_S, ffn_conv = states
    depth = W["w_main"].shape[0]
    mm_in = jnp.broadcast_to(m_m[..., None], m_m.shape + (HEAD_DIM,))
    gconv_new, fconv_new, per_layer = [], [], []
    for l in range(depth):
        last = l == depth - 1
        gather = last and bb_gather is not None and depth > 1
        om, gcs, *st = _mixers(
            x.reshape(Bt, Lp, D_MODEL), W["ln_mix"][l][None], W["w_main"], W["w_small"],
            gdn_S, gdn_conv, m_C, m_n, mm_in, h_S,
            W["cw"][l], W["sp"][l], W["nrm"][l], W["lb"][l][None], l, per_layer if gather else (),
            bb=bb_gather if gather else bb, c=c, lv=lv)
        per_layer.append(st)
        tm = 256
        x, ust = _chan(om.reshape(T, 3 * MIX_W), x, ffn_conv[l], W, ln_final, l, rows_per_seq=Lp,
                       st_end=lv if sample else tm, tm=tm, final=last)
        gconv_new.append(gcs)
        fconv_new.append(ust)
    gS, mC, mn, mm, hS = st if gather else [jnp.stack(s) for s in zip(*per_layer)]
    return x.reshape(Bt, Lp, D_MODEL), (gS, jnp.stack(gconv_new), mC, mn, mm[..., 0], hS, jnp.stack(fconv_new))


def kernel(x_prompt, x_sample, state_gdn_S, state_gdn_conv, state_mlstm_C, state_mlstm_n, state_mlstm_m, state_hgrn_S, state_ffn_conv, ln_mix, w_in, gdn_conv_w, gdn_A_log, gdn_dt_bias, gdn_norm, m_ibias, m_fbias, m_norm, hgrn_lb, hgrn_norm, w_br, w_out, ln_ffn, w_up, ffn_conv_w, ffn_conv_b, w_down, ln_final):
    depth = w_in.shape[0]
    lb_all = jnp.cumsum(jax.nn.softmax(hgrn_lb.astype(F32), axis=0), axis=0)
    lb_all = lb_all - lb_all[0]
    layers = _prep_weights(ln_mix, w_in, gdn_conv_w, gdn_A_log, gdn_dt_bias, gdn_norm, m_ibias,
                           m_fbias, m_norm, lb_all, hgrn_norm, w_br, w_out, ln_ffn, w_up, ffn_conv_w,
                           ffn_conv_b, w_down)

    B, L, _ = x_prompt.shape
    H, Dh = N_HEADS, HEAD_DIM
    zeros = (jnp.zeros((depth, B, H, Dh, Dh), F32), jnp.zeros((depth, B, CONV_W - 1, 3 * MIX_W), F32),
             jnp.zeros((depth, B, H, Dh, Dh), F32), jnp.zeros((depth, B, H, Dh), F32),
             jnp.zeros((depth, B, H), F32), jnp.zeros((depth, B, H, Dh, Dh), F32),
             jnp.zeros((depth, B, FFN_CONV_W - 1, 2 * D_FF), F32))
    y_p, st_p = _trunk(x_prompt, zeros, layers, ln_final, lv=CHUNK, c=CHUNK, bb=4, bb_gather=None, sample=False)

    Bs, Ls, _ = x_sample.shape
    xs = jnp.pad(x_sample, ((0, 0), (0, SAMPLE_PAD - Ls), (0, 0)))
    st_in = (state_gdn_S, state_gdn_conv, state_mlstm_C, state_mlstm_n, state_mlstm_m, state_hgrn_S,
             state_ffn_conv)
    y_s, st_s = _trunk(xs, st_in, layers, ln_final, lv=Ls, c=SAMPLE_PAD, bb=8, bb_gather=4, sample=True)
    return (y_p, y_s[:, :Ls]) + st_p + st_s
```

```python
import functools
import math

import jax
import jax.numpy as jnp
from jax import lax
from jax.experimental import pallas as pl
from jax.experimental.pallas import tpu as pltpu

D_MODEL = 1024
N_HEADS = 4
HEAD_DIM = 128
MIX_W = N_HEADS * HEAD_DIM
N_BRANCH = 3
CONV_W = 4
FFN_CONV_W = 3
D_FF = 2816
CHUNK = 64
EPS = 1e-6
N_MIX = 6144
SAMPLE_PAD = 8
SUB = 8
FFN_COLS = D_FF
LOG2E = 1.4426950408889634
LN2 = 0.6931471805599453

F32 = jnp.float32
BF16 = jnp.bfloat16
VMEM_LIMIT = 56 * 1024 * 1024


def _bmm(a, b):
    return jnp.einsum('gtd,gde->gte', a.astype(BF16), b.astype(BF16), preferred_element_type=F32)


def _bmm_nt(a, b):
    return jnp.einsum('gtd,gsd->gts', a.astype(BF16), b.astype(BF16), preferred_element_type=F32)


def _bmm_tn(a, b):
    return jnp.einsum('gsd,gse->gde', a.astype(BF16), b.astype(BF16), preferred_element_type=F32)


def _split2(a):
    hi = a.astype(BF16)
    return hi, (a - hi.astype(F32)).astype(BF16)


def _bmm_split(a, b):
    a_hi, a_lo = a
    b_hi, b_lo = b
    n = a_hi.shape[1]
    p = jnp.einsum('gtk,gks->gts', jnp.concatenate([a_hi, a_lo], axis=1), b_hi, preferred_element_type=F32)
    if b_lo is None:
        return p[:, :n] + p[:, n:]
    return p[:, :n] + p[:, n:] + jnp.einsum('gtk,gks->gts', a_hi, b_lo, preferred_element_type=F32)


def _cumsum_rows(tri_bf, x):
    n = x.shape[1]
    x1 = x.astype(BF16)
    r1 = x - x1.astype(F32)
    x2 = r1.astype(BF16)
    x3 = (r1 - x2.astype(F32)).astype(BF16)
    p = jnp.dot(tri_bf, jnp.concatenate([x1, x2, x3], axis=1), preferred_element_type=F32)
    return p[:, 0:n] + p[:, n:2 * n] + p[:, 2 * n:3 * n]


def _dot_nt(a, w_t):
    return lax.dot_general(a, w_t, (((1,), (1,)), ((), ())), preferred_element_type=F32)


def _rms(x, g):
    return x * lax.rsqrt(jnp.mean(x * x, -1, keepdims=True) + EPS) * g


def _params(n_axes):
    return pltpu.CompilerParams(dimension_semantics=("arbitrary",) * n_axes,
                                vmem_limit_bytes=VMEM_LIMIT)


def _mix_kernel(x_ref, ln_ref, wm_ref, ws_ref, gS_ref, gconv_ref, mC_ref, mn_ref, mm_ref, hS_ref,
                cw_ref, sp_ref, nrm_ref, lb_ref, *rest, bb, c, lv, layer, n_earlier):
    earlier = [rest[5 * d:5 * d + 5] for d in range(n_earlier)]
    om_ref, gcs_o, *st_o = rest[5 * n_earlier:5 * n_earlier + 7]
    xprev_ref, pm_ref, ps_ref = rest[5 * n_earlier + 7:]
    gS_o, mC_o, mn_o, mm_o, hS_o = [r.at[layer] for r in st_o] if n_earlier else st_o
    G = bb * N_HEADS
    gh = [(bi, h) for bi in range(bb) for h in range(N_HEADS)]
    rows = lambda bi: slice(bi * c, (bi + 1) * c)

    @pl.when(pl.program_id(1) == 0)
    def _():
        gS_o[...] = gS_ref[...]
        mC_o[...] = mC_ref[...]
        mn_o[...] = mn_ref[...]
        mm_o[...] = mm_ref[...]
        hS_o[...] = hS_ref[...]
        xprev_ref[:, 0:CONV_W - 1, :] = gconv_ref[...]
        for d, refs in enumerate(earlier):
            for dst, src in zip(st_o, refs):
                dst[d] = src[...]

    hb = _rms(x_ref[...].reshape(bb * c, D_MODEL), ln_ref[...]).astype(BF16)

    def proj(g0, g1):
        cs = slice(g0 * MIX_W, g1 * MIX_W)
        pm_ref[:, cs] = _dot_nt(hb, wm_ref[cs, :])

    ps_ref[...] = jnp.dot(hb, ws_ref[...], preferred_element_type=F32)
    proj(0, 4)
    for bi in range(bb):
        gcs_o[bi] = pm_ref[bi * c + lv - (CONV_W - 1):bi * c + lv, 0:3 * MIX_W]

    r2 = lax.broadcasted_iota(jnp.int32, (c, c), 0)
    s2 = lax.broadcasted_iota(jnp.int32, (c, c), 1)
    causal = r2 >= s2
    strict = r2 > s2
    tri_bf = causal.astype(BF16)
    eye = (r2 == s2).astype(F32)
    rowc = lax.broadcasted_iota(jnp.int32, (c, 1), 0)
    row8c = lax.broadcasted_iota(jnp.int32, (8, 1), 0)
    live = rowc < lv
    lane = lax.broadcasted_iota(jnp.int32, (c, 128), 1)
    sub = min(SUB, c)
    nblk = c // sub
    n_sq = int(math.log2(c)) - 1

    def heads(col0):
        return jnp.stack([pm_ref[rows(bi), col0 + h * HEAD_DIM:col0 + (h + 1) * HEAD_DIM] for bi, h in gh])

    def head_rows(ref, r):
        return jnp.stack([ref[r:r + 1, h * HEAD_DIM:(h + 1) * HEAD_DIM] for _, h in gh])

    def put(col0, val):
        for g, (bi, h) in enumerate(gh):
            om_ref[bi, :, col0 + h * HEAD_DIM:col0 + (h + 1) * HEAD_DIM] = val[g].astype(om_ref.dtype)

    alog = sp_ref[0:1, :]
    bias = sp_ref[1:2, :]
    gates, gates2, cums2, gates2_t, cums2_t, convs = [], [], [], [], [], []
    for bi in range(bb):
        ps = ps_ref[rows(bi), :]
        z = ps + bias
        gt = jnp.where(lane < 4, jax.nn.sigmoid(ps),
                       jnp.where(lane < 8, -jnp.exp(alog) * jax.nn.softplus(z),
                                 jnp.where(lane < 12, z, jax.nn.log_sigmoid(z))))
        gt2 = gt * LOG2E
        cm2 = _cumsum_rows(tri_bf, gt) * LOG2E
        gates.append(gt)
        gates2.append(gt2)
        cums2.append(cm2)
        gates2_t.append(gt2.T)
        cums2_t.append(cm2.T)
        xq = pm_ref[rows(bi), 0:3 * MIX_W]
        prev = [xprev_ref[bi, i:i + 1, :] for i in range(CONV_W - 1)]
        taps = [xq[0:8]]
        rolled = [xq]
        for k in range(1, CONV_W):
            rk = pltpu.roll(xq, k, 0)
            rolled.append(rk)
            t = rk[0:8]
            for i in range(k):
                t = jnp.where(row8c == i, prev[CONV_W - 1 - k + i], t)
            taps.append(t)
        acc = cw_ref[0:1, :] * taps[CONV_W - 1]
        for j in range(1, CONV_W):
            acc = acc + cw_ref[j:j + 1, :] * taps[CONV_W - 1 - j]
        if c > 8:
            rest = cw_ref[0:1, :] * rolled[CONV_W - 1][8:]
            for j in range(1, CONV_W):
                rest = rest + cw_ref[j:j + 1, :] * rolled[CONV_W - 1 - j][8:]
            acc = jnp.concatenate([acc, rest], axis=0)
        convs.append(jax.nn.silu(acc))
        xprev_ref[bi, 0:CONV_W - 1, :] = xq[c - (CONV_W - 1):c, :]

    col = lambda arrs, j: jnp.stack([arrs[bi][:, j + h:j + h + 1] for bi, h in gh])
    row = lambda arrs, j: jnp.stack([arrs[bi][j + h:j + h + 1, :] for bi, h in gh])
    part = lambda p: jnp.stack([convs[bi][:, p * MIX_W + h * HEAD_DIM:p * MIX_W + (h + 1) * HEAD_DIM]
                                for bi, h in gh])


    def gdn():
        q = part(0)
        k = part(1)
        v = part(2)
        q = q * (lax.rsqrt(jnp.sum(q * q, -1, keepdims=True) + EPS) * HEAD_DIM ** -0.5)
        k = k * lax.rsqrt(jnp.sum(k * k, -1, keepdims=True) + EPS)
        beta = col(gates, 0)
        Gc = col(cums2, 4)
        Gr = row(cums2_t, 4)
        decay = jnp.exp2(jnp.where(causal, Gc - Gr, -jnp.inf))
        S = gS_o[...].reshape(G, HEAD_DIM, HEAD_DIM)
        qk2 = jnp.concatenate([q, k], axis=1)
        P = _bmm_nt(qk2, k)
        R = _bmm(qk2, S)
        QK, KK = P[:, :c], P[:, c:]
        QS, KS = R[:, :c], R[:, c:]
        eG = jnp.exp2(Gc)
        rhs = beta * (v - eG * KS)
        Mp = -jnp.where(strict, beta * decay * KK, 0.0)
        Tinv = eye + Mp
        Ms = _split2(Mp)
        yield
        for _ in range(n_sq):
            Mp = _bmm_split(Ms, (Ms[0], None))
            Ms = _split2(Mp)
            Tinv = Tinv + _bmm_split(_split2(Tinv), (Ms[0], None))
            yield
        u = _bmm_split(_split2(Tinv), _split2(rhs))
        yield
        o = eG * QS + _bmm(QK * decay, u)
        Gl = Gc[:, lv - 1:lv, :]
        wl = jnp.where(live, jnp.exp2(Gl - Gc), 0.0)
        gS_o[...] = (jnp.exp2(Gl) * S + _bmm_tn(k * wl, u)).reshape(bb, N_HEADS, HEAD_DIM, HEAD_DIM)
        put(0, _rms(o, head_rows(nrm_ref, 0)) * jax.nn.silu(heads(3 * MIX_W)))

    def mlstm():
        yield
        q = heads(4 * MIX_W)
        k = heads(5 * MIX_W) * HEAD_DIM ** -0.5
        ig_c = col(gates2, 8)
        ig_r = row(gates2_t, 8)
        Fc = col(cums2, 12)
        Fr = row(cums2_t, 12)
        m0 = jnp.stack([mm_o[bi, h:h + 1, 0:1] for bi, h in gh])
        n_row = jnp.stack([mn_o[bi, h:h + 1, :] for bi, h in gh])
        C = mC_o[...].reshape(G, HEAD_DIM, HEAD_DIM)
        logD = jnp.where(causal, Fc - Fr + ig_r, -jnp.inf)
        m02 = m0 * LOG2E
        b = Fc + m02
        mt = jnp.maximum(b, jnp.max(logD, -1, keepdims=True))
        s = _bmm_nt(q, k) * jnp.exp2(logD - mt)
        inter = jnp.exp2(b - mt)
        yield
        v = heads(6 * MIX_W)
        num = _bmm(s, v) + inter * _bmm(q, C)
        den = jnp.sum(s, -1, keepdims=True) + inter * jnp.sum(q * n_row, -1, keepdims=True)
        hm = num / jnp.maximum(jnp.abs(den), jnp.exp2(-mt))
        yield
        mt_l = mt[:, lv - 1:lv, :]
        Fl = Fc[:, lv - 1:lv, :]
        wl = jnp.where(live, jnp.exp2(Fl - Fc + ig_c - mt_l), 0.0)
        d0 = jnp.exp2(Fl + m02 - mt_l)
        kw = k * wl
        mC_o[...] = (d0 * C + _bmm_tn(kw, v)).reshape(bb, N_HEADS, HEAD_DIM, HEAD_DIM)
        n_new = d0 * n_row + jnp.sum(kw, axis=1, keepdims=True)
        m_b = jnp.broadcast_to(mt_l * LN2, (G, 1, HEAD_DIM))
        for g, (bi, h) in enumerate(gh):
            mn_o[bi, h:h + 1, :] = n_new[g]
            mm_o[bi, h:h + 1, :] = m_b[g]
        yield
        put(MIX_W, _rms(hm, head_rows(nrm_ref, 1)) * jax.nn.sigmoid(heads(7 * MIX_W)))

    def hgrn():
        hq = jax.nn.silu(heads(8 * MIX_W))
        lbh = lb_ref[...]
        Gs_l, kk_l = [], []
        for bi in range(bb):
            fg = lbh + (1.0 - lbh) * jax.nn.sigmoid(pm_ref[rows(bi), 9 * MIX_W:10 * MIX_W])
            kk_l.append(1.0 - fg)
            Gs_l.append(_cumsum_rows(tri_bf, jnp.log2(fg)))
        kk = jnp.stack([kk_l[bi][:, h * HEAD_DIM:(h + 1) * HEAD_DIM] for bi, h in gh])
        Gm = jnp.stack([Gs_l[bi][:, h * HEAD_DIM:(h + 1) * HEAD_DIM] for bi, h in gh])
        S = hS_o[...].reshape(G, HEAD_DIM, HEAD_DIM)
        yield
        tiles = lambda a: a.reshape(G * nblk, sub, HEAD_DIM)
        hq_t, kk_t, Gm_t = tiles(hq), tiles(kk), tiles(Gm)
        A = jnp.zeros((G, c, c), F32)
        for dlt in range(sub):
            ks = kk_t if dlt == 0 else pltpu.roll(kk_t, dlt, 1)
            Gs = Gm_t if dlt == 0 else pltpu.roll(Gm_t, dlt, 1)
            d = jnp.sum(hq_t * ks * jnp.exp2(Gm_t - Gs), -1, keepdims=True)
            A = jnp.where((s2 == r2 - dlt) & (jnp.bitwise_and(r2, sub - 1) >= dlt), d.reshape(G, c, 1), A)
            if dlt % 2 == 1:
                yield
        if nblk > 1:
            Gref = jnp.concatenate(
                [jnp.broadcast_to(Gm[:, I * sub:I * sub + 1, :], (G, sub, HEAD_DIM)) for I in range(nblk)], axis=1)
            qt = hq * jnp.exp2(Gm - Gref)
            pieces = [jnp.zeros((G, sub, c), F32)]
            for I in range(1, nblk):
                kt = kk[:, :I * sub] * jnp.exp2(Gm[:, I * sub:I * sub + 1, :] - Gm[:, :I * sub])
                kt = jnp.concatenate([kt, jnp.zeros((G, c - I * sub, HEAD_DIM), F32)], axis=1)
                pieces.append(_bmm_nt(qt[:, I * sub:(I + 1) * sub], kt))
                if I % 2 == 1:
                    yield
            A = jnp.where(s2 < jnp.bitwise_and(r2, -sub), jnp.concatenate(pieces, axis=1), A)
        hi = heads(10 * MIX_W)
        o = _bmm(hq * jnp.exp2(Gm), S) + _bmm(A, hi)
        Gl = Gm[:, lv - 1:lv, :]
        kw = jnp.where(live, kk * jnp.exp2(Gl - Gm), 0.0)
        dcol = jnp.exp2(jnp.swapaxes(jnp.broadcast_to(Gl, (G, 8, HEAD_DIM)), 1, 2)[:, :, 0:1])
        hS_o[...] = (dcol * S + _bmm_tn(kw, hi)).reshape(bb, N_HEADS, HEAD_DIM, HEAD_DIM)
        yield
        put(2 * MIX_W, _rms(o, head_rows(nrm_ref, 2)) * jax.nn.silu(heads(11 * MIX_W)))

    def proj_rest():
        proj(8, 12)
        yield
        proj(4, 8)

    streams = [gdn(), proj_rest(), hgrn(), mlstm()]
    while streams:
        for st in list(streams):
            if next(st, "done") == "done":
                streams.remove(st)


def _mixers(x3, ln, w_main, w_small, gS, gconv, mC, mn, mm, hS, cw, sp, nrm, lb, l, earlier, *, bb, c, lv):
    Bt, Lp, _ = x3.shape
    n_earlier = len(earlier)
    kern = functools.partial(_mix_kernel, bb=bb, c=c, lv=lv, layer=l, n_earlier=n_earlier)
    st4 = pl.BlockSpec((bb, N_HEADS, HEAD_DIM, HEAD_DIM), lambda b, j: (b, 0, 0, 0))
    st3 = pl.BlockSpec((bb, N_HEADS, HEAD_DIM), lambda b, j: (b, 0, 0))
    in4 = pl.BlockSpec((None, bb, N_HEADS, HEAD_DIM, HEAD_DIM), lambda b, j: (l, b, 0, 0, 0))
    in3 = pl.BlockSpec((None, bb, N_HEADS, HEAD_DIM), lambda b, j: (l, b, 0, 0))
    full = lambda a: pl.BlockSpec(a.shape, lambda b, j: (0,) * a.ndim)
    once = dict(pipeline_mode=pl.Buffered(1))
    if n_earlier:
        depth = n_earlier + 1
        out4 = pl.BlockSpec((depth, bb, N_HEADS, HEAD_DIM, HEAD_DIM), lambda b, j: (0, b, 0, 0, 0))
        out3 = pl.BlockSpec((depth, bb, N_HEADS, HEAD_DIM), lambda b, j: (0, b, 0, 0))
        lead = (depth,)
    else:
        out4, out3, lead = st4, st3, ()
    return pl.pallas_call(
        kern,
        grid=(Bt // bb, Lp // c),
        in_specs=[pl.BlockSpec((bb, c, D_MODEL), lambda b, j: (b, j, 0)),
                  full(ln),
                  pl.BlockSpec((None, N_MIX, D_MODEL), lambda b, j: (l, 0, 0), **once),
                  pl.BlockSpec((None, D_MODEL, 128), lambda b, j: (l, 0, 0), **once),
                  in4,
                  pl.BlockSpec((None, bb, CONV_W - 1, 3 * MIX_W), lambda b, j: (l, b, 0, 0)),
                  in4, in3, in3, in4,
                  full(cw), full(sp), full(nrm), full(lb)] + [st4, st4, st3, st3, st4] * n_earlier,
        out_specs=[pl.BlockSpec((bb, c, 3 * MIX_W), lambda b, j: (b, j, 0)),
                   pl.BlockSpec((bb, CONV_W - 1, 3 * MIX_W), lambda b, j: (b, 0, 0)),
                   out4, out4, out3, out3, out4],
        out_shape=[jax.ShapeDtypeStruct((Bt, Lp, 3 * MIX_W), BF16),
                   jax.ShapeDtypeStruct(gconv.shape[1:], F32),
                   jax.ShapeDtypeStruct(lead + gS.shape[1:], F32),
                   jax.ShapeDtypeStruct(lead + mC.shape[1:], F32),
                   jax.ShapeDtypeStruct(lead + mn.shape[1:], F32),
                   jax.ShapeDtypeStruct(lead + mm.shape[1:], F32),
                   jax.ShapeDtypeStruct(lead + hS.shape[1:], F32)],
        scratch_shapes=[pltpu.VMEM((bb, 8, 3 * MIX_W), F32),
                        pltpu.VMEM((bb * c, N_MIX), F32),
                        pltpu.VMEM((bb * c, 128), F32)],
        compiler_params=_params(2),
        name="mixers",
    )(x3, ln, w_main, w_small, gS, gconv, mC, mn, mm, hS, cw, sp, nrm, lb, *[a for e in earlier for a in e])


def _chan_kernel(om_ref, x_ref, st_ref, lnm_ref, wg_ref, wbr_ref, wout_ref, lnf_ref, wup_ref, cw_ref, cb_ref,
                 wd_ref, lno_ref, out_ref, ust_ref, *carry, tm, nseq, tiles_per_seq, st_end, final):
    rows = tm // nseq
    x = x_ref[...]
    hm = _rms(x, lnm_ref[...]).astype(BF16)
    acc = None
    for n in range(N_BRANCH):
        br = jnp.dot(om_ref[:, n * MIX_W:(n + 1) * MIX_W], wbr_ref[n], preferred_element_type=F32)
        pg = _dot_nt(hm, wg_ref[n * D_MODEL:(n + 1) * D_MODEL, :])
        t = jax.nn.sigmoid(pg) * br
        acc = t if acc is None else acc + t
    x1 = x + jnp.dot(acc.astype(BF16), wout_ref[...], preferred_element_type=F32)
    hf = _rms(x1, lnf_ref[...]).astype(BF16)

    if tiles_per_seq > 1:
        prev_ref, = carry

        @pl.when(pl.program_id(0) % tiles_per_seq == 0)
        def _():
            prev_ref[:, 0:2, :] = st_ref[...]
    else:
        prev_ref = st_ref

    row8 = lax.broadcasted_iota(jnp.int32, (1, 8, 1), 1)

    def up(col0):
        return jnp.dot(hf, wup_ref[:, col0:col0 + FFN_COLS], preferred_element_type=F32)

    def conv(col0, u_raw):
        cs = slice(col0, col0 + FFN_COLS)
        w0, w1, w2, cb = cw_ref[0:1, cs], cw_ref[1:2, cs], cw_ref[2:3, cs], cb_ref[:, cs]
        uc = u_raw.reshape(nseq, rows, FFN_COLS)
        r1 = pltpu.roll(uc, 1, 1)
        r2 = pltpu.roll(uc, 2, 1)
        p0 = prev_ref[:, 0:1, cs]
        p1 = prev_ref[:, 1:2, cs]
        s1 = jnp.where(row8 == 0, p1, r1[:, 0:8])
        s2 = jnp.where(row8 == 0, p0, jnp.where(row8 == 1, p1, r2[:, 0:8]))
        y = w0 * s2 + w1 * s1 + w2 * uc[:, 0:8] + cb
        if rows > 8:
            y = jnp.concatenate([y, w0 * r2[:, 8:] + w1 * r1[:, 8:] + w2 * uc[:, 8:] + cb], axis=1)
        ust_ref[:, :, cs] = uc[:, st_end - 2:st_end, :]
        if tiles_per_seq > 1:
            prev_ref[:, 0:2, cs] = uc[:, rows - 2:rows, :]
        return y.reshape(tm, FFN_COLS)

    n_chunks = D_FF // FFN_COLS
    out = x1
    nxt = up(0), up(D_FF)
    for jc in range(n_chunks):
        cur = nxt
        if jc + 1 < n_chunks:
            nxt = up((jc + 1) * FFN_COLS), up(D_FF + (jc + 1) * FFN_COLS)
        ua = conv(jc * FFN_COLS, cur[0])
        ub = conv(D_FF + jc * FFN_COLS, cur[1])
        act = (jax.nn.silu(ua) * ub).astype(BF16)
        out = out + jnp.dot(act, wd_ref[jc * FFN_COLS:(jc + 1) * FFN_COLS, :], preferred_element_type=F32)
    out_ref[...] = _rms(out, lno_ref[...]) if final else out


def _chan(om, x, st, W, ln_final, l, *, rows_per_seq, st_end, tm, final):
    T = x.shape[0]
    nseq = max(1, tm // rows_per_seq)
    tiles_per_seq = max(1, rows_per_seq // tm)
    kern = functools.partial(_chan_kernel, tm=tm, nseq=nseq, tiles_per_seq=tiles_per_seq,
                             st_end=st_end, final=final)
    once = dict(pipeline_mode=pl.Buffered(1))
    st_spec = pl.BlockSpec((nseq, FFN_CONV_W - 1, 2 * D_FF), lambda i: (i // tiles_per_seq, 0, 0))
    return pl.pallas_call(
        kern,
        grid=(T // tm,),
        in_specs=[pl.BlockSpec((tm, 3 * MIX_W), lambda i: (i, 0)),
                  pl.BlockSpec((tm, D_MODEL), lambda i: (i, 0)),
                  st_spec,
                  pl.BlockSpec((1, D_MODEL), lambda i: (0, 0)),
                  pl.BlockSpec((None, N_BRANCH * D_MODEL, D_MODEL),
                               lambda i: (l, N_MIX // (N_BRANCH * D_MODEL), 0), **once),
                  pl.BlockSpec((None, N_BRANCH, MIX_W, D_MODEL), lambda i: (l, 0, 0, 0), **once),
                  pl.BlockSpec((None, D_MODEL, D_MODEL), lambda i: (l, 0, 0), **once),
                  pl.BlockSpec((1, D_MODEL), lambda i: (0, 0)),
                  pl.BlockSpec((None, D_MODEL, 2 * D_FF), lambda i: (l, 0, 0), **once),
                  pl.BlockSpec((FFN_CONV_W, 2 * D_FF), lambda i: (0, 0)),
                  pl.BlockSpec((1, 2 * D_FF), lambda i: (0, 0)),
                  pl.BlockSpec((None, D_FF, D_MODEL), lambda i: (l, 0, 0), **once),
                  pl.BlockSpec((1, D_MODEL), lambda i: (0, 0))],
        out_specs=[pl.BlockSpec((tm, D_MODEL), lambda i: (i, 0)), st_spec],
        out_shape=[jax.ShapeDtypeStruct((T, D_MODEL), F32),
                   jax.ShapeDtypeStruct(st.shape, F32)],
        scratch_shapes=[pltpu.VMEM((1, 8, 2 * D_FF), F32)] if tiles_per_seq > 1 else [],
        compiler_params=_params(1),
        name="chan",
    )(om, x, st, W["ln_mix"][l][None], W["w_main"], W["w_br"], W["w_out"], W["ln_ffn"][l][None], W["w_up"],
      W["fcw"][l], W["fcb"][l][None], W["w_down"], ln_final[None])


def _prep_weights(ln_mix, w_in, gdn_conv_w, gdn_A_log, gdn_dt_bias, gdn_norm, m_ibias, m_fbias,
                  m_norm, lb_all, hgrn_norm, w_br, w_out, ln_ffn, w_up, ffn_conv_w, ffn_conv_b, w_down):
    depth = w_in.shape[0]
    w_t = jnp.swapaxes(w_in, 1, 2)
    w_main = jnp.concatenate([w_t[:, 0:2048], w_t[:, 2056:4104], w_t[:, 4112:]], axis=1).astype(BF16)
    w_small = jnp.concatenate([w_in[:, :, 2048:2056], w_in[:, :, 4104:4112],
                               jnp.zeros((depth, D_MODEL, 128 - 4 * N_HEADS), F32)], axis=2).astype(BF16)
    z4 = jnp.zeros((depth, N_HEADS), F32)
    pad = jnp.zeros((depth, 128 - 4 * N_HEADS), F32)
    sp = jnp.zeros((depth, 8, 128), F32)
    sp = sp.at[:, 0].set(jnp.concatenate([z4, gdn_A_log, z4, z4, pad], axis=1))
    sp = sp.at[:, 1].set(jnp.concatenate([z4, gdn_dt_bias, m_ibias, m_fbias, pad], axis=1))
    return dict(
        ln_mix=ln_mix, w_main=w_main, w_small=w_small, cw=gdn_conv_w, sp=sp,
        nrm=jnp.stack([gdn_norm, m_norm, hgrn_norm], axis=1), lb=lb_all,
        w_br=w_br.astype(BF16), w_out=w_out.astype(BF16), ln_ffn=ln_ffn,
        w_up=w_up.astype(BF16), fcw=ffn_conv_w, fcb=ffn_conv_b, w_down=w_down.astype(BF16))


def _trunk(x3, states, W, ln_final, *, lv, c, bb, bb_gather, sample):
    Bt, Lp, _ = x3.shape
    T = Bt * Lp
    x = x3.reshape(T, D_MODEL)
    gdn_S, gdn_conv, m_C, m_n, m_m, h_S, ffn_conv = states
    depth = W["w_main"].shape[0]
    mm_in = jnp.broadcast_to(m_m[..., None], m_m.shape + (HEAD_DIM,))
    gconv_new, fconv_new, per_layer = [], [], []
    for l in range(depth):
        last = l == depth - 1
        gather = last and bb_gather is not None and depth > 1
        om, gcs, *st = _mixers(
            x.reshape(Bt, Lp, D_MODEL), W["ln_mix"][l][None], W["w_main"], W["w_small"],
            gdn_S, gdn_conv, m_C, m_n, mm_in, h_S,
            W["cw"][l], W["sp"][l], W["nrm"][l], W["lb"][l][None], l, per_layer if gather else (),
            bb=bb_gather if gather else bb, c=c, lv=lv)
        per_layer.append(st)
        tm = 256
        x, ust = _chan(om.reshape(T, 3 * MIX_W), x, ffn_conv[l], W, ln_final, l, rows_per_seq=Lp,
                       st_end=lv if sample else tm, tm=tm, final=last)
        gconv_new.append(gcs)
        fconv_new.append(ust)
    gS, mC, mn, mm, hS = st if gather else [jnp.stack(s) for s in zip(*per_layer)]
    return x.reshape(Bt, Lp, D_MODEL), (gS, jnp.stack(gconv_new), mC, mn, mm[..., 0], hS, jnp.stack(fconv_new))


def kernel(x_prompt, x_sample, state_gdn_S, state_gdn_conv, state_mlstm_C, state_mlstm_n, state_mlstm_m, state_hgrn_S, state_ffn_conv, ln_mix, w_in, gdn_conv_w, gdn_A_log, gdn_dt_bias, gdn_norm, m_ibias, m_fbias, m_norm, hgrn_lb, hgrn_norm, w_br, w_out, ln_ffn, w_up, ffn_conv_w, ffn_conv_b, w_down, ln_final):
    depth = w_in.shape[0]
    lb_all = jnp.cumsum(jax.nn.softmax(hgrn_lb.astype(F32), axis=0), axis=0)
    lb_all = lb_all - lb_all[0]
    layers = _prep_weights(ln_mix, w_in, gdn_conv_w, gdn_A_log, gdn_dt_bias, gdn_norm, m_ibias,
                           m_fbias, m_norm, lb_all, hgrn_norm, w_br, w_out, ln_ffn, w_up, ffn_conv_w,
                           ffn_conv_b, w_down)

    B, L, _ = x_prompt.shape
    H, Dh = N_HEADS, HEAD_DIM
    zeros = (jnp.zeros((depth, B, H, Dh, Dh), F32), jnp.zeros((depth, B, CONV_W - 1, 3 * MIX_W), F32),
             jnp.zeros((depth, B, H, Dh, Dh), F32), jnp.zeros((depth, B, H, Dh), F32),
             jnp.zeros((depth, B, H), F32), jnp.zeros((depth, B, H, Dh, Dh), F32),
             jnp.zeros((depth, B, FFN_CONV_W - 1, 2 * D_FF), F32))
    y_p, st_p = _trunk(x_prompt, zeros, layers, ln_final, lv=CHUNK, c=CHUNK, bb=4, bb_gather=None, sample=False)

    Bs, Ls, _ = x_sample.shape
    xs = jnp.pad(x_sample, ((0, 0), (0, SAMPLE_PAD - Ls), (0, 0)))
    st_in = (state_gdn_S, state_gdn_conv, state_mlstm_C, state_mlstm_n, state_mlstm_m, state_hgrn_S,
             state_ffn_conv)
    y_s, st_s = _trunk(xs, st_in, layers, ln_final, lv=Ls, c=SAMPLE_PAD, bb=8, bb_gather=4, sample=True)
    return (y_p, y_s[:, :Ls]) + st_p + st_s
```
